```python
import math
import jax, jax.numpy as jnp
from jax import lax
import numpy as np

D_MODEL = 1024
BATCH = 32
SEQ = 256
DEPTH = 4
DEC_BATCH = 2
DEC_SEQ = 1024
PAST_LEN = 512

GRID_W = 64
N_MIXERS = 2
N_NA_LAYERS = (DEPTH + 1) // 2
N_SSM_LAYERS = DEPTH // 2
NA_HEADS = 16
HEAD_DIM = D_MODEL // NA_HEADS
ATTN_SCALE = HEAD_DIM ** -0.5
WIN_R = 8
WIN_C = 16
Q_BLOCK_C = 16
KEY_BLOCK_C = 2 * WIN_C
ATTN_Q_BLOCK = 128
SSM_GROUP = 16
SSM_GROUPS = D_MODEL // SSM_GROUP
SSM_STATE = 64
N_DIR = 2
D_FF = -(-8 * D_MODEL // (3 * 256)) * 256
N_MOD = 6
EPS = 1e-6

kernel_name = "hybrid_natten_s5_diffusion_step"


def rms_norm(x, g):
    xf = x.astype(jnp.float32)
    y = xf * lax.rsqrt(jnp.mean(xf * xf, axis=-1, keepdims=True) + EPS)
    return (y * g.astype(jnp.float32)).astype(x.dtype)


def ada_modulation(cond, w, b):
    m = jax.nn.silu(cond) @ w + b
    return jnp.split(m[..., None, :], N_MOD, axis=-1)


def modulate(h, shift, scale):
    return h * (1.0 + scale) + shift


def swiglu(h, w1, w3, w2):
    return (jax.nn.silu(h @ w1) * (h @ w3)) @ w2


def na_qkv(h, w_qkv, q_gain, k_gain):
    b, n, _ = h.shape
    qkv = (h @ w_qkv).reshape(b, n, 3, NA_HEADS, HEAD_DIM)
    q = rms_norm(qkv[:, :, 0], q_gain)
    k = rms_norm(qkv[:, :, 1], k_gain)
    return q, k, qkv[:, :, 2]


def context_attention(q, k, v):
    b, s, h, d = q.shape
    nblk = s // ATTN_Q_BLOCK
    qb = jnp.moveaxis(q.reshape(b, nblk, ATTN_Q_BLOCK, h, d), 1, 0)

    def one_block(qi):
        sc = jnp.einsum('bqhd,bshd->bhqs', qi, k).astype(jnp.float32) * ATTN_SCALE
        p = jax.nn.softmax(sc, axis=-1).astype(v.dtype)
        return jnp.einsum('bhqs,bshd->bqhd', p, v)

    o = lax.map(one_block, qb)
    return jnp.moveaxis(o, 0, 1).reshape(b, s, h * d)


def neighbourhood_attention(q, k, v, k_ctx, v_ctx, rpb):
    b, n, h, d = q.shape
    rows = n // GRID_W
    kr = min(WIN_R, rows)
    ncb = GRID_W // Q_BLOCK_C
    r = jnp.arange(rows)
    row_start = jnp.clip(r - kr // 2, 0, rows - kr)
    key_rows = row_start[:, None] + jnp.arange(kr)
    cb = jnp.arange(ncb) * Q_BLOCK_C
    key_col0 = jnp.clip(cb - WIN_C // 2, 0, GRID_W - KEY_BLOCK_C)
    key_cols = key_col0[:, None] + jnp.arange(KEY_BLOCK_C)
    q_cols = cb[:, None] + jnp.arange(Q_BLOCK_C)
    col_start = jnp.clip(q_cols - WIN_C // 2, 0, GRID_W - WIN_C)
    rel = key_cols[:, None, :] - col_start[:, :, None]
    in_win = (rel >= 0) & (rel < WIN_C)
    dc = jnp.clip(key_cols[:, None, :] - q_cols[:, :, None], -(WIN_C - 1), WIN_C - 1)
    dr = key_rows - r[:, None]
    bias = rpb[:, dr[:, None, None, :, None] + (WIN_R - 1), dc[None, :, :, None, :] + (WIN_C - 1)]
    bias = jnp.where(in_win[None, None, :, :, None, :], bias.astype(jnp.float32), -jnp.inf)
    k_grid = k.reshape(b, rows, GRID_W, h, d)
    v_grid = v.reshape(b, rows, GRID_W, h, d)
    ridx = key_rows[:, :, None, None]
    cidx = key_cols[None, None, :, :]
    kg = k_grid[:, ridx, cidx]
    vg = v_grid[:, ridx, cidx]
    qg = q.reshape(b, rows, ncb, Q_BLOCK_C, h, d)
    s_loc = jnp.einsum('brnqhd,brknchd->bhrnqkc', qg, kg).astype(jnp.float32) * ATTN_SCALE + bias[None]
    s_ctx = jnp.einsum('brnqhd,bshd->bhrnqs', qg, k_ctx).astype(jnp.float32) * ATTN_SCALE
    n_loc = kr * KEY_BLOCK_C
    scores = jnp.concatenate([s_loc.reshape(s_loc.shape[:5] + (n_loc,)), s_ctx], axis=-1)
    p = jax.nn.softmax(scores, axis=-1).astype(v.dtype)
    p_loc = p[..., :n_loc].reshape(s_loc.shape)
    p_ctx = p[..., n_loc:]
    o = jnp.einsum('bhrnqkc,brknchd->brnqhd', p_loc, vg) + jnp.einsum('bhrnqs,bshd->brnqhd', p_ctx, v_ctx)
    return o.reshape(b, n, h * d)


def s5_discretise(lam_re, lam_im, log_step, b_re, b_im):
    f32 = jnp.float32
    lam_re, lam_im = lam_re.astype(f32), lam_im.astype(f32)
    step = jnp.exp(log_step.astype(f32))[:, None]
    mag = jnp.exp(lam_re * step)
    a_re, a_im = mag * jnp.cos(lam_im * step), mag * jnp.sin(lam_im * step)
    den = lam_re * lam_re + lam_im * lam_im
    nr, ni = a_re - 1.0, a_im
    f_re = (nr * lam_re + ni * lam_im) / den
    f_im = (ni * lam_re - nr * lam_im) / den
    b_re, b_im = b_re.astype(f32), b_im.astype(f32)
    bb_re = f_re[..., None] * b_re - f_im[..., None] * b_im
    bb_im = f_re[..., None] * b_im + f_im[..., None] * b_re
    return a_re, a_im, bb_re, bb_im


def s5_scan(a_re, a_im, bu_re, bu_im, reverse):
    A_re = jnp.broadcast_to(a_re, bu_re.shape)
    A_im = jnp.broadcast_to(a_im, bu_re.shape)

    def combine(e1, e2):
        a1r, a1i, b1r, b1i = e1
        a2r, a2i, b2r, b2i = e2
        return (a2r * a1r - a2i * a1i, a2r * a1i + a2i * a1r,
                a2r * b1r - a2i * b1i + b2r, a2r * b1i + a2i * b1r + b2i)

    _, _, x_re, x_im = lax.associative_scan(combine, (A_re, A_im, bu_re, bu_im), axis=1, reverse=reverse)
    return x_re, x_im


def s5_mixer(u, lam_re, lam_im, log_step, b_re, b_im, c_re, c_im, d_skip, w_glu, h0_re, h0_im, return_state):
    f32 = jnp.float32
    bsz, length, _ = u.shape
    uf = u.astype(f32)
    ug = uf.reshape(bsz, length, SSM_GROUPS, SSM_GROUP)
    y = ug * d_skip.astype(f32).reshape(SSM_GROUPS, SSM_GROUP)
    fin_re, fin_im = [], []
    for dr in range(N_DIR):
        reverse = dr == 1
        a_re, a_im, bb_re, bb_im = s5_discretise(lam_re[dr], lam_im[dr], log_step[dr], b_re[dr], b_im[dr])
        bu_re = jnp.einsum('gpc,blgc->blgp', bb_re, ug)
        bu_im = jnp.einsum('gpc,blgc->blgp', bb_im, ug)
        if h0_re is not None:
            pos = length - 1 if reverse else 0
            s_re = h0_re[:, dr].astype(f32)
            s_im = h0_im[:, dr].astype(f32)
            bu_re = bu_re.at[:, pos].add(a_re * s_re - a_im * s_im)
            bu_im = bu_im.at[:, pos].add(a_re * s_im + a_im * s_re)
        x_re, x_im = s5_scan(a_re, a_im, bu_re, bu_im, reverse)
        y = y + jnp.einsum('gcp,blgp->blgc', c_re[dr].astype(f32), x_re) \
              - jnp.einsum('gcp,blgp->blgc', c_im[dr].astype(f32), x_im)
        if return_state:
            end = 0 if reverse else length - 1
            fin_re.append(x_re[:, end])
            fin_im.append(x_im[:, end])
    z = jax.nn.gelu(y.reshape(bsz, length, D_MODEL)).astype(u.dtype)
    val, gate = jnp.split(z @ w_glu, 2, axis=-1)
    out = val * jax.nn.sigmoid(gate)
    if return_state:
        return out, jnp.stack(fin_re, axis=1).astype(u.dtype), jnp.stack(fin_im, axis=1).astype(u.dtype)
    return out


def setup_inputs(seed: int = 0) -> dict:
    key = jax.random.key(seed)
    ks = jax.random.split(key, 32)
    f32 = jnp.float32

    def nrm(i, shape, s):
        return jax.random.normal(ks[i], shape, f32) * s

    d = D_MODEL
    lam_im_base = jnp.pi * jnp.arange(SSM_STATE, dtype=f32)
    return {
        "x_prompt": nrm(0, (BATCH, SEQ, d), 1.0),
        "x_sample": nrm(1, (DEC_BATCH, DEC_SEQ, d), 1.0),
        "cache_k": nrm(2, (DEC_BATCH, N_NA_LAYERS, PAST_LEN, NA_HEADS, HEAD_DIM), 1.0),
        "cache_v": nrm(3, (DEC_BATCH, N_NA_LAYERS, PAST_LEN, NA_HEADS, HEAD_DIM), 1.0),
        "state_ssm_re": nrm(4, (DEC_BATCH, N_SSM_LAYERS, N_DIR, SSM_GROUPS, SSM_STATE), 0.3),
        "state_ssm_im": nrm(5, (DEC_BATCH, N_SSM_LAYERS, N_DIR, SSM_GROUPS, SSM_STATE), 0.3),
        "c": nrm(6, (DEC_BATCH, d), 1.0),
        "c_ctx": nrm(7, (d,), 1.0),
        "norm_mix": 1.0 + nrm(8, (DEPTH, d), 0.02),
        "norm_ffn": 1.0 + nrm(9, (DEPTH, d), 0.02),
        "ada_w": nrm(10, (DEPTH, d, N_MOD * d), 0.5 * d ** -0.5),
        "ada_b": nrm(11, (DEPTH, N_MOD * d), 0.02),
        "na_w_qkv": nrm(12, (N_NA_LAYERS, d, 3 * d), d ** -0.5),
        "na_w_o": nrm(13, (N_NA_LAYERS, d, d), d ** -0.5),
        "na_q_gain": 1.0 + nrm(14, (N_NA_LAYERS, HEAD_DIM), 0.02),
        "na_k_gain": 1.0 + nrm(15, (N_NA_LAYERS, HEAD_DIM), 0.02),
        "na_rpb": nrm(16, (N_NA_LAYERS, NA_HEADS, 2 * WIN_R - 1, 2 * WIN_C - 1), 0.02),
        "ssm_lambda_re": -0.5 + nrm(17, (N_SSM_LAYERS, N_DIR, SSM_GROUPS, SSM_STATE), 0.01),
        "ssm_lambda_im": lam_im_base + nrm(18, (N_SSM_LAYERS, N_DIR, SSM_GROUPS, SSM_STATE), 0.01),
        "ssm_log_step": jax.random.uniform(ks[19], (N_SSM_LAYERS, N_DIR, SSM_GROUPS), f32,
                                           minval=math.log(1e-3), maxval=math.log(1e-1)),
        "ssm_b_re": nrm(20, (N_SSM_LAYERS, N_DIR, SSM_GROUPS, SSM_STATE, SSM_GROUP), SSM_GROUP ** -0.5),
        "ssm_b_im": nrm(21, (N_SSM_LAYERS, N_DIR, SSM_GROUPS, SSM_STATE, SSM_GROUP), SSM_GROUP ** -0.5),
        "ssm_c_re": nrm(22, (N_SSM_LAYERS, N_DIR, SSM_GROUPS, SSM_GROUP, SSM_STATE), SSM_STATE ** -0.5),
        "ssm_c_im": nrm(23, (N_SSM_LAYERS, N_DIR, SSM_GROUPS, SSM_GROUP, SSM_STATE), SSM_STATE ** -0.5),
        "ssm_d": nrm(24, (N_SSM_LAYERS, d), 1.0),
        "ssm_w_glu": nrm(25, (N_SSM_LAYERS, d, 2 * d), d ** -0.5),
        "ffn_w1": nrm(26, (DEPTH, d, D_FF), d ** -0.5),
        "ffn_w3": nrm(27, (DEPTH, d, D_FF), d ** -0.5),
        "ffn_w2": nrm(28, (DEPTH, D_FF, d), D_FF ** -0.5),
    }


def reference(x_prompt, x_sample, cache_k, cache_v, state_ssm_re, state_ssm_im, c, c_ctx,
              norm_mix, norm_ffn, ada_w, ada_b,
              na_w_qkv, na_w_o, na_q_gain, na_k_gain, na_rpb,
              ssm_lambda_re, ssm_lambda_im, ssm_log_step, ssm_b_re, ssm_b_im, ssm_c_re, ssm_c_im,
              ssm_d, ssm_w_glu, ffn_w1, ffn_w3, ffn_w2):
    xp, xs = x_prompt, x_sample
    new_k, new_v, new_sre, new_sim = [], [], [], []
    for i in range(DEPTH):
        j = i // N_MIXERS
        m_ctx = ada_modulation(c_ctx, ada_w[i], ada_b[i])
        m_lat = ada_modulation(c, ada_w[i], ada_b[i])
        hp = modulate(rms_norm(xp, norm_mix[i]), m_ctx[0], m_ctx[1])
        hs = modulate(rms_norm(xs, norm_mix[i]), m_lat[0], m_lat[1])
        if i % N_MIXERS == 0:
            qp, kp, vp = na_qkv(hp, na_w_qkv[j], na_q_gain[j], na_k_gain[j])
            op = context_attention(qp, kp, vp) @ na_w_o[j]
            new_k.append(kp)
            new_v.append(vp)
            qs, ks_, vs = na_qkv(hs, na_w_qkv[j], na_q_gain[j], na_k_gain[j])
            os_ = neighbourhood_attention(qs, ks_, vs, cache_k[:, j], cache_v[:, j], na_rpb[j]) @ na_w_o[j]
        else:
            op, sre, sim = s5_mixer(hp, ssm_lambda_re[j], ssm_lambda_im[j], ssm_log_step[j],
                                    ssm_b_re[j], ssm_b_im[j], ssm_c_re[j], ssm_c_im[j],
                                    ssm_d[j], ssm_w_glu[j], None, None, True)
            new_sre.append(sre)
            new_sim.append(sim)
            os_ = s5_mixer(hs, ssm_lambda_re[j], ssm_lambda_im[j], ssm_log_step[j],
                           ssm_b_re[j], ssm_b_im[j], ssm_c_re[j], ssm_c_im[j],
                           ssm_d[j], ssm_w_glu[j], state_ssm_re[:, j], state_ssm_im[:, j], False)
        xp = xp + m_ctx[2] * op
        xs = xs + m_lat[2] * os_
        hp = modulate(rms_norm(xp, norm_ffn[i]), m_ctx[3], m_ctx[4])
        hs = modulate(rms_norm(xs, norm_ffn[i]), m_lat[3], m_lat[4])
        xp = xp + m_ctx[5] * swiglu(hp, ffn_w1[i], ffn_w3[i], ffn_w2[i])
        xs = xs + m_lat[5] * swiglu(hs, ffn_w1[i], ffn_w3[i], ffn_w2[i])
    new_cache_k = jnp.stack(new_k, axis=1)
    new_cache_v = jnp.stack(new_v, axis=1)
    new_state_ssm_re = jnp.stack(new_sre, axis=1)
    new_state_ssm_im = jnp.stack(new_sim, axis=1)
    return (xp, xs, new_cache_k, new_cache_v, new_state_ssm_re, new_state_ssm_im)
```

```python
import functools
import math

import jax
import jax.numpy as jnp
from jax import lax
from jax.experimental import pallas as pl
from jax.experimental.pallas import tpu as pltpu

F32 = jnp.float32
BF16 = jnp.bfloat16

EPS = 1e-6
N_MOD = 6
GRID_W = 64
WIN_R = 8
WIN_C = 16
SSM_GROUP = 16
SSM_STATE = 64

SUBLANES = 8
LANES = 128
MXU_DIM = 256
VMEM_LIMIT_BYTES = 56 * 1024 * 1024

HEAD_PAIR = LANES
SSM_CHUNK = MXU_DIM
SCAN_STEPS = 32


def _cparams(*sem):
    return pltpu.CompilerParams(dimension_semantics=sem, vmem_limit_bytes=VMEM_LIMIT_BYTES)


def _single(block_shape, index_map):
    return pl.BlockSpec(block_shape, index_map, pipeline_mode=pl.Buffered(1))


def _norm_mod(x, g, shift, scale):
    ms = jnp.mean(x * x, axis=-1, keepdims=True)
    y = x * lax.rsqrt(ms + EPS) * g
    return y * (1.0 + scale) + shift


def _mod_kernel(cond_ref, w_ref, b_ref, o_ref):
    c = cond_ref[...]
    a = (c * jax.nn.sigmoid(c)).astype(BF16)
    o_ref[...] = jnp.dot(a, w_ref[...].astype(BF16), preferred_element_type=F32) + b_ref[...]


def _modulation(cond8, ada_w, ada_b):
    depth, d, n = ada_w.shape
    tn = n // 4
    return pl.pallas_call(
        _mod_kernel,
        grid=(depth, n // tn),
        in_specs=[
            pl.BlockSpec((SUBLANES, d), lambda i, j: (0, 0)),
            pl.BlockSpec((None, d, tn), lambda i, j: (i, 0, j)),
            pl.BlockSpec((None, 1, tn), lambda i, j: (i, 0, j)),
        ],
        out_specs=pl.BlockSpec((None, SUBLANES, tn), lambda i, j: (i, 0, j)),
        out_shape=jax.ShapeDtypeStruct((depth, SUBLANES, n), F32),
        compiler_params=_cparams("arbitrary", "arbitrary"),
        name="adaln_modulation",
    )(cond8, ada_w, ada_b.reshape(depth, 1, n))


def _qkv_kernel(x_ref, mod_ref, g_ref, w_ref, qg_ref, kg_ref, hm_ref, *out_refs, d, attn_scale, with_cache):
    qp_ref, kp_ref, vp_ref = out_refs[:3]
    h = _norm_mod(x_ref[...], g_ref[...], mod_ref[0:1, :], mod_ref[1:2, :]).astype(BF16)
    qkv = jnp.dot(h, w_ref[...], preferred_element_type=F32)
    q, k, v = qkv[:, :d], qkv[:, d:2 * d], qkv[:, 2 * d:]

    def head_norm(t, gain):
        parts = []
        for c in range(d // MXU_DIM):
            tc = t[:, c * MXU_DIM:(c + 1) * MXU_DIM]
            ms = jnp.dot((tc * tc).astype(BF16), hm_ref[...], preferred_element_type=F32)
            parts.append(tc * lax.rsqrt(ms + EPS))
        return jnp.concatenate(parts, axis=-1) * gain

    q = head_norm(q, qg_ref[...])
    k = head_norm(k, kg_ref[...])
    if with_cache:
        out_refs[3][...] = k
        out_refs[4][...] = v
    qs = (q * attn_scale).astype(BF16)
    kb = k.astype(BF16)
    vb = v.astype(BF16)
    for p in range(d // HEAD_PAIR):
        sl = slice(p * HEAD_PAIR, (p + 1) * HEAD_PAIR)
        qp_ref[p] = qs[:, sl]
        kp_ref[p] = kb[:, sl]
        vp_ref[p] = vb[:, sl]


def _qkv(x, mod, group_of_tile, g, w_qkv, q_gain, k_gain, head_mean, *, tm, head_dim, with_cache):
    n, d = x.shape
    npairs = d // HEAD_PAIR
    pair_spec = pl.BlockSpec((npairs, tm, HEAD_PAIR), lambda i: (0, i, 0))
    out_specs = [pair_spec] * 3
    out_shape = [jax.ShapeDtypeStruct((npairs, n, HEAD_PAIR), BF16)] * 3
    if with_cache:
        out_specs += [pl.BlockSpec((tm, d), lambda i: (i, 0))] * 2
        out_shape += [jax.ShapeDtypeStruct((n, d), F32)] * 2
    return pl.pallas_call(
        functools.partial(_qkv_kernel, d=d, attn_scale=head_dim ** -0.5, with_cache=with_cache),
        grid=(n // tm,),
        in_specs=[
            pl.BlockSpec((tm, d), lambda i: (i, 0)),
            pl.BlockSpec((None, N_MOD, d), lambda i: (group_of_tile(i), 0, 0)),
            pl.BlockSpec((1, d), lambda i: (0, 0)),
            _single((d, 3 * d), lambda i: (0, 0)),
            pl.BlockSpec((1, d), lambda i: (0, 0)),
            pl.BlockSpec((1, d), lambda i: (0, 0)),
            pl.BlockSpec((MXU_DIM, MXU_DIM), lambda i: (0, 0)),
        ],
        out_specs=out_specs,
        out_shape=out_shape,
        compiler_params=_cparams("arbitrary"),
        name="norm_qkv",
    )(x, mod, g, w_qkv, q_gain, k_gain, head_mean)


def _pair_masks():
    lane = lax.broadcasted_iota(jnp.int32, (1, HEAD_PAIR), 1)
    first = lane < HEAD_PAIR // 2
    return first, jnp.logical_not(first)


def _dot_nt(a, b):
    return lax.dot_general(a, b, (((1,), (1,)), ((), ())), preferred_element_type=F32)


def _ctx_attn_kernel(q_ref, k_ref, v_ref, x_ref, mod_ref, wo_ref, o_ref, att_ref, *, npairs):
    first, second = _pair_masks()
    for p in range(npairs):
        qp, kp, vp = q_ref[p], k_ref[p], v_ref[p]
        outs = []
        for msk in (first, second):
            qh = jnp.where(msk, qp, jnp.zeros_like(qp))
            s = _dot_nt(qh, kp)
            m = jnp.max(s, axis=-1, keepdims=True)
            e = jnp.exp(s - m)
            l = jnp.sum(e, axis=-1, keepdims=True)
            outs.append(jnp.dot(e.astype(BF16), vp, preferred_element_type=F32) / l)
        att_ref[:, p * HEAD_PAIR:(p + 1) * HEAD_PAIR] = jnp.where(first, outs[0], outs[1]).astype(BF16)
    o = jnp.dot(att_ref[...], wo_ref[...], preferred_element_type=F32)
    o_ref[...] = x_ref[...] + mod_ref[2:3, :] * o


def _ctx_attention(qp, kp, vp, x, mod, wo, *, seq):
    n, d = x.shape
    npairs = d // HEAD_PAIR
    pair_spec = pl.BlockSpec((npairs, seq, HEAD_PAIR), lambda b: (0, b, 0))
    return pl.pallas_call(
        functools.partial(_ctx_attn_kernel, npairs=npairs),
        grid=(n // seq,),
        in_specs=[
            pair_spec, pair_spec, pair_spec,
            pl.BlockSpec((seq, d), lambda b: (b, 0)),
            pl.BlockSpec((None, N_MOD, d), lambda b: (0, 0, 0)),
            _single((d, d), lambda b: (0, 0)),
        ],
        out_specs=pl.BlockSpec((seq, d), lambda b: (b, 0)),
        out_shape=jax.ShapeDtypeStruct((n, d), F32),
        scratch_shapes=[pltpu.VMEM((seq, d), BF16)],
        compiler_params=_cparams("arbitrary"),
        name="context_attention",
    )(qp, kp, vp, x, mod, wo)


def _na_attn_kernel(q_ref, k_ref, v_ref, ck_ref, cv_ref, strip_ref, x_ref, mod_ref, wo_ref, o_ref,
                    bias_ref, att_ref, *, npairs, rows, q_tile):
    first, second = _pair_masks()
    kr = min(WIN_R, rows)
    n_loc = rows * GRID_W

    bias_ref[...] = jnp.full(bias_ref.shape, -jnp.inf, F32)

    def pair_body(p, carry):
        for hh in range(2):
            for qr in range(rows):
                rs = min(max(qr - kr // 2, 0), rows - kr)
                off = rs - qr + (WIN_R - 1)
                bias_ref[hh, qr * GRID_W:(qr + 1) * GRID_W, rs * GRID_W:(rs + kr) * GRID_W] = (
                    strip_ref[2 * p + hh, :, off * GRID_W:(off + kr) * GRID_W])
        kp, vp, ckp, cvp = k_ref[p], v_ref[p], ck_ref[p], cv_ref[p]
        for qt in range(n_loc // q_tile):
            rsl = slice(qt * q_tile, (qt + 1) * q_tile)
            qp = q_ref[p, rsl, :]
            outs = []
            for hh, msk in enumerate((first, second)):
                qh = jnp.where(msk, qp, jnp.zeros_like(qp))
                s_loc = _dot_nt(qh, kp) + bias_ref[hh, rsl, :]
                s_ctx = _dot_nt(qh, ckp)
                m = jnp.maximum(jnp.max(s_loc, axis=-1, keepdims=True), jnp.max(s_ctx, axis=-1, keepdims=True))
                e_loc = jnp.exp(s_loc - m)
                e_ctx = jnp.exp(s_ctx - m)
                l = jnp.sum(e_loc, axis=-1, keepdims=True) + jnp.sum(e_ctx, axis=-1, keepdims=True)
                o = (jnp.dot(e_loc.astype(BF16), vp, preferred_element_type=F32)
                     + jnp.dot(e_ctx.astype(BF16), cvp, preferred_element_type=F32))
                outs.append(o / l)
            att_ref[p, rsl, :] = jnp.where(first, outs[0], outs[1]).astype(BF16)
        return carry

    lax.fori_loop(0, npairs, pair_body, 0)
    att = jnp.concatenate([att_ref[p] for p in range(npairs)], axis=-1)
    o = jnp.dot(att, wo_ref[...], preferred_element_type=F32)
    o_ref[...] = x_ref[...] + mod_ref[2:3, :] * o


def _na_attention(qp, kp, vp, ckp, cvp, strip, x, mod, wo, *, n_batch):
    n, d = x.shape
    npairs = d // HEAD_PAIR
    n_loc = n // n_batch
    past = ckp.shape[2]
    rows = n_loc // GRID_W
    pair_spec = _single((npairs, n_loc, HEAD_PAIR), lambda b: (0, b, 0))
    ctx_spec = _single((None, npairs, past, HEAD_PAIR), lambda b: (b, 0, 0, 0))
    return pl.pallas_call(
        functools.partial(_na_attn_kernel, npairs=npairs, rows=rows, q_tile=256),
        grid=(n_batch,),
        in_specs=[
            pair_spec, pair_spec, pair_spec, ctx_spec, ctx_spec,
            _single(strip.shape, lambda b: (0, 0, 0)),
            _single((n_loc, d), lambda b: (b, 0)),
            pl.BlockSpec((None, N_MOD, d), lambda b: (1 + b, 0, 0)),
            _single((d, d), lambda b: (0, 0)),
        ],
        out_specs=_single((n_loc, d), lambda b: (b, 0)),
        out_shape=jax.ShapeDtypeStruct((n, d), F32),
        scratch_shapes=[pltpu.VMEM((2, n_loc, n_loc), F32), pltpu.VMEM((npairs, n_loc, HEAD_PAIR), BF16)],
        compiler_params=_cparams("arbitrary"),
        name="neighbourhood_attention",
    )(qp, kp, vp, ckp, cvp, strip, x, mod, wo)


def _rpb_strip(rpb):
    qc = jnp.arange(GRID_W)[:, None]
    kc = jnp.arange(GRID_W)[None, :]
    col_start = jnp.clip(qc - WIN_C // 2, 0, GRID_W - WIN_C)
    in_win = (kc >= col_start) & (kc < col_start + WIN_C)
    dc = jnp.clip(kc - qc, -(WIN_C - 1), WIN_C - 1) + (WIN_C - 1)
    t = rpb[:, :, dc].astype(F32)
    t = jnp.where(in_win[None, None], t, -jnp.inf)
    h, ndr = rpb.shape[0], rpb.shape[1]
    return jnp.transpose(t, (0, 2, 1, 3)).reshape(h, GRID_W, ndr * GRID_W)


def _ffn_kernel(x_ref, mod_ref, g_ref, w1_ref, w3_ref, w2_ref, o_ref, *, n_chunks):
    x = x_ref[...]
    h = _norm_mod(x, g_ref[...], mod_ref[3:4, :], mod_ref[4:5, :]).astype(BF16)

    def chunk(c, acc):
        a = jnp.dot(h, w1_ref[c], preferred_element_type=F32)
        b = jnp.dot(h, w3_ref[c], preferred_element_type=F32)
        t = (a * jax.nn.sigmoid(a) * b).astype(BF16)
        return acc + jnp.dot(t, w2_ref[c], preferred_element_type=F32)

    acc = lax.fori_loop(0, n_chunks, chunk, jnp.zeros(x.shape, F32))
    o_ref[...] = x + mod_ref[5:6, :] * acc


def _ffn(x, mod, group_of_tile, g, w1c, w3c, w2c, *, tm):
    n, d = x.shape
    n_chunks, _, fc = w1c.shape
    return pl.pallas_call(
        functools.partial(_ffn_kernel, n_chunks=n_chunks),
        grid=(n // tm,),
        in_specs=[
            pl.BlockSpec((tm, d), lambda i: (i, 0)),
            pl.BlockSpec((None, N_MOD, d), lambda i: (group_of_tile(i), 0, 0)),
            pl.BlockSpec((1, d), lambda i: (0, 0)),
            _single((n_chunks, d, fc), lambda i: (0, 0, 0)),
            _single((n_chunks, d, fc), lambda i: (0, 0, 0)),
            _single((n_chunks, fc, d), lambda i: (0, 0, 0)),
        ],
        out_specs=pl.BlockSpec((tm, d), lambda i: (i, 0)),
        out_shape=jax.ShapeDtypeStruct((n, d), F32),
        compiler_params=_cparams("arbitrary"),
        name="swiglu_ffn",
    )(x, mod, g, w1c, w3c, w2c)


def _ssm_kernel(x_ref, mod_ref, g_ref, dskip_ref, wb_ref, wc_ref, a_ref, h0_ref, wglu_ref, *rest,
                seq, n_seg, n_chunks, with_state):
    if with_state:
        o_ref, sre_ref, sim_ref, u_ref, y_ref, bu_ref, f_ref, z_ref = rest
    else:
        o_ref, u_ref, y_ref, bu_ref, f_ref, z_ref = rest
    d = x_ref.shape[2]
    ns = bu_ref.shape[1] // 2
    tb = 2 * SUBLANES
    ch_rows = SCAN_STEPS * SUBLANES
    n_tc = seq // SCAN_STEPS
    dk = pl.program_id(1)
    direction = dk // n_chunks
    k = dk % n_chunks
    fwd = direction == 0

    @pl.when(dk == 0)
    def _prepare():
        gain, shift, scale, dskip = g_ref[...], mod_ref[0], mod_ref[1], dskip_ref[...]

        def body(i, carry):
            t0 = pl.multiple_of(i * tb, tb)
            xt = jnp.swapaxes(x_ref[:, pl.ds(t0, tb), :], 0, 1)
            h = _norm_mod(xt, gain, shift, scale).reshape(tb * SUBLANES, d)
            hd = h * dskip
            hb = h.astype(BF16)
            row = pl.multiple_of(i * tb * SUBLANES, tb * SUBLANES)
            for kk in range(n_chunks):
                sl = slice(kk * SSM_CHUNK, (kk + 1) * SSM_CHUNK)
                u_ref[kk, pl.ds(row, tb * SUBLANES), :] = hb[:, sl]
                y_ref[kk, pl.ds(row, tb * SUBLANES), :] = hd[:, sl]
            return carry

        lax.fori_loop(0, seq // tb, body, 0)

    a_re = jnp.broadcast_to(a_ref[:, :ns], (SUBLANES, ns))
    a_im = jnp.broadcast_to(a_ref[:, ns:], (SUBLANES, ns))

    def scan_pass(state, store):
        def chunk(i, carry):
            tc = jnp.where(fwd, i, n_tc - 1 - i)
            r0 = pl.multiple_of(tc * ch_rows, ch_rows)
            bu_ref[...] = jnp.dot(u_ref[k, pl.ds(r0, ch_rows), :], wb_ref[...], preferred_element_type=F32)
            xr, xi = carry
            for j in range(SCAN_STEPS):
                off = pl.multiple_of(jnp.where(fwd, j, SCAN_STEPS - 1 - j) * SUBLANES, SUBLANES)
                bur = bu_ref[pl.ds(off, SUBLANES), :ns]
                bui = bu_ref[pl.ds(off, SUBLANES), ns:]
                xr, xi = a_re * xr - a_im * xi + bur, a_re * xi + a_im * xr + bui
                if store:
                    bu_ref[pl.ds(off, SUBLANES), :ns] = xr
                    bu_ref[pl.ds(off, SUBLANES), ns:] = xi
            if store:
                yc = jnp.dot(bu_ref[...].astype(BF16), wc_ref[...], preferred_element_type=F32)
                y_ref[k, pl.ds(r0, ch_rows), :] = y_ref[k, pl.ds(r0, ch_rows), :] + yc
            return xr, xi

        return lax.fori_loop(0, n_tc, chunk, state)

    if n_seg > 1:
        zero = jnp.zeros((SUBLANES, ns), F32)
        z_re, z_im = scan_pass((zero, zero), store=False)
        z_ref[:, :ns] = z_re
        z_ref[:, ns:] = z_im
        f_ref[...] = h0_ref[...]
        p_re, p_im = a_ref[:, :ns], a_ref[:, ns:]
        for _ in range(int(math.log2(seq))):
            p_re, p_im = p_re * p_re - p_im * p_im, 2.0 * p_re * p_im

        def chain(dst, src):
            f_re, f_im = f_ref[src:src + 1, :ns], f_ref[src:src + 1, ns:]
            f_ref[dst:dst + 1, :ns] = p_re * f_re - p_im * f_im + z_ref[src:src + 1, :ns]
            f_ref[dst:dst + 1, ns:] = p_re * f_im + p_im * f_re + z_ref[src:src + 1, ns:]

        @pl.when(fwd)
        def _chain_fwd():
            for b in range(SUBLANES // n_seg):
                for s in range(1, n_seg):
                    chain(b * n_seg + s, b * n_seg + s - 1)

        @pl.when(jnp.logical_not(fwd))
        def _chain_bwd():
            for b in range(SUBLANES // n_seg):
                for s in range(n_seg - 2, -1, -1):
                    chain(b * n_seg + s, b * n_seg + s + 1)

        start = (f_ref[:, :ns], f_ref[:, ns:])
    else:
        start = (h0_ref[:, :ns], h0_ref[:, ns:])

    fin_re, fin_im = scan_pass(start, store=True)
    if with_state:
        sre_ref[...] = fin_re
        sim_ref[...] = fin_im

    @pl.when(dk == 2 * n_chunks - 1)
    def _finish():
        gate = mod_ref[2]

        def body(i, carry):
            r0 = pl.multiple_of(i * ch_rows, ch_rows)
            y = jnp.concatenate([y_ref[kk, pl.ds(r0, ch_rows), :] for kk in range(n_chunks)], axis=-1)
            z = jax.nn.gelu(y).astype(BF16)
            gl = jnp.dot(z, wglu_ref[...], preferred_element_type=F32)
            out = (gl[:, :d] * jax.nn.sigmoid(gl[:, d:])).reshape(SCAN_STEPS, SUBLANES, d) * gate
            for j in range(SCAN_STEPS // tb):
                t0 = pl.multiple_of(i * SCAN_STEPS + j * tb, tb)
                upd = jnp.swapaxes(out[j * tb:(j + 1) * tb], 0, 1)
                o_ref[:, pl.ds(t0, tb), :] = x_ref[:, pl.ds(t0, tb), :] + upd
            return carry

        lax.fori_loop(0, n_tc, body, 0)


def _ssm_mixer(x, mod8, g, dskip, wb, wc, a, h0, wglu, *, seq, n_seg, with_state):
    n_seq, _, d = x.shape
    n_groups = n_seq // SUBLANES
    n_chunks = d // SSM_CHUNK
    ns = wb.shape[-1] // 2
    grid = (n_groups, 2 * n_chunks)
    rows = SUBLANES * seq
    dk_map = lambda gi, dk: (dk // n_chunks, dk % n_chunks, 0, 0)
    x_spec = _single((SUBLANES, seq, d), lambda gi, dk: (gi, 0, 0))
    out_specs = [x_spec]
    out_shape = [jax.ShapeDtypeStruct(x.shape, F32)]
    seq_map = lambda gi, dk: (dk // n_chunks, dk % n_chunks, gi, 0)
    if with_state:
        st_spec = pl.BlockSpec((None, None, SUBLANES, ns), seq_map)
        out_specs += [st_spec, st_spec]
        out_shape += [jax.ShapeDtypeStruct((2, n_chunks, n_groups * SUBLANES, ns), F32)] * 2
    return pl.pallas_call(
        functools.partial(_ssm_kernel, seq=seq, n_seg=n_seg, n_chunks=n_chunks, with_state=with_state),
        grid=grid,
        in_specs=[
            x_spec,
            pl.BlockSpec((N_MOD, SUBLANES, d), lambda gi, dk: (0, 0, 0)),
            pl.BlockSpec((1, d), lambda gi, dk: (0, 0)),
            pl.BlockSpec((1, d), lambda gi, dk: (0, 0)),
            pl.BlockSpec((None, None, SSM_CHUNK, 2 * ns), dk_map),
            pl.BlockSpec((None, None, 2 * ns, SSM_CHUNK), dk_map),
            pl.BlockSpec((None, None, 1, 2 * ns), dk_map),
            pl.BlockSpec((None, None, SUBLANES, 2 * ns), seq_map),
            _single((d, 2 * d), lambda gi, dk: (0, 0)),
        ],
        out_specs=out_specs,
        out_shape=out_shape,
        scratch_shapes=[
            pltpu.VMEM((n_chunks, rows, SSM_CHUNK), BF16),
            pltpu.VMEM((n_chunks, rows, SSM_CHUNK), F32),
            pltpu.VMEM((SCAN_STEPS * SUBLANES, 2 * ns), F32),
            pltpu.VMEM((SUBLANES, 2 * ns), F32),
            pltpu.VMEM((SUBLANES, 2 * ns), F32),
        ],
        compiler_params=_cparams("arbitrary", "arbitrary"),
        name="s5_mixer",
    )(x, mod8, g, dskip, wb, wc, a, h0, wglu)


def _ssm_params(lam_re, lam_im, log_step, b_re, b_im, c_re, c_im):
    lam_re, lam_im = lam_re.astype(F32), lam_im.astype(F32)
    step = jnp.exp(log_step.astype(F32))[..., None]
    mag = jnp.exp(lam_re * step)
    a_re, a_im = mag * jnp.cos(lam_im * step), mag * jnp.sin(lam_im * step)
    den = lam_re * lam_re + lam_im * lam_im
    nr, ni = a_re - 1.0, a_im
    f_re = (nr * lam_re + ni * lam_im) / den
    f_im = (ni * lam_re - nr * lam_im) / den
    b_re, b_im = b_re.astype(F32), b_im.astype(F32)
    bb_re = f_re[..., None] * b_re - f_im[..., None] * b_im
    bb_im = f_re[..., None] * b_im + f_im[..., None] * b_re
    n_dir, n_groups, p, c = bb_re.shape
    gpc = SSM_CHUNK // c
    kk = n_groups // gpc
    eye = jnp.eye(gpc, dtype=F32)

    def pack_in(bb):
        w = jnp.einsum('dkgpc,gh->dkgchp', bb.reshape(n_dir, kk, gpc, p, c), eye)
        return w.reshape(n_dir, kk, gpc * c, gpc * p)

    def pack_out(cc):
        w = jnp.einsum('dkgcp,gh->dkgphc', cc.astype(F32).reshape(n_dir, kk, gpc, c, p), eye)
        return w.reshape(n_dir, kk, gpc * p, gpc * c)

    wb = jnp.concatenate([pack_in(bb_re), pack_in(bb_im)], axis=-1).astype(BF16)
    wc = jnp.concatenate([pack_out(c_re), -pack_out(c_im)], axis=-2).astype(BF16)
    a = jnp.concatenate([a_re.reshape(n_dir, kk, 1, gpc * p), a_im.reshape(n_dir, kk, 1, gpc * p)], axis=-1)
    return wb, wc, a


def kernel(x_prompt, x_sample, cache_k, cache_v, state_ssm_re, state_ssm_im, c, c_ctx, norm_mix, norm_ffn, ada_w, ada_b, na_w_qkv, na_w_o, na_q_gain, na_k_gain, na_rpb, ssm_lambda_re, ssm_lambda_im, ssm_log_step, ssm_b_re, ssm_b_im, ssm_c_re, ssm_c_im, ssm_d, ssm_w_glu, ffn_w1, ffn_w3, ffn_w2):
    batch, seq, d = x_prompt.shape
    dec_batch, dec_seq, _ = x_sample.shape
    depth = ada_w.shape[0]
    heads, head_dim = cache_k.shape[3], cache_k.shape[4]
    past = cache_k.shape[2]
    npairs = d // HEAD_PAIR
    d_ff = ffn_w1.shape[-1]
    n_ff = d_ff // MXU_DIM
    n_dir = state_ssm_re.shape[2]
    n_seg = dec_seq // seq
    assert seq % (2 * SCAN_STEPS) == 0 and dec_seq % seq == 0 and SUBLANES % n_seg == 0
    assert dec_batch * n_seg == SUBLANES and batch % SUBLANES == 0 and seq & (seq - 1) == 0
    assert d_ff % MXU_DIM == 0 and d % SSM_CHUNK == 0 and head_dim * 2 == HEAD_PAIR

    cond8 = jnp.zeros((SUBLANES, d), F32).at[0].set(c_ctx).at[1:1 + dec_batch].set(c)
    mod = _modulation(cond8, ada_w, ada_b).reshape(depth, SUBLANES, N_MOD, d)

    tm = math.gcd(512, dec_seq)
    prompt_group = lambda i: 0
    sample_group = lambda i: 1 + (i * tm) // dec_seq

    head_mean = jnp.kron(jnp.eye(MXU_DIM // head_dim, dtype=F32),
                         jnp.full((head_dim, head_dim), 1.0 / head_dim, F32)).astype(BF16)

    xp = x_prompt.reshape(batch * seq, d)
    xs = x_sample.reshape(dec_batch * dec_seq, d)
    new_k, new_v, new_sre, new_sim = [], [], [], []
    for i in range(depth):
        j = i // 2
        g_mix = norm_mix[i].reshape(1, d)
        if i % 2 == 0:
            w_qkv = na_w_qkv[j].astype(BF16)
            w_o = na_w_o[j].astype(BF16)
            q_gain = jnp.tile(na_q_gain[j], heads).reshape(1, d)
            k_gain = jnp.tile(na_k_gain[j], heads).reshape(1, d)
            qp, kp, vp, k_new, v_new = _qkv(xp, mod[i], prompt_group, g_mix, w_qkv, q_gain, k_gain, head_mean,
                                            tm=tm, head_dim=head_dim, with_cache=True)
            new_k.append(k_new.reshape(batch, seq, heads, head_dim))
            new_v.append(v_new.reshape(batch, seq, heads, head_dim))
            xp = _ctx_attention(qp, kp, vp, xp, mod[i], w_o, seq=seq)
            qs, ks, vs = _qkv(xs, mod[i], sample_group, g_mix, w_qkv, q_gain, k_gain, head_mean,
                              tm=tm, head_dim=head_dim, with_cache=False)
            ckp = cache_k[:, j].astype(BF16).reshape(dec_batch, past, npairs, HEAD_PAIR).transpose(0, 2, 1, 3)
            cvp = cache_v[:, j].astype(BF16).reshape(dec_batch, past, npairs, HEAD_PAIR).transpose(0, 2, 1, 3)
            xs = _na_attention(qs, ks, vs, ckp, cvp, _rpb_strip(na_rpb[j]), xs, mod[i], w_o, n_batch=dec_batch)
        else:
            wb, wc, a = _ssm_params(ssm_lambda_re[j], ssm_lambda_im[j], ssm_log_step[j], ssm_b_re[j], ssm_b_im[j],
                                    ssm_c_re[j], ssm_c_im[j])
            dskip = ssm_d[j].reshape(1, d)
            wglu = ssm_w_glu[j].astype(BF16)
            n_chunks = d // SSM_CHUNK
            ns = wb.shape[-1] // 2
            mod_p = jnp.broadcast_to(mod[i, 0][:, None, :], (N_MOD, SUBLANES, d))
            h0_p = jnp.zeros((n_dir, n_chunks, batch, 2 * ns), F32)
            xp, sre, sim = _ssm_mixer(xp.reshape(batch, seq, d), mod_p, g_mix, dskip, wb, wc, a, h0_p, wglu,
                                      seq=seq, n_seg=1, with_state=True)
            xp = xp.reshape(batch * seq, d)
            new_sre.append(sre.transpose(2, 0, 1, 3).reshape(batch, n_dir, d // SSM_GROUP, SSM_STATE))
            new_sim.append(sim.transpose(2, 0, 1, 3).reshape(batch, n_dir, d // SSM_GROUP, SSM_STATE))
            mod_s = jnp.repeat(mod[i, 1:1 + dec_batch], n_seg, axis=0).transpose(1, 0, 2)
            s_re = state_ssm_re[:, j].astype(F32).reshape(dec_batch, n_dir, n_chunks, ns)
            s_im = state_ssm_im[:, j].astype(F32).reshape(dec_batch, n_dir, n_chunks, ns)
            s0 = jnp.concatenate([s_re, s_im], axis=-1)
            h0_s = jnp.zeros((dec_batch, n_seg, n_dir, n_chunks, 2 * ns), F32)
            h0_s = h0_s.at[:, 0, 0].set(s0[:, 0]).at[:, n_seg - 1, 1].set(s0[:, 1])
            h0_s = h0_s.reshape(dec_batch * n_seg, n_dir, n_chunks, 2 * ns).transpose(1, 2, 0, 3)
            (xs,) = _ssm_mixer(xs.reshape(dec_batch * n_seg, seq, d), mod_s, g_mix, dskip, wb, wc, a, h0_s, wglu,
                               seq=seq, n_seg=n_seg, with_state=False)
            xs = xs.reshape(dec_batch * dec_seq, d)
        g_ffn = norm_ffn[i].reshape(1, d)
        w1c = ffn_w1[i].astype(BF16).reshape(d, n_ff, MXU_DIM).transpose(1, 0, 2)
        w3c = ffn_w3[i].astype(BF16).reshape(d, n_ff, MXU_DIM).transpose(1, 0, 2)
        w2c = ffn_w2[i].astype(BF16).reshape(n_ff, MXU_DIM, d)
        xp = _ffn(xp, mod[i], prompt_group, g_ffn, w1c, w3c, w2c, tm=tm)
        xs = _ffn(xs, mod[i], sample_group, g_ffn, w1c, w3c, w2c, tm=tm)
    return (xp.reshape(batch, seq, d), xs.reshape(dec_batch, dec_seq, d),
            jnp.stack(new_k, axis=1), jnp.stack(new_v, axis=1),
            jnp.stack(new_sre, axis=1), jnp.stack(new_sim, axis=1))
```

```python
import functools
import math

import jax
import jax.numpy as jnp
import numpy as np
from jax import lax
from jax.experimental import pallas as pl
from jax.experimental.pallas import tpu as pltpu

F32 = jnp.float32
BF16 = jnp.bfloat16

EPS = 1e-6
N_MOD = 6
GRID_W = 64
WIN_R = 8
WIN_C = 16
SSM_GROUP = 16
SSM_STATE = 64

SUBLANES = 8
LANES = 128
MXU_DIM = 256
VMEM_LIMIT_BYTES = 56 * 1024 * 1024

HEAD_PAIR = LANES
SSM_CHUNK = MXU_DIM
SCAN_STEPS = 32


def _cparams(*sem):
    return pltpu.CompilerParams(dimension_semantics=sem, vmem_limit_bytes=VMEM_LIMIT_BYTES)


def _single(block_shape, index_map):
    return pl.BlockSpec(block_shape, index_map, pipeline_mode=pl.Buffered(1))


def _norm_mod(x, g, shift, scale):
    ms = jnp.mean(x * x, axis=-1, keepdims=True)
    y = x * lax.rsqrt(ms + EPS) * g
    return y * (1.0 + scale) + shift


def _mod_kernel(cond_ref, w_ref, b_ref, o_ref):
    c = cond_ref[...]
    a = (c * jax.nn.sigmoid(c)).astype(BF16)
    o_ref[...] = jnp.dot(a, w_ref[...].astype(BF16), preferred_element_type=F32) + b_ref[...]


def _modulation(cond8, ada_w, ada_b):
    depth, d, n = ada_w.shape
    tn = n // 4
    return pl.pallas_call(
        _mod_kernel,
        grid=(depth, n // tn),
        in_specs=[
            pl.BlockSpec((SUBLANES, d), lambda i, j: (0, 0)),
            pl.BlockSpec((None, d, tn), lambda i, j: (i, 0, j)),
            pl.BlockSpec((None, 1, tn), lambda i, j: (i, 0, j)),
        ],
        out_specs=pl.BlockSpec((None, SUBLANES, tn), lambda i, j: (i, 0, j)),
        out_shape=jax.ShapeDtypeStruct((depth, SUBLANES, n), F32),
        compiler_params=_cparams("arbitrary", "arbitrary"),
        name="adaln_modulation",
    )(cond8, ada_w, ada_b.reshape(depth, 1, n))


def _qkv_kernel(x_ref, mod_ref, g_ref, w_ref, qg_ref, kg_ref, hm_ref, *out_refs, d, attn_scale, with_cache):
    qp_ref, kp_ref, vp_ref = out_refs[:3]
    h = _norm_mod(x_ref[...], g_ref[...], mod_ref[0:1, :], mod_ref[1:2, :]).astype(BF16)
    qkv = jnp.dot(h, w_ref[...], preferred_element_type=F32)
    q, k, v = qkv[:, :d], qkv[:, d:2 * d], qkv[:, 2 * d:]

    def head_norm(t, gain):
        parts = []
        for c in range(d // MXU_DIM):
            tc = t[:, c * MXU_DIM:(c + 1) * MXU_DIM]
            ms = jnp.dot((tc * tc).astype(BF16), hm_ref[...], preferred_element_type=F32)
            parts.append(tc * lax.rsqrt(ms + EPS))
        return jnp.concatenate(parts, axis=-1) * gain

    q = head_norm(q, qg_ref[...])
    k = head_norm(k, kg_ref[...])
    if with_cache:
        out_refs[3][...] = k
        out_refs[4][...] = v
    qs = (q * attn_scale).astype(BF16)
    kb = k.astype(BF16)
    vb = v.astype(BF16)
    for p in range(d // HEAD_PAIR):
        sl = slice(p * HEAD_PAIR, (p + 1) * HEAD_PAIR)
        qp_ref[p] = qs[:, sl]
        kp_ref[p] = kb[:, sl]
        vp_ref[p] = vb[:, sl]


def _qkv(x, mod, group_of_tile, g, w_qkv, layer, q_gain, k_gain, head_mean, *, tm, head_dim, with_cache):
    n, d = x.shape
    npairs = d // HEAD_PAIR
    pair_spec = pl.BlockSpec((npairs, tm, HEAD_PAIR), lambda i: (0, i, 0))
    out_specs = [pair_spec] * 3
    out_shape = [jax.ShapeDtypeStruct((npairs, n, HEAD_PAIR), BF16)] * 3
    if with_cache:
        out_specs += [pl.BlockSpec((tm, d), lambda i: (i, 0))] * 2
        out_shape += [jax.ShapeDtypeStruct((n, d), F32)] * 2
    return pl.pallas_call(
        functools.partial(_qkv_kernel, d=d, attn_scale=head_dim ** -0.5, with_cache=with_cache),
        grid=(n // tm,),
        in_specs=[
            pl.BlockSpec((tm, d), lambda i: (i, 0)),
            pl.BlockSpec((None, N_MOD, d), lambda i: (group_of_tile(i), 0, 0)),
            pl.BlockSpec((1, d), lambda i: (0, 0)),
            _single((None, d, 3 * d), lambda i: (layer, 0, 0)),
            pl.BlockSpec((1, d), lambda i: (0, 0)),
            pl.BlockSpec((1, d), lambda i: (0, 0)),
            pl.BlockSpec((MXU_DIM, MXU_DIM), lambda i: (0, 0)),
        ],
        out_specs=out_specs,
        out_shape=out_shape,
        compiler_params=_cparams("arbitrary"),
        name="norm_qkv",
    )(x, mod, g, w_qkv, q_gain, k_gain, head_mean)


def _pair_masks():
    lane = lax.broadcasted_iota(jnp.int32, (1, HEAD_PAIR), 1)
    first = lane < HEAD_PAIR // 2
    return first, jnp.logical_not(first)


def _dot_nt(a, b):
    return lax.dot_general(a, b, (((1,), (1,)), ((), ())), preferred_element_type=F32)


def _ctx_attn_kernel(q_ref, k_ref, v_ref, x_ref, mod_ref, wo_ref, o_ref, att_ref, *, npairs):
    first, second = _pair_masks()
    for p in range(npairs):
        qp, kp, vp = q_ref[p], k_ref[p], v_ref[p]
        outs = []
        for msk in (first, second):
            qh = jnp.where(msk, qp, jnp.zeros_like(qp))
            s = _dot_nt(qh, kp)
            m = jnp.max(s, axis=-1, keepdims=True)
            e = jnp.exp(s - m)
            l = jnp.sum(e, axis=-1, keepdims=True)
            outs.append(jnp.dot(e.astype(BF16), vp, preferred_element_type=F32) / l)
        att_ref[:, p * HEAD_PAIR:(p + 1) * HEAD_PAIR] = jnp.where(first, outs[0], outs[1]).astype(BF16)
    o = jnp.dot(att_ref[...], wo_ref[...], preferred_element_type=F32)
    o_ref[...] = x_ref[...] + mod_ref[2:3, :] * o


def _ctx_attention(qp, kp, vp, x, mod, wo, layer, *, seq):
    n, d = x.shape
    npairs = d // HEAD_PAIR
    pair_spec = pl.BlockSpec((npairs, seq, HEAD_PAIR), lambda b: (0, b, 0))
    return pl.pallas_call(
        functools.partial(_ctx_attn_kernel, npairs=npairs),
        grid=(n // seq,),
        in_specs=[
            pair_spec, pair_spec, pair_spec,
            pl.BlockSpec((seq, d), lambda b: (b, 0)),
            pl.BlockSpec((None, N_MOD, d), lambda b: (0, 0, 0)),
            _single((None, d, d), lambda b: (layer, 0, 0)),
        ],
        out_specs=pl.BlockSpec((seq, d), lambda b: (b, 0)),
        out_shape=jax.ShapeDtypeStruct((n, d), F32),
        scratch_shapes=[pltpu.VMEM((seq, d), BF16)],
        compiler_params=_cparams("arbitrary"),
        name="context_attention",
    )(qp, kp, vp, x, mod, wo)


def _na_attn_kernel(q_ref, k_ref, v_ref, ck_ref, cv_ref, strip_ref, x_ref, mod_ref, wo_ref, o_ref,
                    bias_ref, att_ref, *, npairs, rows, q_tile):
    first, second = _pair_masks()
    kr = min(WIN_R, rows)
    n_loc = rows * GRID_W

    bias_ref[...] = jnp.full(bias_ref.shape, -jnp.inf, F32)

    def pair_body(p, carry):
        for hh in range(2):
            for qr in range(rows):
                rs = min(max(qr - kr // 2, 0), rows - kr)
                off = rs - qr + (WIN_R - 1)
                bias_ref[hh, qr * GRID_W:(qr + 1) * GRID_W, rs * GRID_W:(rs + kr) * GRID_W] = (
                    strip_ref[2 * p + hh, :, off * GRID_W:(off + kr) * GRID_W])
        kp, vp, ckp, cvp = k_ref[p], v_ref[p], ck_ref[p], cv_ref[p]
        for qt in range(n_loc // q_tile):
            rsl = slice(qt * q_tile, (qt + 1) * q_tile)
            qp = q_ref[p, rsl, :]
            outs = []
            for hh, msk in enumerate((first, second)):
                qh = jnp.where(msk, qp, jnp.zeros_like(qp))
                s_loc = _dot_nt(qh, kp) + bias_ref[hh, rsl, :]
                s_ctx = _dot_nt(qh, ckp)
                m = jnp.maximum(jnp.max(s_loc, axis=-1, keepdims=True), jnp.max(s_ctx, axis=-1, keepdims=True))
                e_loc = jnp.exp(s_loc - m)
                e_ctx = jnp.exp(s_ctx - m)
                l = jnp.sum(e_loc, axis=-1, keepdims=True) + jnp.sum(e_ctx, axis=-1, keepdims=True)
                o = (jnp.dot(e_loc.astype(BF16), vp, preferred_element_type=F32)
                     + jnp.dot(e_ctx.astype(BF16), cvp, preferred_element_type=F32))
                outs.append(o / l)
            att_ref[p, rsl, :] = jnp.where(first, outs[0], outs[1]).astype(BF16)
        return carry

    lax.fori_loop(0, npairs, pair_body, 0)
    att = jnp.concatenate([att_ref[p] for p in range(npairs)], axis=-1)
    o = jnp.dot(att, wo_ref[...], preferred_element_type=F32)
    o_ref[...] = x_ref[...] + mod_ref[2:3, :] * o


def _na_attention(qp, kp, vp, ckp, cvp, strip, x, mod, wo, layer, *, n_batch):
    n, d = x.shape
    npairs = d // HEAD_PAIR
    n_loc = n // n_batch
    past = ckp.shape[3]
    rows = n_loc // GRID_W
    pair_spec = _single((npairs, n_loc, HEAD_PAIR), lambda b: (0, b, 0))
    ctx_spec = _single((None, None, npairs, past, HEAD_PAIR), lambda b: (layer, b, 0, 0, 0))
    return pl.pallas_call(
        functools.partial(_na_attn_kernel, npairs=npairs, rows=rows, q_tile=256),
        grid=(n_batch,),
        in_specs=[
            pair_spec, pair_spec, pair_spec, ctx_spec, ctx_spec,
            _single((None,) + strip.shape[1:], lambda b: (layer, 0, 0, 0)),
            _single((n_loc, d), lambda b: (b, 0)),
            pl.BlockSpec((None, N_MOD, d), lambda b: (1 + b, 0, 0)),
            _single((None, d, d), lambda b: (layer, 0, 0)),
        ],
        out_specs=_single((n_loc, d), lambda b: (b, 0)),
        out_shape=jax.ShapeDtypeStruct((n, d), F32),
        scratch_shapes=[pltpu.VMEM((2, n_loc, n_loc), F32), pltpu.VMEM((npairs, n_loc, HEAD_PAIR), BF16)],
        compiler_params=_cparams("arbitrary"),
        name="neighbourhood_attention",
    )(qp, kp, vp, ckp, cvp, strip, x, mod, wo)


def _rpb_strip(rpb):
    qc = np.arange(GRID_W)[:, None]
    kc = np.arange(GRID_W)[None, :]
    col_start = np.clip(qc - WIN_C // 2, 0, GRID_W - WIN_C)
    in_win = (kc >= col_start) & (kc < col_start + WIN_C)
    dc = np.clip(kc - qc, -(WIN_C - 1), WIN_C - 1) + (WIN_C - 1)
    sel = (dc[:, :, None] == np.arange(2 * WIN_C - 1)).astype(np.float32)
    t = jnp.einsum('lhdj,qkj->lhqdk', rpb.astype(F32), sel, precision=lax.Precision.HIGHEST)
    t = jnp.where(in_win[None, None, :, None, :], t, -jnp.inf)
    n_layers, h, ndr = rpb.shape[:3]
    return t.reshape(n_layers, h, GRID_W, ndr * GRID_W)


def _ffn_kernel(x_ref, mod_ref, g_ref, w1_ref, w3_ref, w2_ref, o_ref, t_ref):
    x = x_ref[...]
    h = _norm_mod(x, g_ref[...], mod_ref[3:4, :], mod_ref[4:5, :]).astype(BF16)
    for c in range(w1_ref.shape[1] // MXU_DIM):
        sl = slice(c * MXU_DIM, (c + 1) * MXU_DIM)
        a = jnp.dot(h, w1_ref[:, sl], preferred_element_type=F32)
        b = jnp.dot(h, w3_ref[:, sl], preferred_element_type=F32)
        t_ref[:, sl] = (a * jax.nn.sigmoid(a) * b).astype(BF16)
    o_ref[...] = x + mod_ref[5:6, :] * jnp.dot(t_ref[...], w2_ref[...], preferred_element_type=F32)


def _ffn(x, mod, group_of_tile, g, w1, w3, w2, layer, *, tm):
    n, d = x.shape
    d_ff = w1.shape[-1]
    return pl.pallas_call(
        _ffn_kernel,
        grid=(n // tm,),
        in_specs=[
            pl.BlockSpec((tm, d), lambda i: (i, 0)),
            pl.BlockSpec((None, N_MOD, d), lambda i: (group_of_tile(i), 0, 0)),
            pl.BlockSpec((1, d), lambda i: (0, 0)),
            _single((None, d, d_ff), lambda i: (layer, 0, 0)),
            _single((None, d, d_ff), lambda i: (layer, 0, 0)),
            _single((None, d_ff, d), lambda i: (layer, 0, 0)),
        ],
        out_specs=pl.BlockSpec((tm, d), lambda i: (i, 0)),
        out_shape=jax.ShapeDtypeStruct((n, d), F32),
        scratch_shapes=[pltpu.VMEM((tm, d_ff), BF16)],
        compiler_params=_cparams("arbitrary"),
        name="swiglu_ffn",
    )(x, mod, g, w1, w3, w2)


def _ssm_kernel(x_ref, mod_ref, g_ref, dskip_ref, wb_ref, wc_ref, a_ref, h0_ref, wglu_ref, *rest,
                seq, n_seg, n_chunks, with_state):
    if with_state:
        o_ref, sre_ref, sim_ref = rest[:3]
    else:
        o_ref = rest[0]
    u_ref, y_ref, bu_a, bu_b, xs_a, xs_b, f_ref, z_ref = rest[-8:]
    bu_refs, xs_refs = (bu_a, bu_b), (xs_a, xs_b)
    d = x_ref.shape[2]
    ns = bu_a.shape[1] // 2
    tb = 2 * SUBLANES
    ch_rows = SCAN_STEPS * SUBLANES
    n_tc = seq // SCAN_STEPS
    dk = pl.program_id(1)
    direction = dk // n_chunks
    k = dk % n_chunks
    fwd = direction == 0

    @pl.when(dk == 0)
    def _prepare():
        gain, shift, scale, dskip = g_ref[...], mod_ref[0], mod_ref[1], dskip_ref[...]

        def body(i, carry):
            t0 = pl.multiple_of(i * tb, tb)
            xt = jnp.swapaxes(x_ref[:, pl.ds(t0, tb), :], 0, 1)
            h = _norm_mod(xt, gain, shift, scale).reshape(tb * SUBLANES, d)
            hd = h * dskip
            hb = h.astype(BF16)
            row = pl.multiple_of(i * tb * SUBLANES, tb * SUBLANES)
            for kk in range(n_chunks):
                sl = slice(kk * SSM_CHUNK, (kk + 1) * SSM_CHUNK)
                u_ref[kk, pl.ds(row, tb * SUBLANES), :] = hb[:, sl]
                y_ref[kk, pl.ds(row, tb * SUBLANES), :] = hd[:, sl]
            return carry

        lax.fori_loop(0, seq // tb, body, 0)

    a_re = jnp.broadcast_to(a_ref[:, :ns], (SUBLANES, ns))
    a_im = jnp.broadcast_to(a_ref[:, ns:], (SUBLANES, ns))

    def scan_pass(state, store, reverse):
        def rows(i):
            return pl.multiple_of((n_tc - 1 - i if reverse else i) * ch_rows, ch_rows)

        def proj_in(i, b):
            bu_refs[b][...] = jnp.dot(u_ref[k, pl.ds(rows(i), ch_rows), :], wb_ref[...],
                                      preferred_element_type=F32)

        def proj_out(i, b):
            if store:
                r0 = rows(i)
                yc = jnp.dot(xs_refs[b][...], wc_ref[...], preferred_element_type=F32)
                y_ref[k, pl.ds(r0, ch_rows), :] = y_ref[k, pl.ds(r0, ch_rows), :] + yc

        def scan(b, carry):
            xr, xi = carry
            pairs = range(SCAN_STEPS // 2)
            for m in (reversed(pairs) if reverse else pairs):
                new = {}
                for j in ((2 * m + 1, 2 * m) if reverse else (2 * m, 2 * m + 1)):
                    rsl = slice(j * SUBLANES, (j + 1) * SUBLANES)
                    bur, bui = bu_refs[b][rsl, :ns], bu_refs[b][rsl, ns:]
                    xr, xi = a_re * xr - a_im * xi + bur, a_re * xi + a_im * xr + bui
                    new[j] = (xr, xi)
                if store:
                    psl = slice(2 * m * SUBLANES, (2 * m + 2) * SUBLANES)
                    lo, hi = new[2 * m], new[2 * m + 1]
                    xs_refs[b][psl, :ns] = jnp.concatenate([lo[0], hi[0]], axis=0).astype(BF16)
                    xs_refs[b][psl, ns:] = jnp.concatenate([lo[1], hi[1]], axis=0).astype(BF16)
            return xr, xi

        proj_in(0, 0)
        proj_in(1, 1)
        state = scan(0, state)

        def pair(ii, st):
            i = 2 * ii + 1
            proj_out(i - 1, 0)
            proj_in(i + 1, 0)
            st = scan(1, st)
            proj_out(i, 1)
            proj_in(i + 2, 1)
            return scan(0, st)

        state = lax.fori_loop(0, (n_tc - 2) // 2, pair, state)
        proj_out(n_tc - 2, 0)
        state = scan(1, state)
        proj_out(n_tc - 1, 1)
        return state

    def run_direction(reverse):
        if n_seg > 1:
            zero = jnp.zeros((SUBLANES, ns), F32)
            z_re, z_im = scan_pass((zero, zero), False, reverse)
            z_ref[:, :ns] = z_re
            z_ref[:, ns:] = z_im
            f_ref[...] = h0_ref[...]
            p_re, p_im = a_ref[:, :ns], a_ref[:, ns:]
            for _ in range(int(math.log2(seq))):
                p_re, p_im = p_re * p_re - p_im * p_im, 2.0 * p_re * p_im
            for b in range(SUBLANES // n_seg):
                for s in (range(n_seg - 2, -1, -1) if reverse else range(1, n_seg)):
                    dst = b * n_seg + s
                    src = dst + 1 if reverse else dst - 1
                    f_re, f_im = f_ref[src:src + 1, :ns], f_ref[src:src + 1, ns:]
                    f_ref[dst:dst + 1, :ns] = p_re * f_re - p_im * f_im + z_ref[src:src + 1, :ns]
                    f_ref[dst:dst + 1, ns:] = p_re * f_im + p_im * f_re + z_ref[src:src + 1, ns:]
            start = (f_ref[:, :ns], f_ref[:, ns:])
        else:
            start = (h0_ref[:, :ns], h0_ref[:, ns:])
        fin_re, fin_im = scan_pass(start, True, reverse)
        if with_state:
            sre_ref[...] = fin_re
            sim_ref[...] = fin_im

    pl.when(fwd)(functools.partial(run_direction, False))
    pl.when(jnp.logical_not(fwd))(functools.partial(run_direction, True))

    @pl.when(dk == 2 * n_chunks - 1)
    def _finish():
        gate = mod_ref[2]

        def body(i, carry):
            r0 = pl.multiple_of(i * ch_rows, ch_rows)
            y = jnp.concatenate([y_ref[kk, pl.ds(r0, ch_rows), :] for kk in range(n_chunks)], axis=-1)
            z = jax.nn.gelu(y).astype(BF16)
            gl = jnp.dot(z, wglu_ref[...], preferred_element_type=F32)
            out = (gl[:, :d] * jax.nn.sigmoid(gl[:, d:])).reshape(SCAN_STEPS, SUBLANES, d) * gate
            for j in range(SCAN_STEPS // tb):
                t0 = pl.multiple_of(i * SCAN_STEPS + j * tb, tb)
                upd = jnp.swapaxes(out[j * tb:(j + 1) * tb], 0, 1)
                o_ref[:, pl.ds(t0, tb), :] = x_ref[:, pl.ds(t0, tb), :] + upd
            return carry

        lax.fori_loop(0, n_tc, body, 0)


def _ssm_mixer(x, mod8, g, dskip, wb, wc, a, h0, wglu, layer, *, seq, n_seg, with_state):
    n_seq, _, d = x.shape
    n_groups = n_seq // SUBLANES
    n_chunks = d // SSM_CHUNK
    ns = wb.shape[-1] // 2
    grid = (n_groups, 2 * n_chunks)
    rows = SUBLANES * seq
    dk_map = lambda gi, dk: (dk // n_chunks, dk % n_chunks, 0, 0)
    x_spec = _single((SUBLANES, seq, d), lambda gi, dk: (gi, 0, 0))
    out_specs = [x_spec]
    out_shape = [jax.ShapeDtypeStruct(x.shape, F32)]
    seq_map = lambda gi, dk: (dk // n_chunks, dk % n_chunks, gi, 0)
    if with_state:
        st_spec = pl.BlockSpec((None, None, SUBLANES, ns), seq_map)
        out_specs += [st_spec, st_spec]
        out_shape += [jax.ShapeDtypeStruct((2, n_chunks, n_groups * SUBLANES, ns), F32)] * 2
    return pl.pallas_call(
        functools.partial(_ssm_kernel, seq=seq, n_seg=n_seg, n_chunks=n_chunks, with_state=with_state),
        grid=grid,
        in_specs=[
            x_spec,
            pl.BlockSpec((N_MOD, SUBLANES, d), lambda gi, dk: (0, 0, 0)),
            pl.BlockSpec((1, d), lambda gi, dk: (0, 0)),
            pl.BlockSpec((1, d), lambda gi, dk: (0, 0)),
            pl.BlockSpec((None, None, SSM_CHUNK, 2 * ns), dk_map),
            pl.BlockSpec((None, None, 2 * ns, SSM_CHUNK), dk_map),
            pl.BlockSpec((None, None, 1, 2 * ns), dk_map),
            pl.BlockSpec((None, None, SUBLANES, 2 * ns), seq_map),
            _single((None, d, 2 * d), lambda gi, dk: (layer, 0, 0)),
        ],
        out_specs=out_specs,
        out_shape=out_shape,
        scratch_shapes=[
            pltpu.VMEM((n_chunks, rows, SSM_CHUNK), BF16),
            pltpu.VMEM((n_chunks, rows, SSM_CHUNK), F32),
            pltpu.VMEM((SCAN_STEPS * SUBLANES, 2 * ns), F32),
            pltpu.VMEM((SCAN_STEPS * SUBLANES, 2 * ns), F32),
            pltpu.VMEM((SCAN_STEPS * SUBLANES, 2 * ns), BF16),
            pltpu.VMEM((SCAN_STEPS * SUBLANES, 2 * ns), BF16),
            pltpu.VMEM((SUBLANES, 2 * ns), F32),
            pltpu.VMEM((SUBLANES, 2 * ns), F32),
        ],
        compiler_params=_cparams("arbitrary", "arbitrary"),
        name="s5_mixer",
    )(x, mod8, g, dskip, wb, wc, a, h0, wglu)


def _ssm_params(lam_re, lam_im, log_step, b_re, b_im, c_re, c_im):
    lam_re, lam_im = lam_re.astype(F32), lam_im.astype(F32)
    step = jnp.exp(log_step.astype(F32))[..., None]
    mag = jnp.exp(lam_re * step)
    a_re, a_im = mag * jnp.cos(lam_im * step), mag * jnp.sin(lam_im * step)
    den = lam_re * lam_re + lam_im * lam_im
    nr, ni = a_re - 1.0, a_im
    f_re = (nr * lam_re + ni * lam_im) / den
    f_im = (ni * lam_re - nr * lam_im) / den
    b_re, b_im = b_re.astype(F32), b_im.astype(F32)
    bb_re = f_re[..., None] * b_re - f_im[..., None] * b_im
    bb_im = f_re[..., None] * b_im + f_im[..., None] * b_re
    n_dir, n_groups, p, c = bb_re.shape
    gpc = SSM_CHUNK // c
    kk = n_groups // gpc
    eye = jnp.eye(gpc, dtype=F32)

    def pack_in(bb):
        w = jnp.einsum('dkgpc,gh->dkgchp', bb.reshape(n_dir, kk, gpc, p, c), eye)
        return w.reshape(n_dir, kk, gpc * c, gpc * p)

    def pack_out(cc):
        w = jnp.einsum('dkgcp,gh->dkgphc', cc.astype(F32).reshape(n_dir, kk, gpc, c, p), eye)
        return w.reshape(n_dir, kk, gpc * p, gpc * c)

    wb = jnp.concatenate([pack_in(bb_re), pack_in(bb_im)], axis=-1).astype(BF16)
    wc = jnp.concatenate([pack_out(c_re), -pack_out(c_im)], axis=-2).astype(BF16)
    a = jnp.concatenate([a_re.reshape(n_dir, kk, 1, gpc * p), a_im.reshape(n_dir, kk, 1, gpc * p)], axis=-1)
    return wb, wc, a


def kernel(x_prompt, x_sample, cache_k, cache_v, state_ssm_re, state_ssm_im, c, c_ctx, norm_mix, norm_ffn, ada_w, ada_b, na_w_qkv, na_w_o, na_q_gain, na_k_gain, na_rpb, ssm_lambda_re, ssm_lambda_im, ssm_log_step, ssm_b_re, ssm_b_im, ssm_c_re, ssm_c_im, ssm_d, ssm_w_glu, ffn_w1, ffn_w3, ffn_w2):
    batch, seq, d = x_prompt.shape
    dec_batch, dec_seq, _ = x_sample.shape
    depth = ada_w.shape[0]
    heads, head_dim = cache_k.shape[3], cache_k.shape[4]
    past = cache_k.shape[2]
    npairs = d // HEAD_PAIR
    d_ff = ffn_w1.shape[-1]
    n_dir = state_ssm_re.shape[2]
    n_seg = dec_seq // seq
    assert seq % (2 * SCAN_STEPS) == 0 and dec_seq % seq == 0 and SUBLANES % n_seg == 0
    assert dec_batch * n_seg == SUBLANES and batch % SUBLANES == 0 and seq & (seq - 1) == 0
    assert d_ff % MXU_DIM == 0 and d % SSM_CHUNK == 0 and head_dim * 2 == HEAD_PAIR

    cond8 = jnp.zeros((SUBLANES, d), F32).at[0].set(c_ctx).at[1:1 + dec_batch].set(c)
    mod = _modulation(cond8, ada_w, ada_b).reshape(depth, SUBLANES, N_MOD, d)

    tm = math.gcd(512, dec_seq)
    prompt_group = lambda i: 0
    sample_group = lambda i: 1 + (i * tm) // dec_seq

    head_mean = jnp.kron(jnp.eye(MXU_DIM // head_dim, dtype=F32),
                         jnp.full((head_dim, head_dim), 1.0 / head_dim, F32)).astype(BF16)

    w_qkv, w_o, w_glu = na_w_qkv.astype(BF16), na_w_o.astype(BF16), ssm_w_glu.astype(BF16)
    w1, w3, w2 = ffn_w1.astype(BF16), ffn_w3.astype(BF16), ffn_w2.astype(BF16)
    n_na = cache_k.shape[1]
    ckp = cache_k.astype(BF16).reshape(dec_batch, n_na, past, npairs, HEAD_PAIR).transpose(1, 0, 3, 2, 4)
    cvp = cache_v.astype(BF16).reshape(dec_batch, n_na, past, npairs, HEAD_PAIR).transpose(1, 0, 3, 2, 4)
    strip = _rpb_strip(na_rpb)

    xp = x_prompt.reshape(batch * seq, d)
    xs = x_sample.reshape(dec_batch * dec_seq, d)
    new_k, new_v, new_sre, new_sim = [], [], [], []
    for i in range(depth):
        j = i // 2
        g_mix = norm_mix[i].reshape(1, d)
        if i % 2 == 0:
            q_gain = jnp.tile(na_q_gain[j], heads).reshape(1, d)
            k_gain = jnp.tile(na_k_gain[j], heads).reshape(1, d)
            qp, kp, vp, k_new, v_new = _qkv(xp, mod[i], prompt_group, g_mix, w_qkv, j, q_gain, k_gain, head_mean,
                                            tm=tm, head_dim=head_dim, with_cache=True)
            new_k.append(k_new.reshape(batch, seq, heads, head_dim))
            new_v.append(v_new.reshape(batch, seq, heads, head_dim))
            xp = _ctx_attention(qp, kp, vp, xp, mod[i], w_o, j, seq=seq)
            qs, ks, vs = _qkv(xs, mod[i], sample_group, g_mix, w_qkv, j, q_gain, k_gain, head_mean,
                              tm=tm, head_dim=head_dim, with_cache=False)
            xs = _na_attention(qs, ks, vs, ckp, cvp, strip, xs, mod[i], w_o, j, n_batch=dec_batch)
        else:
            wb, wc, a = _ssm_params(ssm_lambda_re[j], ssm_lambda_im[j], ssm_log_step[j], ssm_b_re[j], ssm_b_im[j],
                                    ssm_c_re[j], ssm_c_im[j])
            dskip = ssm_d[j].reshape(1, d)
            n_chunks = d // SSM_CHUNK
            ns = wb.shape[-1] // 2
            mod_p = jnp.broadcast_to(mod[i, 0][:, None, :], (N_MOD, SUBLANES, d))
            h0_p = jnp.zeros((n_dir, n_chunks, batch, 2 * ns), F32)
            xp, sre, sim = _ssm_mixer(xp.reshape(batch, seq, d), mod_p, g_mix, dskip, wb, wc, a, h0_p, w_glu, j,
                                      seq=seq, n_seg=1, with_state=True)
            xp = xp.reshape(batch * seq, d)
            new_sre.append(sre.transpose(2, 0, 1, 3).reshape(batch, n_dir, d // SSM_GROUP, SSM_STATE))
            new_sim.append(sim.transpose(2, 0, 1, 3).reshape(batch, n_dir, d // SSM_GROUP, SSM_STATE))
            mod_s = jnp.repeat(mod[i, 1:1 + dec_batch], n_seg, axis=0).transpose(1, 0, 2)
            s_re = state_ssm_re[:, j].astype(F32).reshape(dec_batch, n_dir, n_chunks, ns)
            s_im = state_ssm_im[:, j].astype(F32).reshape(dec_batch, n_dir, n_chunks, ns)
            s0 = jnp.concatenate([s_re, s_im], axis=-1)
            h0_s = jnp.zeros((dec_batch, n_seg, n_dir, n_chunks, 2 * ns), F32)
            h0_s = h0_s.at[:, 0, 0].set(s0[:, 0]).at[:, n_seg - 1, 1].set(s0[:, 1])
            h0_s = h0_s.reshape(dec_batch * n_seg, n_dir, n_chunks, 2 * ns).transpose(1, 2, 0, 3)
            (xs,) = _ssm_mixer(xs.reshape(dec_batch * n_seg, seq, d), mod_s, g_mix, dskip, wb, wc, a, h0_s, w_glu, j,
                               seq=seq, n_seg=n_seg, with_state=False)
            xs = xs.reshape(dec_batch * dec_seq, d)
        g_ffn = norm_ffn[i].reshape(1, d)
        xp = _ffn(xp, mod[i], prompt_group, g_ffn, w1, w3, w2, i, tm=tm)
        xs = _ffn(xs, mod[i], sample_group, g_ffn, w1, w3, w2, i, tm=tm)
    return (xp.reshape(batch, seq, d), xs.reshape(dec_batch, dec_seq, d),
            jnp.stack(new_k, axis=1), jnp.stack(new_v, axis=1),
            jnp.stack(new_sre, axis=1), jnp.stack(new_sim, axis=1))
```

```python
import functools
import math

import jax
import jax.numpy as jnp
import numpy as np
from jax import lax
from jax.experimental import pallas as pl
from jax.experimental.pallas import tpu as pltpu

F32 = jnp.float32
BF16 = jnp.bfloat16

EPS = 1e-6
N_MOD = 6
GRID_W = 64
WIN_R = 8
WIN_C = 16
SSM_GROUP = 16
SSM_STATE = 64

SUBLANES = 8
LANES = 128
MXU_DIM = 256
VMEM_LIMIT_BYTES = 56 * 1024 * 1024

HEAD_PAIR = LANES
SSM_CHUNK = MXU_DIM
SCAN_STEPS = 32


def _cparams(*sem):
    return pltpu.CompilerParams(dimension_semantics=sem, vmem_limit_bytes=VMEM_LIMIT_BYTES)


def _single(block_shape, index_map):
    return pl.BlockSpec(block_shape, index_map, pipeline_mode=pl.Buffered(1))


def _norm_mod(x, g, shift, scale):
    ms = jnp.mean(x * x, axis=-1, keepdims=True)
    y = x * lax.rsqrt(ms + EPS) * g
    return y * (1.0 + scale) + shift


def _mod_kernel(cond_ref, w_ref, b_ref, o_ref):
    c = cond_ref[...]
    a = (c * jax.nn.sigmoid(c)).astype(BF16)
    o_ref[...] = jnp.dot(a, w_ref[...].astype(BF16), preferred_element_type=F32) + b_ref[...]


def _modulation(cond8, ada_w, ada_b):
    depth, d, n = ada_w.shape
    tn = n // 4
    return pl.pallas_call(
        _mod_kernel,
        grid=(depth, n // tn),
        in_specs=[
            pl.BlockSpec((SUBLANES, d), lambda i, j: (0, 0)),
            pl.BlockSpec((None, d, tn), lambda i, j: (i, 0, j)),
            pl.BlockSpec((None, 1, tn), lambda i, j: (i, 0, j)),
        ],
        out_specs=pl.BlockSpec((None, SUBLANES, tn), lambda i, j: (i, 0, j)),
        out_shape=jax.ShapeDtypeStruct((depth, SUBLANES, n), F32),
        compiler_params=_cparams("arbitrary", "arbitrary"),
        name="adaln_modulation",
    )(cond8, ada_w, ada_b.reshape(depth, 1, n))


def _qkv_kernel(x_ref, mod_ref, g_ref, w_ref, qg_ref, kg_ref, hm_ref, *rest, d, attn_scale, with_cache):
    out_refs = rest[-5:] if with_cache else rest
    qp_ref, kp_ref, vp_ref = out_refs[:3]
    h = _norm_mod(x_ref[...], g_ref[...], mod_ref[0:1, :], mod_ref[1:2, :]).astype(BF16)
    qkv = jnp.dot(h, w_ref[...], preferred_element_type=F32)
    q, k, v = qkv[:, :d], qkv[:, d:2 * d], qkv[:, 2 * d:]

    def head_norm(t, gain):
        parts = []
        for c in range(d // MXU_DIM):
            tc = t[:, c * MXU_DIM:(c + 1) * MXU_DIM]
            ms = jnp.dot((tc * tc).astype(BF16), hm_ref[...], preferred_element_type=F32)
            parts.append(tc * lax.rsqrt(ms + EPS))
        return jnp.concatenate(parts, axis=-1) * gain

    q = head_norm(q, qg_ref[...])
    k = head_norm(k, kg_ref[...])
    if with_cache:
        n_seq, _, seq = out_refs[3].shape
        for s in range(n_seq):
            out_refs[3][s] = k[s * seq:(s + 1) * seq, :].T
            out_refs[4][s] = v[s * seq:(s + 1) * seq, :].T
    qs = (q * attn_scale).astype(BF16)
    kb = k.astype(BF16)
    vb = v.astype(BF16)
    for p in range(d // HEAD_PAIR):
        sl = slice(p * HEAD_PAIR, (p + 1) * HEAD_PAIR)
        qp_ref[p] = qs[:, sl]
        kp_ref[p] = kb[:, sl]
        vp_ref[p] = vb[:, sl]


def _qkv(x, mod, group_of_tile, g, w_qkv, layer, n_layers, q_gain, k_gain, head_mean, cache, *,
         tm, head_dim, seq, with_cache):
    n, d = x.shape
    npairs = d // HEAD_PAIR
    pair_spec = pl.BlockSpec((npairs, tm, HEAD_PAIR), lambda i: (0, i, 0))
    out_specs = [pair_spec] * 3
    out_shape = [jax.ShapeDtypeStruct((npairs, n, HEAD_PAIR), BF16)] * 3
    in_specs = [
        pl.BlockSpec((tm, d), lambda i: (i, 0)),
        pl.BlockSpec((None, N_MOD, d), lambda i: (group_of_tile(i), 0, 0)),
        pl.BlockSpec((1, d), lambda i: (0, 0)),
        _single((None, d, 3 * d), lambda i: (layer, 0, 0)),
        pl.BlockSpec((1, d), lambda i: (0, 0)),
        pl.BlockSpec((1, d), lambda i: (0, 0)),
        pl.BlockSpec((MXU_DIM, MXU_DIM), lambda i: (0, 0)),
    ]
    args = [x, mod, g, w_qkv, q_gain, k_gain, head_mean]
    aliases = {}
    if with_cache:
        out_specs += [pl.BlockSpec((tm // seq, None, d, seq), lambda i: (i, layer, 0, 0))] * 2
        out_shape += [jax.ShapeDtypeStruct((n // seq, n_layers, d, seq), F32)] * 2
        if cache is not None:
            aliases = {len(args): 3, len(args) + 1: 4}
            in_specs += [pl.BlockSpec(memory_space=pl.ANY)] * 2
            args += list(cache)
    return pl.pallas_call(
        functools.partial(_qkv_kernel, d=d, attn_scale=head_dim ** -0.5, with_cache=with_cache),
        grid=(n // tm,),
        in_specs=in_specs,
        out_specs=out_specs,
        out_shape=out_shape,
        input_output_aliases=aliases,
        compiler_params=_cparams("arbitrary"),
        name="norm_qkv",
    )(*args)


def _pair_masks():
    lane = lax.broadcasted_iota(jnp.int32, (1, HEAD_PAIR), 1)
    first = lane < HEAD_PAIR // 2
    return first, jnp.logical_not(first)


def _dot_nt(a, b):
    return lax.dot_general(a, b, (((1,), (1,)), ((), ())), preferred_element_type=F32)


def _ctx_attn_kernel(q_ref, k_ref, v_ref, x_ref, mod_ref, wo_ref, o_ref, att_ref, *, npairs):
    first, second = _pair_masks()
    for p in range(npairs):
        qp, kp, vp = q_ref[p], k_ref[p], v_ref[p]
        outs = []
        for msk in (first, second):
            qh = jnp.where(msk, qp, jnp.zeros_like(qp))
            s = _dot_nt(qh, kp)
            m = jnp.max(s, axis=-1, keepdims=True)
            e = jnp.exp(s - m)
            l = jnp.sum(e, axis=-1, keepdims=True)
            outs.append(jnp.dot(e.astype(BF16), vp, preferred_element_type=F32) / l)
        att_ref[:, p * HEAD_PAIR:(p + 1) * HEAD_PAIR] = jnp.where(first, outs[0], outs[1]).astype(BF16)
    o = jnp.dot(att_ref[...], wo_ref[...], preferred_element_type=F32)
    o_ref[...] = x_ref[...] + mod_ref[2:3, :] * o


def _ctx_attention(qp, kp, vp, x, mod, wo, layer, *, seq):
    n, d = x.shape
    npairs = d // HEAD_PAIR
    pair_spec = pl.BlockSpec((npairs, seq, HEAD_PAIR), lambda b: (0, b, 0))
    return pl.pallas_call(
        functools.partial(_ctx_attn_kernel, npairs=npairs),
        grid=(n // seq,),
        in_specs=[
            pair_spec, pair_spec, pair_spec,
            pl.BlockSpec((seq, d), lambda b: (b, 0)),
            pl.BlockSpec((None, N_MOD, d), lambda b: (0, 0, 0)),
            _single((None, d, d), lambda b: (layer, 0, 0)),
        ],
        out_specs=pl.BlockSpec((seq, d), lambda b: (b, 0)),
        out_shape=jax.ShapeDtypeStruct((n, d), F32),
        scratch_shapes=[pltpu.VMEM((seq, d), BF16)],
        compiler_params=_cparams("arbitrary"),
        name="context_attention",
    )(qp, kp, vp, x, mod, wo)


def _na_attn_kernel(q_ref, k_ref, v_ref, ck_ref, cv_ref, strip_ref, x_ref, mod_ref, wo_ref, o_ref,
                    bias_ref, att_ref, *, npairs, rows, q_tile):
    first, second = _pair_masks()
    kr = min(WIN_R, rows)
    n_loc = rows * GRID_W

    bias_ref[...] = jnp.full(bias_ref.shape, -jnp.inf, F32)

    def pair_body(p, carry):
        for hh in range(2):
            for qr in range(rows):
                rs = min(max(qr - kr // 2, 0), rows - kr)
                off = rs - qr + (WIN_R - 1)
                bias_ref[hh, qr * GRID_W:(qr + 1) * GRID_W, rs * GRID_W:(rs + kr) * GRID_W] = (
                    strip_ref[2 * p + hh, :, off * GRID_W:(off + kr) * GRID_W])
        kp, vp, ckp, cvp = k_ref[p], v_ref[p], ck_ref[p], cv_ref[p]
        for qt in range(n_loc // q_tile):
            rsl = slice(qt * q_tile, (qt + 1) * q_tile)
            qp = q_ref[p, rsl, :]
            outs = []
            for hh, msk in enumerate((first, second)):
                qh = jnp.where(msk, qp, jnp.zeros_like(qp))
                s_loc = _dot_nt(qh, kp) + bias_ref[hh, rsl, :]
                s_ctx = _dot_nt(qh, ckp)
                m = jnp.maximum(jnp.max(s_loc, axis=-1, keepdims=True), jnp.max(s_ctx, axis=-1, keepdims=True))
                e_loc = jnp.exp(s_loc - m)
                e_ctx = jnp.exp(s_ctx - m)
                l = jnp.sum(e_loc, axis=-1, keepdims=True) + jnp.sum(e_ctx, axis=-1, keepdims=True)
                o = (jnp.dot(e_loc.astype(BF16), vp, preferred_element_type=F32)
                     + jnp.dot(e_ctx.astype(BF16), cvp, preferred_element_type=F32))
                outs.append(o / l)
            att_ref[p, rsl, :] = jnp.where(first, outs[0], outs[1]).astype(BF16)
        return carry

    lax.fori_loop(0, npairs, pair_body, 0)
    att = jnp.concatenate([att_ref[p] for p in range(npairs)], axis=-1)
    o = jnp.dot(att, wo_ref[...], preferred_element_type=F32)
    o_ref[...] = x_ref[...] + mod_ref[2:3, :] * o


def _na_attention(qp, kp, vp, ckp, cvp, strip, x, mod, wo, layer, *, n_batch):
    n, d = x.shape
    npairs = d // HEAD_PAIR
    n_loc = n // n_batch
    past = ckp.shape[3]
    rows = n_loc // GRID_W
    pair_spec = _single((npairs, n_loc, HEAD_PAIR), lambda b: (0, b, 0))
    ctx_spec = _single((None, None, npairs, past, HEAD_PAIR), lambda b: (layer, b, 0, 0, 0))
    return pl.pallas_call(
        functools.partial(_na_attn_kernel, npairs=npairs, rows=rows, q_tile=256),
        grid=(n_batch,),
        in_specs=[
            pair_spec, pair_spec, pair_spec, ctx_spec, ctx_spec,
            _single((None,) + strip.shape[1:], lambda b: (layer, 0, 0, 0)),
            _single((n_loc, d), lambda b: (b, 0)),
            pl.BlockSpec((None, N_MOD, d), lambda b: (1 + b, 0, 0)),
            _single((None, d, d), lambda b: (layer, 0, 0)),
        ],
        out_specs=_single((n_loc, d), lambda b: (b, 0)),
        out_shape=jax.ShapeDtypeStruct((n, d), F32),
        scratch_shapes=[pltpu.VMEM((2, n_loc, n_loc), F32), pltpu.VMEM((npairs, n_loc, HEAD_PAIR), BF16)],
        compiler_params=_cparams("arbitrary"),
        name="neighbourhood_attention",
    )(qp, kp, vp, ckp, cvp, strip, x, mod, wo)


def _rpb_strip(rpb):
    qc = np.arange(GRID_W)[:, None]
    kc = np.arange(GRID_W)[None, :]
    col_start = np.clip(qc - WIN_C // 2, 0, GRID_W - WIN_C)
    in_win = (kc >= col_start) & (kc < col_start + WIN_C)
    dc = np.clip(kc - qc, -(WIN_C - 1), WIN_C - 1) + (WIN_C - 1)
    sel = (dc[:, :, None] == np.arange(2 * WIN_C - 1)).astype(np.float32)
    t = jnp.einsum('lhdj,qkj->lhqdk', rpb.astype(F32), sel, precision=lax.Precision.HIGHEST)
    t = jnp.where(in_win[None, None, :, None, :], t, -jnp.inf)
    n_layers, h, ndr = rpb.shape[:3]
    return t.reshape(n_layers, h, GRID_W, ndr * GRID_W)


def _ffn_kernel(x_ref, mod_ref, g_ref, w1_ref, w3_ref, w2_ref, o_ref, t_ref):
    x = x_ref[...]
    h = _norm_mod(x, g_ref[...], mod_ref[3:4, :], mod_ref[4:5, :]).astype(BF16)
    for c in range(w1_ref.shape[1] // MXU_DIM):
        sl = slice(c * MXU_DIM, (c + 1) * MXU_DIM)
        a = jnp.dot(h, w1_ref[:, sl], preferred_element_type=F32)
        b = jnp.dot(h, w3_ref[:, sl], preferred_element_type=F32)
        t_ref[:, sl] = (a * jax.nn.sigmoid(a) * b).astype(BF16)
    o_ref[...] = x + mod_ref[5:6, :] * jnp.dot(t_ref[...], w2_ref[...], preferred_element_type=F32)


def _ffn(x, mod, group_of_tile, g, w1, w3, w2, layer, *, tm):
    n, d = x.shape
    d_ff = w1.shape[-1]
    return pl.pallas_call(
        _ffn_kernel,
        grid=(n // tm,),
        in_specs=[
            pl.BlockSpec((tm, d), lambda i: (i, 0)),
            pl.BlockSpec((None, N_MOD, d), lambda i: (group_of_tile(i), 0, 0)),
            pl.BlockSpec((1, d), lambda i: (0, 0)),
            _single((None, d, d_ff), lambda i: (layer, 0, 0)),
            _single((None, d, d_ff), lambda i: (layer, 0, 0)),
            _single((None, d_ff, d), lambda i: (layer, 0, 0)),
        ],
        out_specs=pl.BlockSpec((tm, d), lambda i: (i, 0)),
        out_shape=jax.ShapeDtypeStruct((n, d), F32),
        scratch_shapes=[pltpu.VMEM((tm, d_ff), BF16)],
        compiler_params=_cparams("arbitrary"),
        name="swiglu_ffn",
    )(x, mod, g, w1, w3, w2)


def _ssm_kernel(x_ref, mod_ref, g_ref, dskip_ref, wb_ref, wc_ref, a_ref, h0_ref, wglu_ref, *rest,
                seq, n_seg, n_chunks, with_state):
    if with_state:
        o_ref, sre_ref, sim_ref = rest[:3]
    else:
        o_ref = rest[0]
    u_ref, y_ref, bu_a, bu_b, xs_a, xs_b, f_ref, z_ref = rest[-8:]
    bu_refs, xs_refs = (bu_a, bu_b), (xs_a, xs_b)
    d = x_ref.shape[2]
    ns = bu_a.shape[1] // 2
    tb = 2 * SUBLANES
    ch_rows = SCAN_STEPS * SUBLANES
    n_tc = seq // SCAN_STEPS
    dk = pl.program_id(1)
    direction = dk // n_chunks
    k = dk % n_chunks
    fwd = direction == 0

    @pl.when(dk == 0)
    def _prepare():
        gain, shift, scale, dskip = g_ref[...], mod_ref[0], mod_ref[1], dskip_ref[...]

        def body(i, carry):
            t0 = pl.multiple_of(i * tb, tb)
            xt = jnp.swapaxes(x_ref[:, pl.ds(t0, tb), :], 0, 1)
            h = _norm_mod(xt, gain, shift, scale).reshape(tb * SUBLANES, d)
            hd = h * dskip
            hb = h.astype(BF16)
            row = pl.multiple_of(i * tb * SUBLANES, tb * SUBLANES)
            for kk in range(n_chunks):
                sl = slice(kk * SSM_CHUNK, (kk + 1) * SSM_CHUNK)
                u_ref[kk, pl.ds(row, tb * SUBLANES), :] = hb[:, sl]
                y_ref[kk, pl.ds(row, tb * SUBLANES), :] = hd[:, sl]
            return carry

        lax.fori_loop(0, seq // tb, body, 0)

    a_re = jnp.broadcast_to(a_ref[:, :ns], (SUBLANES, ns))
    a_im = jnp.broadcast_to(a_ref[:, ns:], (SUBLANES, ns))

    def scan_pass(state, store, reverse):
        def rows(i):
            return pl.multiple_of((n_tc - 1 - i if reverse else i) * ch_rows, ch_rows)

        def proj_in(i, b):
            bu_refs[b][...] = jnp.dot(u_ref[k, pl.ds(rows(i), ch_rows), :], wb_ref[...],
                                      preferred_element_type=F32)

        def proj_out(i, b):
            if store:
                r0 = rows(i)
                yc = jnp.dot(xs_refs[b][...], wc_ref[...], preferred_element_type=F32)
                y_ref[k, pl.ds(r0, ch_rows), :] = y_ref[k, pl.ds(r0, ch_rows), :] + yc

        def scan(b, carry):
            xr, xi = carry
            pairs = range(SCAN_STEPS // 2)
            for m in (reversed(pairs) if reverse else pairs):
                new = {}
                for j in ((2 * m + 1, 2 * m) if reverse else (2 * m, 2 * m + 1)):
                    rsl = slice(j * SUBLANES, (j + 1) * SUBLANES)
                    bur, bui = bu_refs[b][rsl, :ns], bu_refs[b][rsl, ns:]
                    xr, xi = a_re * xr - a_im * xi + bur, a_re * xi + a_im * xr + bui
                    new[j] = (xr, xi)
                if store:
                    psl = slice(2 * m * SUBLANES, (2 * m + 2) * SUBLANES)
                    lo, hi = new[2 * m], new[2 * m + 1]
                    xs_refs[b][psl, :ns] = jnp.concatenate([lo[0], hi[0]], axis=0).astype(BF16)
                    xs_refs[b][psl, ns:] = jnp.concatenate([lo[1], hi[1]], axis=0).astype(BF16)
            return xr, xi

        proj_in(0, 0)
        proj_in(1, 1)
        state = scan(0, state)

        def pair(ii, st):
            i = 2 * ii + 1
            proj_out(i - 1, 0)
            proj_in(i + 1, 0)
            st = scan(1, st)
            proj_out(i, 1)
            proj_in(i + 2, 1)
            return scan(0, st)

        state = lax.fori_loop(0, (n_tc - 2) // 2, pair, state)
        proj_out(n_tc - 2, 0)
        state = scan(1, state)
        proj_out(n_tc - 1, 1)
        return state

    def run_direction(reverse):
        if n_seg > 1:
            zero = jnp.zeros((SUBLANES, ns), F32)
            z_re, z_im = scan_pass((zero, zero), False, reverse)
            z_ref[:, :ns] = z_re
            z_ref[:, ns:] = z_im
            f_ref[...] = h0_ref[...]
            p_re, p_im = a_ref[:, :ns], a_ref[:, ns:]
            for _ in range(int(math.log2(seq))):
                p_re, p_im = p_re * p_re - p_im * p_im, 2.0 * p_re * p_im
            for b in range(SUBLANES // n_seg):
                for s in (range(n_seg - 2, -1, -1) if reverse else range(1, n_seg)):
                    dst = b * n_seg + s
                    src = dst + 1 if reverse else dst - 1
                    f_re, f_im = f_ref[src:src + 1, :ns], f_ref[src:src + 1, ns:]
                    f_ref[dst:dst + 1, :ns] = p_re * f_re - p_im * f_im + z_ref[src:src + 1, :ns]
                    f_ref[dst:dst + 1, ns:] = p_re * f_im + p_im * f_re + z_ref[src:src + 1, ns:]
            start = (f_ref[:, :ns], f_ref[:, ns:])
        else:
            start = (h0_ref[:, :ns], h0_ref[:, ns:])
        fin_re, fin_im = scan_pass(start, True, reverse)
        if with_state:
            sre_ref[...] = fin_re
            sim_ref[...] = fin_im

    pl.when(fwd)(functools.partial(run_direction, False))
    pl.when(jnp.logical_not(fwd))(functools.partial(run_direction, True))

    @pl.when(dk == 2 * n_chunks - 1)
    def _finish():
        gate = mod_ref[2]

        def body(i, carry):
            r0 = pl.multiple_of(i * ch_rows, ch_rows)
            y = jnp.concatenate([y_ref[kk, pl.ds(r0, ch_rows), :] for kk in range(n_chunks)], axis=-1)
            z = jax.nn.gelu(y).astype(BF16)
            gl = jnp.dot(z, wglu_ref[...], preferred_element_type=F32)
            out = (gl[:, :d] * jax.nn.sigmoid(gl[:, d:])).reshape(SCAN_STEPS, SUBLANES, d) * gate
            for j in range(SCAN_STEPS // tb):
                t0 = pl.multiple_of(i * SCAN_STEPS + j * tb, tb)
                upd = jnp.swapaxes(out[j * tb:(j + 1) * tb], 0, 1)
                o_ref[:, pl.ds(t0, tb), :] = x_ref[:, pl.ds(t0, tb), :] + upd
            return carry

        lax.fori_loop(0, n_tc, body, 0)


def _ssm_mixer(x, mod8, g, dskip, wb, wc, a, h0, wglu, layer, *, seq, n_seg, with_state):
    n_seq, _, d = x.shape
    n_groups = n_seq // SUBLANES
    n_chunks = d // SSM_CHUNK
    ns = wb.shape[-1] // 2
    grid = (n_groups, 2 * n_chunks)
    rows = SUBLANES * seq
    dk_map = lambda gi, dk: (layer, dk // n_chunks, dk % n_chunks, 0, 0)
    x_spec = _single((SUBLANES, seq, d), lambda gi, dk: (gi, 0, 0))
    out_specs = [x_spec]
    out_shape = [jax.ShapeDtypeStruct(x.shape, F32)]
    seq_map = lambda gi, dk: (dk // n_chunks, dk % n_chunks, gi, 0)
    if with_state:
        st_spec = pl.BlockSpec((None, None, SUBLANES, ns), seq_map)
        out_specs += [st_spec, st_spec]
        out_shape += [jax.ShapeDtypeStruct((2, n_chunks, n_groups * SUBLANES, ns), F32)] * 2
    return pl.pallas_call(
        functools.partial(_ssm_kernel, seq=seq, n_seg=n_seg, n_chunks=n_chunks, with_state=with_state),
        grid=grid,
        in_specs=[
            x_spec,
            pl.BlockSpec((N_MOD, SUBLANES, d), lambda gi, dk: (0, 0, 0)),
            pl.BlockSpec((1, d), lambda gi, dk: (0, 0)),
            pl.BlockSpec((1, d), lambda gi, dk: (0, 0)),
            pl.BlockSpec((None, None, None, SSM_CHUNK, 2 * ns), dk_map),
            pl.BlockSpec((None, None, None, 2 * ns, SSM_CHUNK), dk_map),
            pl.BlockSpec((None, None, None, 1, 2 * ns), dk_map),
            pl.BlockSpec((None, None, SUBLANES, 2 * ns), seq_map),
            _single((None, d, 2 * d), lambda gi, dk: (layer, 0, 0)),
        ],
        out_specs=out_specs,
        out_shape=out_shape,
        scratch_shapes=[
            pltpu.VMEM((n_chunks, rows, SSM_CHUNK), BF16),
            pltpu.VMEM((n_chunks, rows, SSM_CHUNK), F32),
            pltpu.VMEM((SCAN_STEPS * SUBLANES, 2 * ns), F32),
            pltpu.VMEM((SCAN_STEPS * SUBLANES, 2 * ns), F32),
            pltpu.VMEM((SCAN_STEPS * SUBLANES, 2 * ns), BF16),
            pltpu.VMEM((SCAN_STEPS * SUBLANES, 2 * ns), BF16),
            pltpu.VMEM((SUBLANES, 2 * ns), F32),
            pltpu.VMEM((SUBLANES, 2 * ns), F32),
        ],
        compiler_params=_cparams("arbitrary", "arbitrary"),
        name="s5_mixer",
    )(x, mod8, g, dskip, wb, wc, a, h0, wglu)


def _ssm_params(lam_re, lam_im, log_step, b_re, b_im, c_re, c_im):
    lam_re, lam_im = lam_re.astype(F32), lam_im.astype(F32)
    step = jnp.exp(log_step.astype(F32))[..., None]
    mag = jnp.exp(lam_re * step)
    a_re, a_im = mag * jnp.cos(lam_im * step), mag * jnp.sin(lam_im * step)
    den = lam_re * lam_re + lam_im * lam_im
    nr, ni = a_re - 1.0, a_im
    f_re = (nr * lam_re + ni * lam_im) / den
    f_im = (ni * lam_re - nr * lam_im) / den
    b_re, b_im = b_re.astype(F32), b_im.astype(F32)
    bb_re = f_re[..., None] * b_re - f_im[..., None] * b_im
    bb_im = f_re[..., None] * b_im + f_im[..., None] * b_re
    lead = bb_re.shape[:2]
    n_groups, p, c = bb_re.shape[2:]
    gpc = SSM_CHUNK // c
    kk = n_groups // gpc
    same_group = np.arange(gpc * c)[:, None] // c == np.arange(gpc * p)[None, :] // p

    def pack_in(bb):
        t = bb.reshape(*lead, kk, gpc, p, c).transpose(0, 1, 2, 5, 3, 4).reshape(*lead, kk, c, gpc * p)
        return jnp.where(same_group, jnp.tile(t, (1, 1, 1, gpc, 1)), 0.0).astype(BF16)

    def pack_out(cc):
        t = cc.astype(F32).reshape(*lead, kk, gpc, c, p).transpose(0, 1, 2, 3, 5, 4).reshape(*lead, kk, gpc * p, c)
        return jnp.where(same_group.T, jnp.tile(t, (1, 1, 1, 1, gpc)), 0.0).astype(BF16)

    wb = jnp.concatenate([pack_in(bb_re), pack_in(bb_im)], axis=-1)
    wc = jnp.concatenate([pack_out(c_re), -pack_out(c_im)], axis=-2)
    a = jnp.concatenate([a_re.reshape(*lead, kk, 1, gpc * p), a_im.reshape(*lead, kk, 1, gpc * p)], axis=-1)
    return wb, wc, a


def kernel(x_prompt, x_sample, cache_k, cache_v, state_ssm_re, state_ssm_im, c, c_ctx, norm_mix, norm_ffn, ada_w, ada_b, na_w_qkv, na_w_o, na_q_gain, na_k_gain, na_rpb, ssm_lambda_re, ssm_lambda_im, ssm_log_step, ssm_b_re, ssm_b_im, ssm_c_re, ssm_c_im, ssm_d, ssm_w_glu, ffn_w1, ffn_w3, ffn_w2):
    batch, seq, d = x_prompt.shape
    dec_batch, dec_seq, _ = x_sample.shape
    depth = ada_w.shape[0]
    heads, head_dim = cache_k.shape[3], cache_k.shape[4]
    past = cache_k.shape[2]
    npairs = d // HEAD_PAIR
    d_ff = ffn_w1.shape[-1]
    n_dir = state_ssm_re.shape[2]
    n_seg = dec_seq // seq
    assert seq % (2 * SCAN_STEPS) == 0 and dec_seq % seq == 0 and SUBLANES % n_seg == 0
    assert dec_batch * n_seg == SUBLANES and batch % SUBLANES == 0 and seq & (seq - 1) == 0
    assert d_ff % MXU_DIM == 0 and d % SSM_CHUNK == 0 and head_dim * 2 == HEAD_PAIR

    cond8 = jnp.zeros((SUBLANES, d), F32).at[0].set(c_ctx).at[1:1 + dec_batch].set(c)
    mod = _modulation(cond8, ada_w, ada_b).reshape(depth, SUBLANES, N_MOD, d)

    tm = math.gcd(512, dec_seq)
    prompt_group = lambda i: 0
    sample_group = lambda i: 1 + (i * tm) // dec_seq

    head_mean = jnp.kron(jnp.eye(MXU_DIM // head_dim, dtype=F32),
                         jnp.full((head_dim, head_dim), 1.0 / head_dim, F32)).astype(BF16)

    w_qkv, w_o, w_glu = na_w_qkv.astype(BF16), na_w_o.astype(BF16), ssm_w_glu.astype(BF16)
    w1, w3, w2 = ffn_w1.astype(BF16), ffn_w3.astype(BF16), ffn_w2.astype(BF16)
    n_na = cache_k.shape[1]
    ckp = cache_k.astype(BF16).reshape(dec_batch, n_na, past, npairs, HEAD_PAIR).transpose(1, 0, 3, 2, 4)
    cvp = cache_v.astype(BF16).reshape(dec_batch, n_na, past, npairs, HEAD_PAIR).transpose(1, 0, 3, 2, 4)
    strip = _rpb_strip(na_rpb)
    wb, wc, a = _ssm_params(ssm_lambda_re, ssm_lambda_im, ssm_log_step, ssm_b_re, ssm_b_im, ssm_c_re, ssm_c_im)

    xp = x_prompt.reshape(batch * seq, d)
    xs = x_sample.reshape(dec_batch * dec_seq, d)
    new_cache, new_sre, new_sim = None, [], []
    for i in range(depth):
        j = i // 2
        g_mix = norm_mix[i].reshape(1, d)
        if i % 2 == 0:
            q_gain = jnp.tile(na_q_gain[j], heads).reshape(1, d)
            k_gain = jnp.tile(na_k_gain[j], heads).reshape(1, d)
            qp, kp, vp, *new_cache = _qkv(xp, mod[i], prompt_group, g_mix, w_qkv, j, n_na, q_gain, k_gain,
                                          head_mean, new_cache, tm=tm, head_dim=head_dim, seq=seq, with_cache=True)
            xp = _ctx_attention(qp, kp, vp, xp, mod[i], w_o, j, seq=seq)
            qs, ks, vs = _qkv(xs, mod[i], sample_group, g_mix, w_qkv, j, n_na, q_gain, k_gain, head_mean, None,
                              tm=tm, head_dim=head_dim, seq=seq, with_cache=False)
            xs = _na_attention(qs, ks, vs, ckp, cvp, strip, xs, mod[i], w_o, j, n_batch=dec_batch)
        else:
            dskip = ssm_d[j].reshape(1, d)
            n_chunks = d // SSM_CHUNK
            ns = wb.shape[-1] // 2
            mod_p = jnp.broadcast_to(mod[i, 0][:, None, :], (N_MOD, SUBLANES, d))
            h0_p = jnp.zeros((n_dir, n_chunks, batch, 2 * ns), F32)
            xp, sre, sim = _ssm_mixer(xp.reshape(batch, seq, d), mod_p, g_mix, dskip, wb, wc, a, h0_p, w_glu, j,
                                      seq=seq, n_seg=1, with_state=True)
            xp = xp.reshape(batch * seq, d)
            new_sre.append(sre.transpose(2, 0, 1, 3).reshape(batch, n_dir, d // SSM_GROUP, SSM_STATE))
            new_sim.append(sim.transpose(2, 0, 1, 3).reshape(batch, n_dir, d // SSM_GROUP, SSM_STATE))
            mod_s = jnp.repeat(mod[i, 1:1 + dec_batch], n_seg, axis=0).transpose(1, 0, 2)
            s_re = state_ssm_re[:, j].astype(F32).reshape(dec_batch, n_dir, n_chunks, ns)
            s_im = state_ssm_im[:, j].astype(F32).reshape(dec_batch, n_dir, n_chunks, ns)
            s0 = jnp.concatenate([s_re, s_im], axis=-1)
            h0_s = jnp.zeros((dec_batch, n_seg, n_dir, n_chunks, 2 * ns), F32)
            h0_s = h0_s.at[:, 0, 0].set(s0[:, 0]).at[:, n_seg - 1, 1].set(s0[:, 1])
            h0_s = h0_s.reshape(dec_batch * n_seg, n_dir, n_chunks, 2 * ns).transpose(1, 2, 0, 3)
            (xs,) = _ssm_mixer(xs.reshape(dec_batch * n_seg, seq, d), mod_s, g_mix, dskip, wb, wc, a, h0_s, w_glu, j,
                               seq=seq, n_seg=n_seg, with_state=False)
            xs = xs.reshape(dec_batch * dec_seq, d)
        g_ffn = norm_ffn[i].reshape(1, d)
        xp = _ffn(xp, mod[i], prompt_group, g_ffn, w1, w3, w2, i, tm=tm)
        xs = _ffn(xs, mod[i], sample_group, g_ffn, w1, w3, w2, i, tm=tm)
    new_k, new_v = (t.reshape(batch, n_na, heads, head_dim, seq).transpose(0, 1, 4, 2, 3) for t in new_cache)
    return (xp.reshape(batch, seq, d), xs.reshape(dec_batch, dec_seq, d), new_k, new_v,
            jnp.stack(new_sre, axis=1), jnp.stack(new_sim, axis=1))
```

```python
import functools
import math

import jax
import jax.numpy as jnp
import numpy as np
from jax import lax
from jax.experimental import pallas as pl
from jax.experimental.pallas import tpu as pltpu

F32 = jnp.float32
BF16 = jnp.bfloat16

EPS = 1e-6
N_MOD = 6
GRID_W = 64
WIN_R = 8
WIN_C = 16
SSM_GROUP = 16
SSM_STATE = 64

SUBLANES = 8
LANES = 128
MXU_DIM = 256
VMEM_LIMIT_BYTES = 56 * 1024 * 1024

HEAD_PAIR = LANES
QUAD = 4
Q_CH = MXU_DIM // QUAD
CHUNKS_PER_STEP = 4
FINISH_QUADS = 8


def _cparams(*sem):
    return pltpu.CompilerParams(dimension_semantics=sem, vmem_limit_bytes=VMEM_LIMIT_BYTES)


def _single(block_shape, index_map):
    return pl.BlockSpec(block_shape, index_map, pipeline_mode=pl.Buffered(1))


def _norm_mod(x, g, shift, scale):
    ms = jnp.mean(x * x, axis=-1, keepdims=True)
    y = x * lax.rsqrt(ms + EPS) * g
    return y * (1.0 + scale) + shift


def _mod_kernel(cond_ref, w_ref, b_ref, o_ref):
    c = cond_ref[...]
    a = (c * jax.nn.sigmoid(c)).astype(BF16)
    o_ref[...] = jnp.dot(a, w_ref[...].astype(BF16), preferred_element_type=F32) + b_ref[...]


def _modulation(cond8, ada_w, ada_b):
    depth, d, n = ada_w.shape
    tn = n // 4
    return pl.pallas_call(
        _mod_kernel,
        grid=(depth, n // tn),
        in_specs=[
            pl.BlockSpec((SUBLANES, d), lambda i, j: (0, 0)),
            pl.BlockSpec((None, d, tn), lambda i, j: (i, 0, j)),
            pl.BlockSpec((None, 1, tn), lambda i, j: (i, 0, j)),
        ],
        out_specs=pl.BlockSpec((None, SUBLANES, tn), lambda i, j: (i, 0, j)),
        out_shape=jax.ShapeDtypeStruct((depth, SUBLANES, n), F32),
        compiler_params=_cparams("arbitrary", "arbitrary"),
        name="adaln_modulation",
    )(cond8, ada_w, ada_b.reshape(depth, 1, n))


def _qkv_kernel(x_ref, mod_ref, g_ref, w_ref, qg_ref, kg_ref, hm_ref, *rest, d, attn_scale, with_cache):
    out_refs = rest[-5:] if with_cache else rest
    qp_ref, kp_ref, vp_ref = out_refs[:3]
    h = _norm_mod(x_ref[...], g_ref[...], mod_ref[0:1, :], mod_ref[1:2, :]).astype(BF16)
    qkv = jnp.dot(h, w_ref[...], preferred_element_type=F32)
    q, k, v = qkv[:, :d], qkv[:, d:2 * d], qkv[:, 2 * d:]

    def head_norm(t, gain):
        parts = []
        for c in range(d // MXU_DIM):
            tc = t[:, c * MXU_DIM:(c + 1) * MXU_DIM]
            ms = jnp.dot((tc * tc).astype(BF16), hm_ref[...], preferred_element_type=F32)
            parts.append(tc * lax.rsqrt(ms + EPS))
        return jnp.concatenate(parts, axis=-1) * gain

    q = head_norm(q, qg_ref[...])
    k = head_norm(k, kg_ref[...])
    if with_cache:
        n_seq, _, seq = out_refs[3].shape
        for s in range(n_seq):
            out_refs[3][s] = k[s * seq:(s + 1) * seq, :].T
            out_refs[4][s] = v[s * seq:(s + 1) * seq, :].T
    qs = (q * attn_scale).astype(BF16)
    kb = k.astype(BF16)
    vb = v.astype(BF16)
    for p in range(d // HEAD_PAIR):
        sl = slice(p * HEAD_PAIR, (p + 1) * HEAD_PAIR)
        qp_ref[p] = qs[:, sl]
        kp_ref[p] = kb[:, sl]
        vp_ref[p] = vb[:, sl]


def _qkv(x, mod, group_of_tile, g, w_qkv, layer, n_layers, q_gain, k_gain, head_mean, cache, *,
         tm, head_dim, seq, with_cache):
    n, d = x.shape
    npairs = d // HEAD_PAIR
    pair_spec = pl.BlockSpec((npairs, tm, HEAD_PAIR), lambda i: (0, i, 0))
    out_specs = [pair_spec] * 3
    out_shape = [jax.ShapeDtypeStruct((npairs, n, HEAD_PAIR), BF16)] * 3
    in_specs = [
        pl.BlockSpec((tm, d), lambda i: (i, 0)),
        pl.BlockSpec((None, N_MOD, d), lambda i: (group_of_tile(i), 0, 0)),
        pl.BlockSpec((1, d), lambda i: (0, 0)),
        _single((None, d, 3 * d), lambda i: (layer, 0, 0)),
        pl.BlockSpec((1, d), lambda i: (0, 0)),
        pl.BlockSpec((1, d), lambda i: (0, 0)),
        pl.BlockSpec((MXU_DIM, MXU_DIM), lambda i: (0, 0)),
    ]
    args = [x, mod, g, w_qkv, q_gain, k_gain, head_mean]
    aliases = {}
    if with_cache:
        out_specs += [pl.BlockSpec((tm // seq, None, d, seq), lambda i: (i, layer, 0, 0))] * 2
        out_shape += [jax.ShapeDtypeStruct((n // seq, n_layers, d, seq), F32)] * 2
        if cache is not None:
            aliases = {len(args): 3, len(args) + 1: 4}
            in_specs += [pl.BlockSpec(memory_space=pl.ANY)] * 2
            args += list(cache)
    return pl.pallas_call(
        functools.partial(_qkv_kernel, d=d, attn_scale=head_dim ** -0.5, with_cache=with_cache),
        grid=(n // tm,),
        in_specs=in_specs,
        out_specs=out_specs,
        out_shape=out_shape,
        input_output_aliases=aliases,
        compiler_params=_cparams("arbitrary"),
        name="norm_qkv",
    )(*args)


def _pair_masks():
    lane = lax.broadcasted_iota(jnp.int32, (1, HEAD_PAIR), 1)
    first = lane < HEAD_PAIR // 2
    return first, jnp.logical_not(first)


def _dot_nt(a, b):
    return lax.dot_general(a, b, (((1,), (1,)), ((), ())), preferred_element_type=F32)


def _ctx_attn_kernel(q_ref, k_ref, v_ref, x_ref, mod_ref, wo_ref, o_ref, att_ref, *, npairs):
    first, second = _pair_masks()
    for p in range(npairs):
        qp, kp, vp = q_ref[p], k_ref[p], v_ref[p]
        outs = []
        for msk in (first, second):
            qh = jnp.where(msk, qp, jnp.zeros_like(qp))
            s = _dot_nt(qh, kp)
            m = jnp.max(s, axis=-1, keepdims=True)
            e = jnp.exp(s - m)
            l = jnp.sum(e, axis=-1, keepdims=True)
            outs.append(jnp.dot(e.astype(BF16), vp, preferred_element_type=F32) / l)
        att_ref[:, p * HEAD_PAIR:(p + 1) * HEAD_PAIR] = jnp.where(first, outs[0], outs[1]).astype(BF16)
    o = jnp.dot(att_ref[...], wo_ref[...], preferred_element_type=F32)
    o_ref[...] = x_ref[...] + mod_ref[2:3, :] * o


def _ctx_attention(qp, kp, vp, x, mod, wo, layer, *, seq):
    n, d = x.shape
    npairs = d // HEAD_PAIR
    pair_spec = pl.BlockSpec((npairs, seq, HEAD_PAIR), lambda b: (0, b, 0))
    return pl.pallas_call(
        functools.partial(_ctx_attn_kernel, npairs=npairs),
        grid=(n // seq,),
        in_specs=[
            pair_spec, pair_spec, pair_spec,
            pl.BlockSpec((seq, d), lambda b: (b, 0)),
            pl.BlockSpec((None, N_MOD, d), lambda b: (0, 0, 0)),
            _single((None, d, d), lambda b: (layer, 0, 0)),
        ],
        out_specs=pl.BlockSpec((seq, d), lambda b: (b, 0)),
        out_shape=jax.ShapeDtypeStruct((n, d), F32),
        scratch_shapes=[pltpu.VMEM((seq, d), BF16)],
        compiler_params=_cparams("arbitrary"),
        name="context_attention",
    )(qp, kp, vp, x, mod, wo)


def _na_attn_kernel(q_ref, k_ref, v_ref, ck_ref, cv_ref, strip_ref, x_ref, mod_ref, wo_ref, o_ref,
                    bias_ref, att_ref, *, npairs, rows, q_tile):
    first, second = _pair_masks()
    kr = min(WIN_R, rows)
    n_loc = rows * GRID_W

    bias_ref[...] = jnp.full(bias_ref.shape, -jnp.inf, F32)

    def pair_body(p, carry):
        for hh in range(2):
            for qr in range(rows):
                rs = min(max(qr - kr // 2, 0), rows - kr)
                off = rs - qr + (WIN_R - 1)
                bias_ref[hh, qr * GRID_W:(qr + 1) * GRID_W, rs * GRID_W:(rs + kr) * GRID_W] = (
                    strip_ref[2 * p + hh, :, off * GRID_W:(off + kr) * GRID_W])
        kp, vp, ckp, cvp = k_ref[p], v_ref[p], ck_ref[p], cv_ref[p]
        for qt in range(n_loc // q_tile):
            rsl = slice(qt * q_tile, (qt + 1) * q_tile)
            qp = q_ref[p, rsl, :]
            outs = []
            for hh, msk in enumerate((first, second)):
                qh = jnp.where(msk, qp, jnp.zeros_like(qp))
                s_loc = _dot_nt(qh, kp) + bias_ref[hh, rsl, :]
                s_ctx = _dot_nt(qh, ckp)
                m = jnp.maximum(jnp.max(s_loc, axis=-1, keepdims=True), jnp.max(s_ctx, axis=-1, keepdims=True))
                e_loc = jnp.exp(s_loc - m)
                e_ctx = jnp.exp(s_ctx - m)
                l = jnp.sum(e_loc, axis=-1, keepdims=True) + jnp.sum(e_ctx, axis=-1, keepdims=True)
                o = (jnp.dot(e_loc.astype(BF16), vp, preferred_element_type=F32)
                     + jnp.dot(e_ctx.astype(BF16), cvp, preferred_element_type=F32))
                outs.append(o / l)
            att_ref[p, rsl, :] = jnp.where(first, outs[0], outs[1]).astype(BF16)
        return carry

    lax.fori_loop(0, npairs, pair_body, 0)
    att = jnp.concatenate([att_ref[p] for p in range(npairs)], axis=-1)
    o = jnp.dot(att, wo_ref[...], preferred_element_type=F32)
    o_ref[...] = x_ref[...] + mod_ref[2:3, :] * o


def _na_attention(qp, kp, vp, ckp, cvp, strip, x, mod, wo, layer, *, n_batch):
    n, d = x.shape
    npairs = d // HEAD_PAIR
    n_loc = n // n_batch
    past = ckp.shape[3]
    rows = n_loc // GRID_W
    pair_spec = _single((npairs, n_loc, HEAD_PAIR), lambda b: (0, b, 0))
    ctx_spec = _single((None, None, npairs, past, HEAD_PAIR), lambda b: (layer, b, 0, 0, 0))
    return pl.pallas_call(
        functools.partial(_na_attn_kernel, npairs=npairs, rows=rows, q_tile=256),
        grid=(n_batch,),
        in_specs=[
            pair_spec, pair_spec, pair_spec, ctx_spec, ctx_spec,
            _single((None,) + strip.shape[1:], lambda b: (layer, 0, 0, 0)),
            _single((n_loc, d), lambda b: (b, 0)),
            pl.BlockSpec((None, N_MOD, d), lambda b: (1 + b, 0, 0)),
            _single((None, d, d), lambda b: (layer, 0, 0)),
        ],
        out_specs=_single((n_loc, d), lambda b: (b, 0)),
        out_shape=jax.ShapeDtypeStruct((n, d), F32),
        scratch_shapes=[pltpu.VMEM((2, n_loc, n_loc), F32), pltpu.VMEM((npairs, n_loc, HEAD_PAIR), BF16)],
        compiler_params=_cparams("arbitrary"),
        name="neighbourhood_attention",
    )(qp, kp, vp, ckp, cvp, strip, x, mod, wo)


def _rpb_strip(rpb):
    qc = np.arange(GRID_W)[:, None]
    kc = np.arange(GRID_W)[None, :]
    col_start = np.clip(qc - WIN_C // 2, 0, GRID_W - WIN_C)
    in_win = (kc >= col_start) & (kc < col_start + WIN_C)
    dc = np.clip(kc - qc, -(WIN_C - 1), WIN_C - 1) + (WIN_C - 1)
    sel = (dc[:, :, None] == np.arange(2 * WIN_C - 1)).astype(np.float32)
    t = jnp.einsum('lhdj,qkj->lhqdk', rpb.astype(F32), sel, precision=lax.Precision.HIGHEST)
    t = jnp.where(in_win[None, None, :, None, :], t, -jnp.inf)
    n_layers, h, ndr = rpb.shape[:3]
    return t.reshape(n_layers, h, GRID_W, ndr * GRID_W)


def _ffn_kernel(x_ref, mod_ref, g_ref, w1_ref, w3_ref, w2_ref, o_ref, t_ref):
    x = x_ref[...]
    h = _norm_mod(x, g_ref[...], mod_ref[3:4, :], mod_ref[4:5, :]).astype(BF16)
    for c in range(w1_ref.shape[1] // MXU_DIM):
        sl = slice(c * MXU_DIM, (c + 1) * MXU_DIM)
        a = jnp.dot(h, w1_ref[:, sl], preferred_element_type=F32)
        b = jnp.dot(h, w3_ref[:, sl], preferred_element_type=F32)
        t_ref[:, sl] = (a * jax.nn.sigmoid(a) * b).astype(BF16)
    o_ref[...] = x + mod_ref[5:6, :] * jnp.dot(t_ref[...], w2_ref[...], preferred_element_type=F32)


def _ffn(x, mod, group_of_tile, g, w1, w3, w2, layer, *, tm):
    n, d = x.shape
    d_ff = w1.shape[-1]
    return pl.pallas_call(
        _ffn_kernel,
        grid=(n // tm,),
        in_specs=[
            pl.BlockSpec((tm, d), lambda i: (i, 0)),
            pl.BlockSpec((None, N_MOD, d), lambda i: (group_of_tile(i), 0, 0)),
            pl.BlockSpec((1, d), lambda i: (0, 0)),
            _single((None, d, d_ff), lambda i: (layer, 0, 0)),
            _single((None, d, d_ff), lambda i: (layer, 0, 0)),
            _single((None, d_ff, d), lambda i: (layer, 0, 0)),
        ],
        out_specs=pl.BlockSpec((tm, d), lambda i: (i, 0)),
        out_shape=jax.ShapeDtypeStruct((n, d), F32),
        scratch_shapes=[pltpu.VMEM((tm, d_ff), BF16)],
        compiler_params=_cparams("arbitrary"),
        name="swiglu_ffn",
    )(x, mod, g, w1, w3, w2)


def _lane_half_swap(x):
    n = x.shape[-1]
    lane = lax.broadcasted_iota(jnp.int32, (1, n), 1)
    return jnp.where(lane % LANES < LANES // 2, pltpu.roll(x, n - LANES // 2, axis=1),
                     pltpu.roll(x, LANES // 2, axis=1))


def _quad_pairs(t0, t1, t2, t3):
    lo = lax.broadcasted_iota(jnp.int32, (1, t0.shape[-1]), 1) % LANES < LANES // 2
    s0, s1, s2, s3 = (_lane_half_swap(t) for t in (t0, t1, t2, t3))
    even = (jnp.where(lo, t0, s1), jnp.where(lo, t2, s3))
    odd = (jnp.where(lo, s0, t1), jnp.where(lo, s2, t3))
    return even, odd


def _ssm_kernel(x_ref, mod_ref, g_ref, dskip_ref, win_ref, wcar_ref, kin_ref, a4_ref, h0_ref, wglu_ref, *rest,
                seq, n_seg, with_state):
    if with_state:
        o_ref, sre_ref, sim_ref = rest[:3]
    else:
        o_ref = rest[0]
    u_ref, y_ref, z_ref, sp_ref, f_ref, e_ref = rest[-6:]
    d = x_ref.shape[2]
    cps = win_ref.shape[0]
    nsq = a4_ref.shape[-1] // 2
    nst = cps * nsq
    nq = seq // QUAD
    steps_per_dir = (d // Q_CH) // cps
    tb = 4 * QUAD
    qb = tb // QUAD * SUBLANES
    dk = pl.program_id(1)
    kb = (dk % steps_per_dir) * cps
    fwd = dk // steps_per_dir == 0

    @pl.when(dk == 0)
    def _prepare():
        gain, shift, scale, dskip = g_ref[...], mod_ref[0], mod_ref[1], dskip_ref[...]

        def body(i, carry):
            t0 = pl.multiple_of(i * tb, tb)
            xt = jnp.swapaxes(x_ref[:, pl.ds(t0, tb), :], 0, 1)
            h = _norm_mod(xt, gain, shift, scale).reshape(tb // QUAD, QUAD, SUBLANES, d)
            rows = pl.ds(pl.multiple_of(i * qb, qb), qb)
            for val, dst in ((h, u_ref), (h * dskip, y_ref)):
                halves = _quad_pairs(*(val[:, s].reshape(qb, d) for s in range(QUAD)))
                for m in range(d // LANES):
                    for parity, (v01, v23) in enumerate(halves):
                        sl = slice(m * LANES, (m + 1) * LANES)
                        dst[2 * m + parity, rows, :LANES] = v01[:, sl].astype(dst.dtype)
                        dst[2 * m + parity, rows, LANES:] = v23[:, sl].astype(dst.dtype)
            return carry

        lax.fori_loop(0, seq // tb, body, 0)

    a_re = jnp.broadcast_to(jnp.concatenate([a4_ref[c][:, :nsq] for c in range(cps)], axis=-1), (SUBLANES, nst))
    a_im = jnp.broadcast_to(jnp.concatenate([a4_ref[c][:, nsq:] for c in range(cps)], axis=-1), (SUBLANES, nst))

    def chunk_cols(ref, c):
        return jnp.concatenate([ref[:, c * nsq:(c + 1) * nsq], ref[:, nst + c * nsq:nst + (c + 1) * nsq]], axis=-1)

    def scan(state, store, reverse):
        s_re, s_im = state
        pairs = range(nq // 2)
        for m in (reversed(pairs) if reverse else pairs):
            entering = {}
            for j in ((2 * m + 1, 2 * m) if reverse else (2 * m, 2 * m + 1)):
                rsl = slice(j * SUBLANES, (j + 1) * SUBLANES)
                entering[j] = (s_re, s_im)
                s_re, s_im = (a_re * s_re - a_im * s_im + z_ref[rsl, :nst],
                              a_re * s_im + a_im * s_re + z_ref[rsl, nst:])
            if store:
                psl = slice(2 * m * SUBLANES, (2 * m + 2) * SUBLANES)
                lo, hi = entering[2 * m], entering[2 * m + 1]
                sp_ref[psl, :nst] = jnp.concatenate([lo[0], hi[0]], axis=0).astype(BF16)
                sp_ref[psl, nst:] = jnp.concatenate([lo[1], hi[1]], axis=0).astype(BF16)
        return s_re, s_im

    def run_direction(reverse):
        for c in range(cps):
            z = jnp.dot(u_ref[kb + c], win_ref[c], preferred_element_type=F32)
            z_ref[:, c * nsq:(c + 1) * nsq] = z[:, :nsq]
            z_ref[:, nst + c * nsq:nst + (c + 1) * nsq] = z[:, nsq:]
        start = (jnp.concatenate([h0_ref[c][:, :nsq] for c in range(cps)], axis=-1),
                 jnp.concatenate([h0_ref[c][:, nsq:] for c in range(cps)], axis=-1))
        if n_seg > 1:
            zero = jnp.zeros((SUBLANES, nst), F32)
            e_re, e_im = scan((zero, zero), False, reverse)
            e_ref[:, :nst] = e_re
            e_ref[:, nst:] = e_im
            f_ref[:, :nst] = start[0]
            f_ref[:, nst:] = start[1]
            p_re, p_im = a_re[:1], a_im[:1]
            for _ in range(int(math.log2(nq))):
                p_re, p_im = p_re * p_re - p_im * p_im, 2.0 * p_re * p_im
            for b in range(SUBLANES // n_seg):
                for s in (range(n_seg - 2, -1, -1) if reverse else range(1, n_seg)):
                    dst = b * n_seg + s
                    src = dst + 1 if reverse else dst - 1
                    f_re, f_im = f_ref[src:src + 1, :nst], f_ref[src:src + 1, nst:]
                    f_ref[dst:dst + 1, :nst] = p_re * f_re - p_im * f_im + e_ref[src:src + 1, :nst]
                    f_ref[dst:dst + 1, nst:] = p_re * f_im + p_im * f_re + e_ref[src:src + 1, nst:]
            start = (f_ref[:, :nst], f_ref[:, nst:])
        fin_re, fin_im = scan(start, True, reverse)
        if with_state:
            for c in range(cps):
                sre_ref[c] = fin_re[:, c * nsq:(c + 1) * nsq]
                sim_ref[c] = fin_im[:, c * nsq:(c + 1) * nsq]
        for c in range(cps):
            y_ref[kb + c] = (y_ref[kb + c]
                             + jnp.dot(chunk_cols(sp_ref, c), wcar_ref[c], preferred_element_type=F32)
                             + jnp.dot(u_ref[kb + c], kin_ref[c], preferred_element_type=F32))

    pl.when(fwd)(functools.partial(run_direction, False))
    pl.when(jnp.logical_not(fwd))(functools.partial(run_direction, True))

    @pl.when(dk == 2 * steps_per_dir - 1)
    def _finish():
        gate = mod_ref[2]
        fq = FINISH_QUADS * SUBLANES
        ft = FINISH_QUADS * QUAD

        def body(i, carry):
            rows = pl.ds(pl.multiple_of(i * fq, fq), fq)
            gathered = [jnp.concatenate([y_ref[2 * m + parity, rows, half] for m in range(d // LANES)], axis=-1)
                        for half in (slice(0, LANES), slice(LANES, 2 * LANES)) for parity in (0, 1)]
            (y0, y2), (y1, y3) = _quad_pairs(*gathered)
            y = jnp.stack([t.reshape(FINISH_QUADS, SUBLANES, d) for t in (y0, y1, y2, y3)], axis=1)
            z = jax.nn.gelu(y.reshape(ft * SUBLANES, d)).astype(BF16)
            gl = jnp.dot(z, wglu_ref[...], preferred_element_type=F32)
            out = (gl[:, :d] * jax.nn.sigmoid(gl[:, d:])).reshape(ft, SUBLANES, d) * gate
            for j in range(ft // tb):
                t0 = pl.multiple_of(i * ft + j * tb, tb)
                upd = jnp.swapaxes(out[j * tb:(j + 1) * tb], 0, 1)
                o_ref[:, pl.ds(t0, tb), :] = x_ref[:, pl.ds(t0, tb), :] + upd
            return carry

        lax.fori_loop(0, nq // FINISH_QUADS, body, 0)


def _ssm_mixer(x, mod8, g, dskip, w_in, w_car, k_in, a4, h0, wglu, layer, *, seq, n_seg, with_state):
    n_seq, _, d = x.shape
    n_groups = n_seq // SUBLANES
    n_chunks = d // Q_CH
    nsq = a4.shape[-1] // 2
    cps = CHUNKS_PER_STEP
    steps_per_dir = n_chunks // cps
    rows_q = seq // QUAD * SUBLANES
    dk_map = lambda gi, dk: (layer, dk // steps_per_dir, dk % steps_per_dir, 0, 0)
    seq_map = lambda gi, dk: (dk // steps_per_dir, dk % steps_per_dir, gi, 0)
    x_spec = _single((SUBLANES, seq, d), lambda gi, dk: (gi, 0, 0))
    out_specs = [x_spec]
    out_shape = [jax.ShapeDtypeStruct(x.shape, F32)]
    if with_state:
        st_spec = pl.BlockSpec((None, cps, SUBLANES, nsq), seq_map)
        out_specs += [st_spec, st_spec]
        out_shape += [jax.ShapeDtypeStruct((2, n_chunks, n_seq, nsq), F32)] * 2
    return pl.pallas_call(
        functools.partial(_ssm_kernel, seq=seq, n_seg=n_seg, with_state=with_state),
        grid=(n_groups, 2 * steps_per_dir),
        in_specs=[
            x_spec,
            pl.BlockSpec((N_MOD, SUBLANES, d), lambda gi, dk: (0, 0, 0)),
            pl.BlockSpec((1, d), lambda gi, dk: (0, 0)),
            pl.BlockSpec((1, d), lambda gi, dk: (0, 0)),
            pl.BlockSpec((None, None, cps, MXU_DIM, 2 * nsq), dk_map),
            pl.BlockSpec((None, None, cps, 2 * nsq, MXU_DIM), dk_map),
            pl.BlockSpec((None, None, cps, MXU_DIM, MXU_DIM), dk_map),
            pl.BlockSpec((None, None, cps, 1, 2 * nsq), dk_map),
            pl.BlockSpec((None, cps, SUBLANES, 2 * nsq), seq_map),
            _single((None, d, 2 * d), lambda gi, dk: (layer, 0, 0)),
        ],
        out_specs=out_specs,
        out_shape=out_shape,
        scratch_shapes=[
            pltpu.VMEM((n_chunks, rows_q, MXU_DIM), BF16),
            pltpu.VMEM((n_chunks, rows_q, MXU_DIM), F32),
            pltpu.VMEM((rows_q, 2 * cps * nsq), F32),
            pltpu.VMEM((rows_q, 2 * cps * nsq), BF16),
            pltpu.VMEM((SUBLANES, 2 * cps * nsq), F32),
            pltpu.VMEM((SUBLANES, 2 * cps * nsq), F32),
        ],
        compiler_params=_cparams("arbitrary", "arbitrary"),
        name="s5_mixer",
    )(x, mod8, g, dskip, w_in, w_car, k_in, a4, h0, wglu)


def _ssm_params(lam_re, lam_im, log_step, b_re, b_im, c_re, c_im):
    lam_re, lam_im = lam_re.astype(F32), lam_im.astype(F32)
    step = jnp.exp(log_step.astype(F32))[..., None]
    mag = jnp.exp(lam_re * step)
    a_re, a_im = mag * jnp.cos(lam_im * step), mag * jnp.sin(lam_im * step)
    den = lam_re * lam_re + lam_im * lam_im
    nr, ni = a_re - 1.0, a_im
    f_re = (nr * lam_re + ni * lam_im) / den
    f_im = (ni * lam_re - nr * lam_im) / den
    b_re, b_im = b_re.astype(F32), b_im.astype(F32)
    bb = (f_re[..., None] * b_re - f_im[..., None] * b_im,
          f_re[..., None] * b_im + f_im[..., None] * b_re)
    cc = (c_re.astype(F32), c_im.astype(F32))

    def cmul(x, y):
        return x[0] * y[0] - x[1] * y[1], x[0] * y[1] + x[1] * y[0]

    pw = [(jnp.ones_like(a_re), jnp.zeros_like(a_re))]
    for _ in range(QUAD):
        pw.append(cmul(pw[-1], (a_re, a_im)))

    def by_dir(n_fwd, n_bwd):
        return tuple(jnp.stack([pw[n_fwd][i][:, 0], pw[n_bwd][i][:, 1]], axis=1) for i in range(2))

    n_layers, _, n_groups, p, c = bb[0].shape
    gq = Q_CH // c
    kk = n_groups // gq
    nsq = gq * p
    lead = (n_layers, 2, kk)
    g_of_in_row = (np.arange(MXU_DIM) // c) % gq
    g_of_state = np.arange(nsq) // p

    def pack_in(ws):
        t = jnp.stack(ws, axis=2).reshape(n_layers, 2, QUAD, kk, gq, p, c).transpose(0, 1, 3, 2, 4, 6, 5)
        t = jnp.tile(t.reshape(*lead, MXU_DIM, p), (1, 1, 1, 1, gq))
        return jnp.where(g_of_in_row[:, None] == g_of_state[None, :], t, 0.0)

    def pack_out(ws):
        t = jnp.stack(ws, axis=2).reshape(n_layers, 2, QUAD, kk, gq, c, p).transpose(0, 1, 3, 4, 6, 2, 5)
        t = jnp.broadcast_to(t.reshape(*lead, nsq, QUAD, 1, c), (*lead, nsq, QUAD, gq, c))
        return jnp.where(g_of_state[:, None] == g_of_in_row[None, :], t.reshape(*lead, nsq, MXU_DIM), 0.0)

    w_in = [cmul(tuple(t[..., None] for t in by_dir(QUAD - 1 - s, s)), bb) for s in range(QUAD)]
    w_in = jnp.concatenate([pack_in([w[0] for w in w_in]), pack_in([w[1] for w in w_in])], axis=-1)
    w_car = [cmul(cc, tuple(t[..., None, :] for t in by_dir(s + 1, QUAD - s))) for s in range(QUAD)]
    w_car = jnp.concatenate([pack_out([w[0] for w in w_car]), -pack_out([w[1] for w in w_car])], axis=-2)

    hi = lax.Precision.HIGHEST
    taps = []
    for n in range(QUAD):
        cb = cmul(cc, tuple(t[..., None, :] for t in pw[n]))
        taps.append(jnp.einsum('ldgkp,ldgpc->ldgck', cb[0], bb[0], precision=hi)
                    - jnp.einsum('ldgkp,ldgpc->ldgck', cb[1], bb[1], precision=hi))
    zero = jnp.zeros_like(taps[0][:, 0])

    def tap(d, s, t):
        n = t - s if d == 0 else s - t
        return taps[n][:, d] if n >= 0 else zero

    t = jnp.stack([jnp.stack([jnp.stack([tap(d, s, t) for t in range(QUAD)], axis=1) for s in range(QUAD)], axis=1)
                   for d in range(2)], axis=1)
    t = t.reshape(n_layers, 2, QUAD, QUAD, kk, gq, c, c).transpose(0, 1, 4, 2, 5, 6, 3, 7)
    t = jnp.broadcast_to(t.reshape(*lead, MXU_DIM, QUAD, 1, c), (*lead, MXU_DIM, QUAD, gq, c))
    k_in = jnp.where(g_of_in_row[:, None] == g_of_in_row[None, :], t.reshape(*lead, MXU_DIM, MXU_DIM), 0.0)

    a4 = jnp.concatenate([pw[QUAD][0].reshape(*lead, 1, nsq), pw[QUAD][1].reshape(*lead, 1, nsq)], axis=-1)
    return w_in.astype(BF16), w_car.astype(BF16), k_in.astype(BF16), a4


def kernel(x_prompt, x_sample, cache_k, cache_v, state_ssm_re, state_ssm_im, c, c_ctx, norm_mix, norm_ffn, ada_w, ada_b, na_w_qkv, na_w_o, na_q_gain, na_k_gain, na_rpb, ssm_lambda_re, ssm_lambda_im, ssm_log_step, ssm_b_re, ssm_b_im, ssm_c_re, ssm_c_im, ssm_d, ssm_w_glu, ffn_w1, ffn_w3, ffn_w2):
    batch, seq, d = x_prompt.shape
    dec_batch, dec_seq, _ = x_sample.shape
    depth = ada_w.shape[0]
    heads, head_dim = cache_k.shape[3], cache_k.shape[4]
    past = cache_k.shape[2]
    npairs = d // HEAD_PAIR
    d_ff = ffn_w1.shape[-1]
    n_dir = state_ssm_re.shape[2]
    n_seg = dec_seq // seq
    assert seq % (FINISH_QUADS * QUAD) == 0 and dec_seq % seq == 0 and SUBLANES % n_seg == 0
    assert dec_batch * n_seg == SUBLANES and batch % SUBLANES == 0 and seq & (seq - 1) == 0
    assert d_ff % MXU_DIM == 0 and d % (Q_CH * CHUNKS_PER_STEP) == 0 and head_dim * 2 == HEAD_PAIR
    assert Q_CH * 2 == LANES and Q_CH % SSM_GROUP == 0

    cond8 = jnp.zeros((SUBLANES, d), F32).at[0].set(c_ctx).at[1:1 + dec_batch].set(c)
    mod = _modulation(cond8, ada_w, ada_b).reshape(depth, SUBLANES, N_MOD, d)

    tm = math.gcd(512, dec_seq)
    prompt_group = lambda i: 0
    sample_group = lambda i: 1 + (i * tm) // dec_seq

    head_mean = jnp.kron(jnp.eye(MXU_DIM // head_dim, dtype=F32),
                         jnp.full((head_dim, head_dim), 1.0 / head_dim, F32)).astype(BF16)

    w_qkv, w_o, w_glu = na_w_qkv.astype(BF16), na_w_o.astype(BF16), ssm_w_glu.astype(BF16)
    w1, w3, w2 = ffn_w1.astype(BF16), ffn_w3.astype(BF16), ffn_w2.astype(BF16)
    n_na = cache_k.shape[1]
    ckp = cache_k.astype(BF16).reshape(dec_batch, n_na, past, npairs, HEAD_PAIR).transpose(1, 0, 3, 2, 4)
    cvp = cache_v.astype(BF16).reshape(dec_batch, n_na, past, npairs, HEAD_PAIR).transpose(1, 0, 3, 2, 4)
    strip = _rpb_strip(na_rpb)
    s5_w = _ssm_params(ssm_lambda_re, ssm_lambda_im, ssm_log_step, ssm_b_re, ssm_b_im, ssm_c_re, ssm_c_im)
    a4 = s5_w[-1]

    xp = x_prompt.reshape(batch * seq, d)
    xs = x_sample.reshape(dec_batch * dec_seq, d)
    new_cache, new_sre, new_sim = None, [], []
    for i in range(depth):
        j = i // 2
        g_mix = norm_mix[i].reshape(1, d)
        if i % 2 == 0:
            q_gain = jnp.tile(na_q_gain[j], heads).reshape(1, d)
            k_gain = jnp.tile(na_k_gain[j], heads).reshape(1, d)
            qp, kp, vp, *new_cache = _qkv(xp, mod[i], prompt_group, g_mix, w_qkv, j, n_na, q_gain, k_gain,
                                          head_mean, new_cache, tm=tm, head_dim=head_dim, seq=seq, with_cache=True)
            xp = _ctx_attention(qp, kp, vp, xp, mod[i], w_o, j, seq=seq)
            qs, ks, vs = _qkv(xs, mod[i], sample_group, g_mix, w_qkv, j, n_na, q_gain, k_gain, head_mean, None,
                              tm=tm, head_dim=head_dim, seq=seq, with_cache=False)
            xs = _na_attention(qs, ks, vs, ckp, cvp, strip, xs, mod[i], w_o, j, n_batch=dec_batch)
        else:
            dskip = ssm_d[j].reshape(1, d)
            n_chunks = d // Q_CH
            ns = a4.shape[-1] // 2
            mod_p = jnp.broadcast_to(mod[i, 0][:, None, :], (N_MOD, SUBLANES, d))
            h0_p = jnp.zeros((n_dir, n_chunks, batch, 2 * ns), F32)
            xp, sre, sim = _ssm_mixer(xp.reshape(batch, seq, d), mod_p, g_mix, dskip, *s5_w, h0_p, w_glu, j,
                                      seq=seq, n_seg=1, with_state=True)
            xp = xp.reshape(batch * seq, d)
            new_sre.append(sre.transpose(2, 0, 1, 3).reshape(batch, n_dir, d // SSM_GROUP, SSM_STATE))
            new_sim.append(sim.transpose(2, 0, 1, 3).reshape(batch, n_dir, d // SSM_GROUP, SSM_STATE))
            mod_s = jnp.repeat(mod[i, 1:1 + dec_batch], n_seg, axis=0).transpose(1, 0, 2)
            s_re = state_ssm_re[:, j].astype(F32).reshape(dec_batch, n_dir, n_chunks, ns)
            s_im = state_ssm_im[:, j].astype(F32).reshape(dec_batch, n_dir, n_chunks, ns)
            s0 = jnp.concatenate([s_re, s_im], axis=-1)
            h0_s = jnp.zeros((dec_batch, n_seg, n_dir, n_chunks, 2 * ns), F32)
            h0_s = h0_s.at[:, 0, 0].set(s0[:, 0]).at[:, n_seg - 1, 1].set(s0[:, 1])
            h0_s = h0_s.reshape(dec_batch * n_seg, n_dir, n_chunks, 2 * ns).transpose(1, 2, 0, 3)
            (xs,) = _ssm_mixer(xs.reshape(dec_batch * n_seg, seq, d), mod_s, g_mix, dskip, *s5_w, h0_s, w_glu, j,
                               seq=seq, n_seg=n_seg, with_state=False)
            xs = xs.reshape(dec_batch * dec_seq, d)
        g_ffn = norm_ffn[i].reshape(1, d)
        xp = _ffn(xp, mod[i], prompt_group, g_ffn, w1, w3, w2, i, tm=tm)
        xs = _ffn(xs, mod[i], sample_group, g_ffn, w1, w3, w2, i, tm=tm)
    new_k, new_v = (t.reshape(batch, n_na, heads, head_dim, seq).transpose(0, 1, 4, 2, 3) for t in new_cache)
    return (xp.reshape(batch, seq, d), xs.reshape(dec_batch, dec_seq, d), new_k, new_v,
            jnp.stack(new_sre, axis=1), jnp.stack(new_sim, axis=1))
```

```python
import functools
import math

import jax
import jax.numpy as jnp
import numpy as np
from jax import lax
from jax.experimental import pallas as pl
from jax.experimental.pallas import tpu as pltpu

F32 = jnp.float32
BF16 = jnp.bfloat16

EPS = 1e-6
N_MOD = 6
GRID_W = 64
WIN_R = 8
WIN_C = 16
SSM_GROUP = 16
SSM_STATE = 64

SUBLANES = 8
LANES = 128
MXU_DIM = 256
VMEM_LIMIT_BYTES = 56 * 1024 * 1024

HEAD_PAIR = LANES
QUAD = 4
Q_CH = MXU_DIM // QUAD
CHUNKS_PER_STEP = 4
FINISH_QUADS = 8


def _cparams(*sem):
    return pltpu.CompilerParams(dimension_semantics=sem, vmem_limit_bytes=VMEM_LIMIT_BYTES)


def _single(block_shape, index_map):
    return pl.BlockSpec(block_shape, index_map, pipeline_mode=pl.Buffered(1))


def _norm_mod(x, g, shift, scale):
    ms = jnp.mean(x * x, axis=-1, keepdims=True)
    y = x * lax.rsqrt(ms + EPS) * g
    return y * (1.0 + scale) + shift


def _mod_kernel(cond_ref, w_ref, b_ref, o_ref):
    c = cond_ref[...]
    a = (c * jax.nn.sigmoid(c)).astype(BF16)
    o_ref[...] = jnp.dot(a, w_ref[...].astype(BF16), preferred_element_type=F32) + b_ref[...]


def _modulation(cond8, ada_w, ada_b):
    depth, d, n = ada_w.shape
    tn = n // 4
    return pl.pallas_call(
        _mod_kernel,
        grid=(depth, n // tn),
        in_specs=[
            pl.BlockSpec((SUBLANES, d), lambda i, j: (0, 0)),
            pl.BlockSpec((None, d, tn), lambda i, j: (i, 0, j)),
            pl.BlockSpec((None, 1, tn), lambda i, j: (i, 0, j)),
        ],
        out_specs=pl.BlockSpec((None, SUBLANES, tn), lambda i, j: (i, 0, j)),
        out_shape=jax.ShapeDtypeStruct((depth, SUBLANES, n), F32),
        compiler_params=_cparams("arbitrary", "arbitrary"),
        name="adaln_modulation",
    )(cond8, ada_w, ada_b.reshape(depth, 1, n))


def _qkv_kernel(x_ref, mod_ref, g_ref, w_ref, qg_ref, kg_ref, hm_ref, *rest, d, attn_scale, layer, with_cache):
    out_refs = rest[-5:] if with_cache else rest
    qp_ref, kp_ref, vp_ref = out_refs[:3]
    h = _norm_mod(x_ref[...], g_ref[...], mod_ref[0:1, :], mod_ref[1:2, :]).astype(BF16)
    qkv = jnp.dot(h, w_ref[...], preferred_element_type=F32)
    q, k, v = qkv[:, :d], qkv[:, d:2 * d], qkv[:, 2 * d:]

    def head_norm(t, gain):
        parts = []
        for c in range(d // MXU_DIM):
            tc = t[:, c * MXU_DIM:(c + 1) * MXU_DIM]
            ms = jnp.dot((tc * tc).astype(BF16), hm_ref[...], preferred_element_type=F32)
            parts.append(tc * lax.rsqrt(ms + EPS))
        return jnp.concatenate(parts, axis=-1) * gain

    q = head_norm(q, qg_ref[...])
    k = head_norm(k, kg_ref[...])
    if with_cache:
        for ref, val in ((out_refs[3], k), (out_refs[4], v)):
            seq = ref.shape[-1]
            for s in range(ref.shape[0]):
                vt = val[s * seq:(s + 1) * seq, :].T
                if ref.ndim == 3:
                    ref[s] = vt
                else:
                    for l in range(ref.shape[1]):
                        ref[s, l] = vt if l == layer else jnp.zeros_like(vt)
    qs = (q * attn_scale).astype(BF16)
    kb = k.astype(BF16)
    vb = v.astype(BF16)
    for p in range(d // HEAD_PAIR):
        sl = slice(p * HEAD_PAIR, (p + 1) * HEAD_PAIR)
        qp_ref[p] = qs[:, sl]
        kp_ref[p] = kb[:, sl]
        vp_ref[p] = vb[:, sl]


def _qkv(x, mod, group_of_tile, g, w_qkv, layer, n_layers, q_gain, k_gain, head_mean, cache, *,
         tm, head_dim, seq, with_cache):
    n, d = x.shape
    npairs = d // HEAD_PAIR
    pair_spec = pl.BlockSpec((npairs, tm, HEAD_PAIR), lambda i: (0, i, 0))
    out_specs = [pair_spec] * 3
    out_shape = [jax.ShapeDtypeStruct((npairs, n, HEAD_PAIR), BF16)] * 3
    in_specs = [
        pl.BlockSpec((tm, d), lambda i: (i, 0)),
        pl.BlockSpec((None, N_MOD, d), lambda i: (group_of_tile(i), 0, 0)),
        pl.BlockSpec((1, d), lambda i: (0, 0)),
        _single((None, d, 3 * d), lambda i: (layer, 0, 0)),
        pl.BlockSpec((1, d), lambda i: (0, 0)),
        pl.BlockSpec((1, d), lambda i: (0, 0)),
        pl.BlockSpec((MXU_DIM, MXU_DIM), lambda i: (0, 0)),
    ]
    args = [x, mod, g, w_qkv, q_gain, k_gain, head_mean]
    aliases = {}
    if with_cache:
        out_shape += [jax.ShapeDtypeStruct((n // seq, n_layers, d, seq), F32)] * 2
        if cache is None:
            out_specs += [pl.BlockSpec((tm // seq, n_layers, d, seq), lambda i: (i, 0, 0, 0))] * 2
        else:
            out_specs += [pl.BlockSpec((tm // seq, None, d, seq), lambda i: (i, layer, 0, 0))] * 2
            aliases = {len(args): 3, len(args) + 1: 4}
            in_specs += [pl.BlockSpec(memory_space=pl.ANY)] * 2
            args += list(cache)
    return pl.pallas_call(
        functools.partial(_qkv_kernel, d=d, attn_scale=head_dim ** -0.5, layer=layer, with_cache=with_cache),
        grid=(n // tm,),
        in_specs=in_specs,
        out_specs=out_specs,
        out_shape=out_shape,
        input_output_aliases=aliases,
        compiler_params=_cparams("arbitrary"),
        name="norm_qkv",
    )(*args)


def _pair_masks():
    lane = lax.broadcasted_iota(jnp.int32, (1, HEAD_PAIR), 1)
    first = lane < HEAD_PAIR // 2
    return first, jnp.logical_not(first)


def _dot_nt(a, b):
    return lax.dot_general(a, b, (((1,), (1,)), ((), ())), preferred_element_type=F32)


def _ctx_attn_kernel(q_ref, k_ref, v_ref, x_ref, mod_ref, wo_ref, o_ref, att_ref, *, npairs):
    first, second = _pair_masks()
    for p in range(npairs):
        qp, kp, vp = q_ref[p], k_ref[p], v_ref[p]
        outs = []
        for msk in (first, second):
            qh = jnp.where(msk, qp, jnp.zeros_like(qp))
            s = _dot_nt(qh, kp)
            m = jnp.max(s, axis=-1, keepdims=True)
            e = jnp.exp(s - m)
            l = jnp.sum(e, axis=-1, keepdims=True)
            outs.append(jnp.dot(e.astype(BF16), vp, preferred_element_type=F32) / l)
        att_ref[:, p * HEAD_PAIR:(p + 1) * HEAD_PAIR] = jnp.where(first, outs[0], outs[1]).astype(BF16)
    o = jnp.dot(att_ref[...], wo_ref[...], preferred_element_type=F32)
    o_ref[...] = x_ref[...] + mod_ref[2:3, :] * o


def _ctx_attention(qp, kp, vp, x, mod, wo, layer, *, seq):
    n, d = x.shape
    npairs = d // HEAD_PAIR
    pair_spec = pl.BlockSpec((npairs, seq, HEAD_PAIR), lambda b: (0, b, 0))
    return pl.pallas_call(
        functools.partial(_ctx_attn_kernel, npairs=npairs),
        grid=(n // seq,),
        in_specs=[
            pair_spec, pair_spec, pair_spec,
            pl.BlockSpec((seq, d), lambda b: (b, 0)),
            pl.BlockSpec((None, N_MOD, d), lambda b: (0, 0, 0)),
            _single((None, d, d), lambda b: (layer, 0, 0)),
        ],
        out_specs=pl.BlockSpec((seq, d), lambda b: (b, 0)),
        out_shape=jax.ShapeDtypeStruct((n, d), F32),
        scratch_shapes=[pltpu.VMEM((seq, d), BF16)],
        compiler_params=_cparams("arbitrary"),
        name="context_attention",
    )(qp, kp, vp, x, mod, wo)


def _na_attn_kernel(q_ref, k_ref, v_ref, ck_ref, cv_ref, strip_ref, x_ref, mod_ref, wo_ref, o_ref,
                    bias_ref, att_ref, *, npairs, rows, q_tile):
    first, second = _pair_masks()
    kr = min(WIN_R, rows)
    n_loc = rows * GRID_W
    tile_rows = q_tile // GRID_W
    win_rows = min(kr + tile_rows, rows)

    bias_ref[...] = jnp.full(bias_ref.shape, -jnp.inf, F32)

    def pair_body(p, carry):
        for hh in range(2):
            for qr in range(rows):
                rs = min(max(qr - kr // 2, 0), rows - kr)
                off = rs - qr + (WIN_R - 1)
                bias_ref[hh, qr * GRID_W:(qr + 1) * GRID_W, rs * GRID_W:(rs + kr) * GRID_W] = (
                    strip_ref[2 * p + hh, :, off * GRID_W:(off + kr) * GRID_W])
        ckp, cvp = ck_ref[p], cv_ref[p]
        for qt in range(n_loc // q_tile):
            rsl = slice(qt * q_tile, (qt + 1) * q_tile)
            ks = min(max(qt * tile_rows - kr // 2, 0), rows - win_rows)
            ksl = slice(ks * GRID_W, (ks + win_rows) * GRID_W)
            qp, kp, vp = q_ref[p, rsl, :], k_ref[p, ksl, :], v_ref[p, ksl, :]
            outs = []
            for hh, msk in enumerate((first, second)):
                qh = jnp.where(msk, qp, jnp.zeros_like(qp))
                s_loc = _dot_nt(qh, kp) + bias_ref[hh, rsl, ksl]
                s_ctx = _dot_nt(qh, ckp)
                m = jnp.maximum(jnp.max(s_loc, axis=-1, keepdims=True), jnp.max(s_ctx, axis=-1, keepdims=True))
                e_loc = jnp.exp(s_loc - m)
                e_ctx = jnp.exp(s_ctx - m)
                l = jnp.sum(e_loc, axis=-1, keepdims=True) + jnp.sum(e_ctx, axis=-1, keepdims=True)
                o = (jnp.dot(e_loc.astype(BF16), vp, preferred_element_type=F32)
                     + jnp.dot(e_ctx.astype(BF16), cvp, preferred_element_type=F32))
                outs.append(o / l)
            att_ref[p, rsl, :] = jnp.where(first, outs[0], outs[1]).astype(BF16)
        return carry

    lax.fori_loop(0, npairs, pair_body, 0)
    att = jnp.concatenate([att_ref[p] for p in range(npairs)], axis=-1)
    o = jnp.dot(att, wo_ref[...], preferred_element_type=F32)
    o_ref[...] = x_ref[...] + mod_ref[2:3, :] * o


def _na_attention(qp, kp, vp, ckp, cvp, strip, x, mod, wo, layer, *, n_batch):
    n, d = x.shape
    npairs = d // HEAD_PAIR
    n_loc = n // n_batch
    past = ckp.shape[3]
    rows = n_loc // GRID_W
    pair_spec = _single((npairs, n_loc, HEAD_PAIR), lambda b: (0, b, 0))
    ctx_spec = _single((None, None, npairs, past, HEAD_PAIR), lambda b: (layer, b, 0, 0, 0))
    return pl.pallas_call(
        functools.partial(_na_attn_kernel, npairs=npairs, rows=rows, q_tile=256),
        grid=(n_batch,),
        in_specs=[
            pair_spec, pair_spec, pair_spec, ctx_spec, ctx_spec,
            _single((None,) + strip.shape[1:], lambda b: (layer, 0, 0, 0)),
            _single((n_loc, d), lambda b: (b, 0)),
            pl.BlockSpec((None, N_MOD, d), lambda b: (1 + b, 0, 0)),
            _single((None, d, d), lambda b: (layer, 0, 0)),
        ],
        out_specs=_single((n_loc, d), lambda b: (b, 0)),
        out_shape=jax.ShapeDtypeStruct((n, d), F32),
        scratch_shapes=[pltpu.VMEM((2, n_loc, n_loc), F32), pltpu.VMEM((npairs, n_loc, HEAD_PAIR), BF16)],
        compiler_params=_cparams("arbitrary"),
        name="neighbourhood_attention",
    )(qp, kp, vp, ckp, cvp, strip, x, mod, wo)


def _rpb_strip(rpb):
    qc = np.arange(GRID_W)[:, None]
    kc = np.arange(GRID_W)[None, :]
    col_start = np.clip(qc - WIN_C // 2, 0, GRID_W - WIN_C)
    in_win = (kc >= col_start) & (kc < col_start + WIN_C)
    dc = np.clip(kc - qc, -(WIN_C - 1), WIN_C - 1) + (WIN_C - 1)
    sel = (dc[:, :, None] == np.arange(2 * WIN_C - 1)).astype(np.float32)
    t = jnp.einsum('lhdj,qkj->lhqdk', rpb.astype(F32), sel, precision=lax.Precision.HIGHEST)
    t = jnp.where(in_win[None, None, :, None, :], t, -jnp.inf)
    n_layers, h, ndr = rpb.shape[:3]
    return t.reshape(n_layers, h, GRID_W, ndr * GRID_W)


def _ffn_kernel(xa_ref, xb_ref, mod_ref, g_ref, w1_ref, w3_ref, w2_ref, oa_ref, ob_ref, t_ref, *, tiles_a):
    def tile(x_ref, o_ref):
        x = x_ref[...]
        h = _norm_mod(x, g_ref[...], mod_ref[3:4, :], mod_ref[4:5, :]).astype(BF16)
        for c in range(w1_ref.shape[1] // MXU_DIM):
            sl = slice(c * MXU_DIM, (c + 1) * MXU_DIM)
            a = jnp.dot(h, w1_ref[:, sl], preferred_element_type=F32)
            b = jnp.dot(h, w3_ref[:, sl], preferred_element_type=F32)
            t_ref[:, sl] = (a * jax.nn.sigmoid(a) * b).astype(BF16)
        o_ref[...] = x + mod_ref[5:6, :] * jnp.dot(t_ref[...], w2_ref[...], preferred_element_type=F32)

    first_stream = pl.program_id(0) < tiles_a
    pl.when(first_stream)(functools.partial(tile, xa_ref, oa_ref))
    pl.when(jnp.logical_not(first_stream))(functools.partial(tile, xb_ref, ob_ref))


def _ffn(xa, xb, mod, group_of_tile_b, g, w1, w3, w2, layer, *, tm):
    d = xa.shape[1]
    d_ff = w1.shape[-1]
    tiles_a, tiles_b = xa.shape[0] // tm, xb.shape[0] // tm
    a_map = lambda i: (jnp.minimum(i, tiles_a - 1), 0)
    b_map = lambda i: (jnp.maximum(i - tiles_a, 0), 0)
    group = lambda i: jnp.where(i < tiles_a, 0, group_of_tile_b(jnp.maximum(i - tiles_a, 0)))
    return pl.pallas_call(
        functools.partial(_ffn_kernel, tiles_a=tiles_a),
        grid=(tiles_a + tiles_b,),
        in_specs=[
            pl.BlockSpec((tm, d), a_map),
            pl.BlockSpec((tm, d), b_map),
            pl.BlockSpec((None, N_MOD, d), lambda i: (group(i), 0, 0)),
            pl.BlockSpec((1, d), lambda i: (0, 0)),
            _single((None, d, d_ff), lambda i: (layer, 0, 0)),
            _single((None, d, d_ff), lambda i: (layer, 0, 0)),
            _single((None, d_ff, d), lambda i: (layer, 0, 0)),
        ],
        out_specs=[pl.BlockSpec((tm, d), a_map), pl.BlockSpec((tm, d), b_map)],
        out_shape=[jax.ShapeDtypeStruct(xa.shape, F32), jax.ShapeDtypeStruct(xb.shape, F32)],
        scratch_shapes=[pltpu.VMEM((tm, d_ff), BF16)],
        compiler_params=_cparams("arbitrary"),
        name="swiglu_ffn",
    )(xa, xb, mod, g, w1, w3, w2)


def _lane_half_swap(x):
    n = x.shape[-1]
    lane = lax.broadcasted_iota(jnp.int32, (1, n), 1)
    return jnp.where(lane % LANES < LANES // 2, pltpu.roll(x, n - LANES // 2, axis=1),
                     pltpu.roll(x, LANES // 2, axis=1))


def _quad_pairs(t0, t1, t2, t3):
    lo = lax.broadcasted_iota(jnp.int32, (1, t0.shape[-1]), 1) % LANES < LANES // 2
    s0, s1, s2, s3 = (_lane_half_swap(t) for t in (t0, t1, t2, t3))
    even = (jnp.where(lo, t0, s1), jnp.where(lo, t2, s3))
    odd = (jnp.where(lo, s0, t1), jnp.where(lo, s2, t3))
    return even, odd


def _ssm_kernel(x_ref, mod_ref, g_ref, dskip_ref, win_ref, wcar_ref, kin_ref, a4_ref, h0_ref, wglu_ref, *rest,
                seq, n_seg, with_state):
    if with_state:
        o_ref, sre_ref, sim_ref = rest[:3]
    else:
        o_ref = rest[0]
    u_ref, y_ref, z_ref, sp_ref, f_ref, e_ref = rest[-6:]
    d = x_ref.shape[2]
    cps = win_ref.shape[0]
    nsq = a4_ref.shape[-1] // 2
    nst = cps * nsq
    nq = seq // QUAD
    steps_per_dir = (d // Q_CH) // cps
    tb = 4 * QUAD
    qb = tb // QUAD * SUBLANES
    dk = pl.program_id(1)
    kb = (dk % steps_per_dir) * cps
    fwd = dk // steps_per_dir == 0

    @pl.when(dk == 0)
    def _prepare():
        gain, shift, scale, dskip = g_ref[...], mod_ref[0], mod_ref[1], dskip_ref[...]

        def body(i, carry):
            t0 = pl.multiple_of(i * tb, tb)
            xt = jnp.swapaxes(x_ref[:, pl.ds(t0, tb), :], 0, 1)
            h = _norm_mod(xt, gain, shift, scale).reshape(tb // QUAD, QUAD, SUBLANES, d)
            rows = pl.ds(pl.multiple_of(i * qb, qb), qb)
            for val, dst in ((h, u_ref), (h * dskip, y_ref)):
                halves = _quad_pairs(*(val[:, s].reshape(qb, d) for s in range(QUAD)))
                for m in range(d // LANES):
                    for parity, (v01, v23) in enumerate(halves):
                        sl = slice(m * LANES, (m + 1) * LANES)
                        dst[2 * m + parity, rows, :LANES] = v01[:, sl].astype(dst.dtype)
                        dst[2 * m + parity, rows, LANES:] = v23[:, sl].astype(dst.dtype)
            return carry

        lax.fori_loop(0, seq // tb, body, 0)

    a_re = jnp.broadcast_to(jnp.concatenate([a4_ref[c][:, :nsq] for c in range(cps)], axis=-1), (SUBLANES, nst))
    a_im = jnp.broadcast_to(jnp.concatenate([a4_ref[c][:, nsq:] for c in range(cps)], axis=-1), (SUBLANES, nst))

    def chunk_cols(ref, c):
        return jnp.concatenate([ref[:, c * nsq:(c + 1) * nsq], ref[:, nst + c * nsq:nst + (c + 1) * nsq]], axis=-1)

    def scan(state, store, reverse):
        s_re, s_im = state
        pairs = range(nq // 2)
        for m in (reversed(pairs) if reverse else pairs):
            entering = {}
            for j in ((2 * m + 1, 2 * m) if reverse else (2 * m, 2 * m + 1)):
                rsl = slice(j * SUBLANES, (j + 1) * SUBLANES)
                entering[j] = (s_re, s_im)
                s_re, s_im = (a_re * s_re - a_im * s_im + z_ref[rsl, :nst],
                              a_re * s_im + a_im * s_re + z_ref[rsl, nst:])
            if store:
                psl = slice(2 * m * SUBLANES, (2 * m + 2) * SUBLANES)
                lo, hi = entering[2 * m], entering[2 * m + 1]
                sp_ref[psl, :nst] = jnp.concatenate([lo[0], hi[0]], axis=0).astype(BF16)
                sp_ref[psl, nst:] = jnp.concatenate([lo[1], hi[1]], axis=0).astype(BF16)
        return s_re, s_im

    def run_direction(reverse):
        for c in range(cps):
            z = jnp.dot(u_ref[kb + c], win_ref[c], preferred_element_type=F32)
            z_ref[:, c * nsq:(c + 1) * nsq] = z[:, :nsq]
            z_ref[:, nst + c * nsq:nst + (c + 1) * nsq] = z[:, nsq:]
        start = (jnp.concatenate([h0_ref[c][:, :nsq] for c in range(cps)], axis=-1),
                 jnp.concatenate([h0_ref[c][:, nsq:] for c in range(cps)], axis=-1))
        if n_seg > 1:
            zero = jnp.zeros((SUBLANES, nst), F32)
            e_re, e_im = scan((zero, zero), False, reverse)
            e_ref[:, :nst] = e_re
            e_ref[:, nst:] = e_im
            f_ref[:, :nst] = start[0]
            f_ref[:, nst:] = start[1]
            p_re, p_im = a_re[:1], a_im[:1]
            for _ in range(int(math.log2(nq))):
                p_re, p_im = p_re * p_re - p_im * p_im, 2.0 * p_re * p_im
            for b in range(SUBLANES // n_seg):
                for s in (range(n_seg - 2, -1, -1) if reverse else range(1, n_seg)):
                    dst = b * n_seg + s
                    src = dst + 1 if reverse else dst - 1
                    f_re, f_im = f_ref[src:src + 1, :nst], f_ref[src:src + 1, nst:]
                    f_ref[dst:dst + 1, :nst] = p_re * f_re - p_im * f_im + e_ref[src:src + 1, :nst]
                    f_ref[dst:dst + 1, nst:] = p_re * f_im + p_im * f_re + e_ref[src:src + 1, nst:]
            start = (f_ref[:, :nst], f_ref[:, nst:])
        fin_re, fin_im = scan(start, True, reverse)
        if with_state:
            for c in range(cps):
                sre_ref[c] = fin_re[:, c * nsq:(c + 1) * nsq]
                sim_ref[c] = fin_im[:, c * nsq:(c + 1) * nsq]
        for c in range(cps):
            y_ref[kb + c] = (y_ref[kb + c]
                             + _dot_nt(chunk_cols(sp_ref, c), wcar_ref[c])
                             + jnp.dot(u_ref[kb + c], kin_ref[c], preferred_element_type=F32))

    pl.when(fwd)(functools.partial(run_direction, False))
    pl.when(jnp.logical_not(fwd))(functools.partial(run_direction, True))

    @pl.when(dk == 2 * steps_per_dir - 1)
    def _finish():
        gate = mod_ref[2]
        fq = FINISH_QUADS * SUBLANES
        ft = FINISH_QUADS * QUAD

        def body(i, carry):
            rows = pl.ds(pl.multiple_of(i * fq, fq), fq)
            gathered = [jnp.concatenate([y_ref[2 * m + parity, rows, half] for m in range(d // LANES)], axis=-1)
                        for half in (slice(0, LANES), slice(LANES, 2 * LANES)) for parity in (0, 1)]
            (y0, y2), (y1, y3) = _quad_pairs(*gathered)
            y = jnp.stack([t.reshape(FINISH_QUADS, SUBLANES, d) for t in (y0, y1, y2, y3)], axis=1)
            z = jax.nn.gelu(y.reshape(ft * SUBLANES, d)).astype(BF16)
            gl = jnp.dot(z, wglu_ref[...], preferred_element_type=F32)
            out = (gl[:, :d] * jax.nn.sigmoid(gl[:, d:])).reshape(ft, SUBLANES, d) * gate
            for j in range(ft // tb):
                t0 = pl.multiple_of(i * ft + j * tb, tb)
                upd = jnp.swapaxes(out[j * tb:(j + 1) * tb], 0, 1)
                o_ref[:, pl.ds(t0, tb), :] = x_ref[:, pl.ds(t0, tb), :] + upd
            return carry

        lax.fori_loop(0, nq // FINISH_QUADS, body, 0)


def _ssm_mixer(x, mod8, g, dskip, w_in, w_car, k_in, a4, h0, wglu, layer, *, seq, n_seg, with_state):
    n_seq, _, d = x.shape
    n_groups = n_seq // SUBLANES
    n_chunks = d // Q_CH
    nsq = a4.shape[-1] // 2
    cps = CHUNKS_PER_STEP
    steps_per_dir = n_chunks // cps
    rows_q = seq // QUAD * SUBLANES
    dk_map = lambda gi, dk: (layer, dk // steps_per_dir, dk % steps_per_dir, 0, 0)
    seq_map = lambda gi, dk: (dk // steps_per_dir, dk % steps_per_dir, gi, 0)
    x_spec = _single((SUBLANES, seq, d), lambda gi, dk: (gi, 0, 0))
    out_specs = [x_spec]
    out_shape = [jax.ShapeDtypeStruct(x.shape, F32)]
    if with_state:
        st_spec = pl.BlockSpec((None, cps, SUBLANES, nsq), seq_map)
        out_specs += [st_spec, st_spec]
        out_shape += [jax.ShapeDtypeStruct((2, n_chunks, n_seq, nsq), F32)] * 2
    return pl.pallas_call(
        functools.partial(_ssm_kernel, seq=seq, n_seg=n_seg, with_state=with_state),
        grid=(n_groups, 2 * steps_per_dir),
        in_specs=[
            x_spec,
            pl.BlockSpec((N_MOD, SUBLANES, d), lambda gi, dk: (0, 0, 0)),
            pl.BlockSpec((1, d), lambda gi, dk: (0, 0)),
            pl.BlockSpec((1, d), lambda gi, dk: (0, 0)),
            pl.BlockSpec((None, None, cps, MXU_DIM, 2 * nsq), dk_map),
            pl.BlockSpec((None, None, cps, MXU_DIM, 2 * nsq), dk_map),
            pl.BlockSpec((None, None, cps, MXU_DIM, MXU_DIM), dk_map),
            pl.BlockSpec((None, None, cps, 1, 2 * nsq), dk_map),
            pl.BlockSpec((None, cps, SUBLANES, 2 * nsq), seq_map),
            _single((None, d, 2 * d), lambda gi, dk: (layer, 0, 0)),
        ],
        out_specs=out_specs,
        out_shape=out_shape,
        scratch_shapes=[
            pltpu.VMEM((n_chunks, rows_q, MXU_DIM), BF16),
            pltpu.VMEM((n_chunks, rows_q, MXU_DIM), F32),
            pltpu.VMEM((rows_q, 2 * cps * nsq), F32),
            pltpu.VMEM((rows_q, 2 * cps * nsq), BF16),
            pltpu.VMEM((SUBLANES, 2 * cps * nsq), F32),
            pltpu.VMEM((SUBLANES, 2 * cps * nsq), F32),
        ],
        compiler_params=_cparams("arbitrary", "arbitrary"),
        name="s5_mixer",
    )(x, mod8, g, dskip, w_in, w_car, k_in, a4, h0, wglu)


def _ssm_params(lam_re, lam_im, log_step, b_re, b_im, c_re, c_im):
    lam_re, lam_im = lam_re.astype(F32), lam_im.astype(F32)
    step = jnp.exp(log_step.astype(F32))[..., None]
    mag = jnp.exp(lam_re * step)
    a_re, a_im = mag * jnp.cos(lam_im * step), mag * jnp.sin(lam_im * step)
    den = lam_re * lam_re + lam_im * lam_im
    nr, ni = a_re - 1.0, a_im
    f_re = (nr * lam_re + ni * lam_im) / den
    f_im = (ni * lam_re - nr * lam_im) / den
    bt_re, bt_im = jnp.swapaxes(b_re.astype(F32), -1, -2), jnp.swapaxes(b_im.astype(F32), -1, -2)
    bb = (f_re[..., None, :] * bt_re - f_im[..., None, :] * bt_im,
          f_re[..., None, :] * bt_im + f_im[..., None, :] * bt_re)
    cc = (c_re.astype(F32), c_im.astype(F32))

    def cmul(x, y):
        return x[0] * y[0] - x[1] * y[1], x[0] * y[1] + x[1] * y[0]

    pw = [(jnp.ones_like(a_re), jnp.zeros_like(a_re))]
    for _ in range(QUAD):
        pw.append(cmul(pw[-1], (a_re, a_im)))

    def by_dir(n_fwd, n_bwd):
        return tuple(jnp.stack([pw[n_fwd][i][:, 0], pw[n_bwd][i][:, 1]], axis=1)[..., None, :] for i in range(2))

    n_layers, _, n_groups, c, p = bb[0].shape
    gq = Q_CH // c
    kk = n_groups // gq
    nsq = gq * p
    lead = (n_layers, 2, kk)
    g_of_tgc = (np.arange(MXU_DIM) // c) % gq

    def pack(ws):
        t = jnp.stack(ws, axis=2).reshape(n_layers, 2, QUAD, kk, gq, c, p).transpose(0, 1, 3, 2, 4, 5, 6)
        t = jnp.tile(t.reshape(*lead, MXU_DIM, p), (1, 1, 1, 1, gq))
        return jnp.where(g_of_tgc[:, None] == (np.arange(nsq) // p)[None, :], t, 0.0)

    w_in = [cmul(by_dir(QUAD - 1 - s, s), bb) for s in range(QUAD)]
    w_in = jnp.concatenate([pack([w[0] for w in w_in]), pack([w[1] for w in w_in])], axis=-1)
    w_car = [cmul(cc, by_dir(s + 1, QUAD - s)) for s in range(QUAD)]
    w_car = jnp.concatenate([pack([w[0] for w in w_car]), -pack([w[1] for w in w_car])], axis=-1)

    hi = lax.Precision.HIGHEST
    gc = gq * c
    same_g = (np.arange(gc) // c)[:, None] == (np.arange(gc) // c)[None, :]
    tiles = []
    for n in range(QUAD):
        cb = cmul(cc, tuple(t[..., None, :] for t in pw[n]))
        tap = (jnp.einsum('ldgkp,ldgcp->ldgck', cb[0], bb[0], precision=hi)
               - jnp.einsum('ldgkp,ldgcp->ldgck', cb[1], bb[1], precision=hi))
        tiles.append(jnp.where(same_g, jnp.tile(tap.reshape(*lead, gc, c), (1, 1, 1, 1, gq)), 0.0))
    zero = jnp.zeros_like(tiles[0][:, 0])

    def tile_of(d, s, t):
        n = t - s if d == 0 else s - t
        return tiles[n][:, d] if n >= 0 else zero

    k_in = jnp.stack([jnp.concatenate([jnp.concatenate([tile_of(d, s, t) for t in range(QUAD)], axis=-1)
                                       for s in range(QUAD)], axis=-2) for d in range(2)], axis=1)

    a4 = jnp.concatenate([pw[QUAD][0].reshape(*lead, 1, nsq), pw[QUAD][1].reshape(*lead, 1, nsq)], axis=-1)
    return w_in.astype(BF16), w_car.astype(BF16), k_in.astype(BF16), a4


def kernel(x_prompt, x_sample, cache_k, cache_v, state_ssm_re, state_ssm_im, c, c_ctx, norm_mix, norm_ffn, ada_w, ada_b, na_w_qkv, na_w_o, na_q_gain, na_k_gain, na_rpb, ssm_lambda_re, ssm_lambda_im, ssm_log_step, ssm_b_re, ssm_b_im, ssm_c_re, ssm_c_im, ssm_d, ssm_w_glu, ffn_w1, ffn_w3, ffn_w2):
    batch, seq, d = x_prompt.shape
    dec_batch, dec_seq, _ = x_sample.shape
    depth = ada_w.shape[0]
    heads, head_dim = cache_k.shape[3], cache_k.shape[4]
    past = cache_k.shape[2]
    npairs = d // HEAD_PAIR
    d_ff = ffn_w1.shape[-1]
    n_dir = state_ssm_re.shape[2]
    n_seg = dec_seq // seq
    assert seq % (FINISH_QUADS * QUAD) == 0 and dec_seq % seq == 0 and SUBLANES % n_seg == 0
    assert dec_batch * n_seg == SUBLANES and batch % SUBLANES == 0 and seq & (seq - 1) == 0
    assert d_ff % MXU_DIM == 0 and d % (Q_CH * CHUNKS_PER_STEP) == 0 and head_dim * 2 == HEAD_PAIR
    assert Q_CH * 2 == LANES and Q_CH % SSM_GROUP == 0

    cond8 = jnp.zeros((SUBLANES, d), F32).at[0].set(c_ctx).at[1:1 + dec_batch].set(c)
    mod = _modulation(cond8, ada_w, ada_b).reshape(depth, SUBLANES, N_MOD, d)

    tm = math.gcd(512, dec_seq)
    prompt_group = lambda i: 0
    sample_group = lambda i: 1 + (i * tm) // dec_seq

    head_mean = jnp.kron(jnp.eye(MXU_DIM // head_dim, dtype=F32),
                         jnp.full((head_dim, head_dim), 1.0 / head_dim, F32)).astype(BF16)

    w_qkv, w_o, w_glu = na_w_qkv.astype(BF16), na_w_o.astype(BF16), ssm_w_glu.astype(BF16)
    w1, w3, w2 = ffn_w1.astype(BF16), ffn_w3.astype(BF16), ffn_w2.astype(BF16)
    n_na = cache_k.shape[1]
    ckp = cache_k.astype(BF16).reshape(dec_batch, n_na, past, npairs, HEAD_PAIR).transpose(1, 0, 3, 2, 4)
    cvp = cache_v.astype(BF16).reshape(dec_batch, n_na, past, npairs, HEAD_PAIR).transpose(1, 0, 3, 2, 4)
    strip = _rpb_strip(na_rpb)
    s5_w = _ssm_params(ssm_lambda_re, ssm_lambda_im, ssm_log_step, ssm_b_re, ssm_b_im, ssm_c_re, ssm_c_im)
    a4 = s5_w[-1]

    xp = x_prompt.reshape(batch * seq, d)
    xs = x_sample.reshape(dec_batch * dec_seq, d)
    new_cache, new_sre, new_sim = None, [], []
    for i in range(depth):
        j = i // 2
        g_mix = norm_mix[i].reshape(1, d)
        if i % 2 == 0:
            q_gain = jnp.tile(na_q_gain[j], heads).reshape(1, d)
            k_gain = jnp.tile(na_k_gain[j], heads).reshape(1, d)
            qp, kp, vp, *new_cache = _qkv(xp, mod[i], prompt_group, g_mix, w_qkv, j, n_na, q_gain, k_gain,
                                          head_mean, new_cache, tm=tm, head_dim=head_dim, seq=seq, with_cache=True)
            xp = _ctx_attention(qp, kp, vp, xp, mod[i], w_o, j, seq=seq)
            qs, ks, vs = _qkv(xs, mod[i], sample_group, g_mix, w_qkv, j, n_na, q_gain, k_gain, head_mean, None,
                              tm=tm, head_dim=head_dim, seq=seq, with_cache=False)
            xs = _na_attention(qs, ks, vs, ckp, cvp, strip, xs, mod[i], w_o, j, n_batch=dec_batch)
        else:
            dskip = ssm_d[j].reshape(1, d)
            n_chunks = d // Q_CH
            ns = a4.shape[-1] // 2
            mod_p = jnp.broadcast_to(mod[i, 0][:, None, :], (N_MOD, SUBLANES, d))
            h0_p = jnp.zeros((n_dir, n_chunks, batch, 2 * ns), F32)
            xp, sre, sim = _ssm_mixer(xp.reshape(batch, seq, d), mod_p, g_mix, dskip, *s5_w, h0_p, w_glu, j,
                                      seq=seq, n_seg=1, with_state=True)
            xp = xp.reshape(batch * seq, d)
            new_sre.append(sre.transpose(2, 0, 1, 3).reshape(batch, n_dir, d // SSM_GROUP, SSM_STATE))
            new_sim.append(sim.transpose(2, 0, 1, 3).reshape(batch, n_dir, d // SSM_GROUP, SSM_STATE))
            mod_s = jnp.repeat(mod[i, 1:1 + dec_batch], n_seg, axis=0).transpose(1, 0, 2)
            s_re = state_ssm_re[:, j].astype(F32).reshape(dec_batch, n_dir, n_chunks, ns)
            s_im = state_ssm_im[:, j].astype(F32).reshape(dec_batch, n_dir, n_chunks, ns)
            s0 = jnp.concatenate([s_re, s_im], axis=-1)
            h0_s = jnp.zeros((dec_batch, n_seg, n_dir, n_chunks, 2 * ns), F32)
            h0_s = h0_s.at[:, 0, 0].set(s0[:, 0]).at[:, n_seg - 1, 1].set(s0[:, 1])
            h0_s = h0_s.reshape(dec_batch * n_seg, n_dir, n_chunks, 2 * ns).transpose(1, 2, 0, 3)
            (xs,) = _ssm_mixer(xs.reshape(dec_batch * n_seg, seq, d), mod_s, g_mix, dskip, *s5_w, h0_s, w_glu, j,
                               seq=seq, n_seg=n_seg, with_state=False)
            xs = xs.reshape(dec_batch * dec_seq, d)
        g_ffn = norm_ffn[i].reshape(1, d)
        xp, xs = _ffn(xp, xs, mod[i], sample_group, g_ffn, w1, w3, w2, i, tm=tm)
    new_k, new_v = (t.reshape(batch, n_na, heads, head_dim, seq).transpose(0, 1, 4, 2, 3) for t in new_cache)
    return (xp.reshape(batch, seq, d), xs.reshape(dec_batch, dec_seq, d), new_k, new_v,
            jnp.stack(new_sre, axis=1), jnp.stack(new_sim, axis=1))
```

```python
import functools
import math

import jax
import jax.numpy as jnp
import numpy as np
from jax import lax
from jax.experimental import pallas as pl
from jax.experimental.pallas import tpu as pltpu

F32 = jnp.float32
BF16 = jnp.bfloat16

EPS = 1e-6
N_MOD = 6
GRID_W = 64
WIN_R = 8
WIN_C = 16
SSM_GROUP = 16
SSM_STATE = 64

SUBLANES = 8
LANES = 128
MXU_DIM = 256
VMEM_LIMIT_BYTES = 56 * 1024 * 1024

HEAD_PAIR = LANES
QUAD = 4
Q_CH = MXU_DIM // QUAD
CHUNKS_PER_STEP = 4
FINISH_QUADS = 8


def _cparams(*sem):
    return pltpu.CompilerParams(dimension_semantics=sem, vmem_limit_bytes=VMEM_LIMIT_BYTES)


def _single(block_shape, index_map):
    return pl.BlockSpec(block_shape, index_map, pipeline_mode=pl.Buffered(1))


def _norm_mod(x, g, shift, scale):
    ms = jnp.mean(x * x, axis=-1, keepdims=True)
    y = x * lax.rsqrt(ms + EPS) * g
    return y * (1.0 + scale) + shift


def _mod_kernel(cond_ref, w_ref, b_ref, o_ref):
    c = cond_ref[...]
    a = (c * jax.nn.sigmoid(c)).astype(BF16)
    o_ref[...] = jnp.dot(a, w_ref[...].astype(BF16), preferred_element_type=F32) + b_ref[...]


def _modulation(cond8, ada_w, ada_b):
    depth, d, n = ada_w.shape
    tn = n // 4
    return pl.pallas_call(
        _mod_kernel,
        grid=(depth, n // tn),
        in_specs=[
            pl.BlockSpec((SUBLANES, d), lambda i, j: (0, 0)),
            pl.BlockSpec((None, d, tn), lambda i, j: (i, 0, j)),
            pl.BlockSpec((None, 1, tn), lambda i, j: (i, 0, j)),
        ],
        out_specs=pl.BlockSpec((None, SUBLANES, tn), lambda i, j: (i, 0, j)),
        out_shape=jax.ShapeDtypeStruct((depth, SUBLANES, n), F32),
        compiler_params=_cparams("arbitrary", "arbitrary"),
        name="adaln_modulation",
    )(cond8, ada_w, ada_b.reshape(depth, 1, n))


def _qkv_kernel(x_ref, mod_ref, g_ref, w_ref, qg_ref, kg_ref, hm_ref, *rest, d, attn_scale, layer, with_cache):
    out_refs = rest[-5:] if with_cache else rest
    qp_ref, kp_ref, vp_ref = out_refs[:3]
    h = _norm_mod(x_ref[...], g_ref[...], mod_ref[0:1, :], mod_ref[1:2, :]).astype(BF16)
    qkv = jnp.dot(h, w_ref[...], preferred_element_type=F32)
    q, k, v = qkv[:, :d], qkv[:, d:2 * d], qkv[:, 2 * d:]

    def head_norm(t, gain):
        parts = []
        for c in range(d // MXU_DIM):
            tc = t[:, c * MXU_DIM:(c + 1) * MXU_DIM]
            ms = jnp.dot((tc * tc).astype(BF16), hm_ref[...], preferred_element_type=F32)
            parts.append(tc * lax.rsqrt(ms + EPS))
        return jnp.concatenate(parts, axis=-1) * gain

    q = head_norm(q, qg_ref[...])
    k = head_norm(k, kg_ref[...])
    if with_cache:
        for ref, val in ((out_refs[3], k), (out_refs[4], v)):
            seq = ref.shape[-1]
            for s in range(ref.shape[0]):
                vt = val[s * seq:(s + 1) * seq, :].T
                if ref.ndim == 3:
                    ref[s] = vt
                else:
                    for l in range(ref.shape[1]):
                        ref[s, l] = vt if l == layer else jnp.zeros_like(vt)
    qs = (q * attn_scale).astype(BF16)
    kb = k.astype(BF16)
    vb = v.astype(BF16)
    for p in range(d // HEAD_PAIR):
        sl = slice(p * HEAD_PAIR, (p + 1) * HEAD_PAIR)
        qp_ref[p] = qs[:, sl]
        kp_ref[p] = kb[:, sl]
        vp_ref[p] = vb[:, sl]


def _qkv(x, mod, group_of_tile, g, w_qkv, layer, n_layers, q_gain, k_gain, head_mean, cache, *,
         tm, head_dim, seq, with_cache):
    n, d = x.shape
    npairs = d // HEAD_PAIR
    pair_spec = pl.BlockSpec((npairs, tm, HEAD_PAIR), lambda i: (0, i, 0))
    out_specs = [pair_spec] * 3
    out_shape = [jax.ShapeDtypeStruct((npairs, n, HEAD_PAIR), BF16)] * 3
    in_specs = [
        pl.BlockSpec((tm, d), lambda i: (i, 0)),
        pl.BlockSpec((None, N_MOD, d), lambda i: (group_of_tile(i), 0, 0)),
        pl.BlockSpec((1, d), lambda i: (0, 0)),
        _single((None, d, 3 * d), lambda i: (layer, 0, 0)),
        pl.BlockSpec((1, d), lambda i: (0, 0)),
        pl.BlockSpec((1, d), lambda i: (0, 0)),
        pl.BlockSpec((MXU_DIM, MXU_DIM), lambda i: (0, 0)),
    ]
    args = [x, mod, g, w_qkv, q_gain, k_gain, head_mean]
    aliases = {}
    if with_cache:
        out_shape += [jax.ShapeDtypeStruct((n // seq, n_layers, d, seq), F32)] * 2
        if cache is None:
            out_specs += [pl.BlockSpec((tm // seq, n_layers, d, seq), lambda i: (i, 0, 0, 0))] * 2
        else:
            out_specs += [pl.BlockSpec((tm // seq, None, d, seq), lambda i: (i, layer, 0, 0))] * 2
            aliases = {len(args): 3, len(args) + 1: 4}
            in_specs += [pl.BlockSpec(memory_space=pl.ANY)] * 2
            args += list(cache)
    return pl.pallas_call(
        functools.partial(_qkv_kernel, d=d, attn_scale=head_dim ** -0.5, layer=layer, with_cache=with_cache),
        grid=(n // tm,),
        in_specs=in_specs,
        out_specs=out_specs,
        out_shape=out_shape,
        input_output_aliases=aliases,
        compiler_params=_cparams("arbitrary"),
        name="norm_qkv",
    )(*args)


def _pair_masks():
    lane = lax.broadcasted_iota(jnp.int32, (1, HEAD_PAIR), 1)
    first = lane < HEAD_PAIR // 2
    return first, jnp.logical_not(first)


def _dot_nt(a, b):
    return lax.dot_general(a, b, (((1,), (1,)), ((), ())), preferred_element_type=F32)


def _ctx_attn_kernel(q_ref, k_ref, v_ref, x_ref, mod_ref, wo_ref, o_ref, att_ref, *, npairs):
    first, second = _pair_masks()
    for p in range(npairs):
        qp, kp, vp = q_ref[p], k_ref[p], v_ref[p]
        outs = []
        for msk in (first, second):
            qh = jnp.where(msk, qp, jnp.zeros_like(qp))
            s = _dot_nt(qh, kp)
            m = jnp.max(s, axis=-1, keepdims=True)
            e = jnp.exp(s - m)
            l = jnp.sum(e, axis=-1, keepdims=True)
            outs.append(jnp.dot(e.astype(BF16), vp, preferred_element_type=F32) / l)
        att_ref[:, p * HEAD_PAIR:(p + 1) * HEAD_PAIR] = jnp.where(first, outs[0], outs[1]).astype(BF16)
    o = jnp.dot(att_ref[...], wo_ref[...], preferred_element_type=F32)
    o_ref[...] = x_ref[...] + mod_ref[2:3, :] * o


def _ctx_attention(qp, kp, vp, x, mod, wo, layer, *, seq):
    n, d = x.shape
    npairs = d // HEAD_PAIR
    pair_spec = pl.BlockSpec((npairs, seq, HEAD_PAIR), lambda b: (0, b, 0))
    return pl.pallas_call(
        functools.partial(_ctx_attn_kernel, npairs=npairs),
        grid=(n // seq,),
        in_specs=[
            pair_spec, pair_spec, pair_spec,
            pl.BlockSpec((seq, d), lambda b: (b, 0)),
            pl.BlockSpec((None, N_MOD, d), lambda b: (0, 0, 0)),
            _single((None, d, d), lambda b: (layer, 0, 0)),
        ],
        out_specs=pl.BlockSpec((seq, d), lambda b: (b, 0)),
        out_shape=jax.ShapeDtypeStruct((n, d), F32),
        scratch_shapes=[pltpu.VMEM((seq, d), BF16)],
        compiler_params=_cparams("arbitrary"),
        name="context_attention",
    )(qp, kp, vp, x, mod, wo)


def _na_attn_kernel(q_ref, k_ref, v_ref, ck_ref, cv_ref, strip_ref, x_ref, mod_ref, wo_ref, o_ref,
                    bias_ref, att_ref, *, npairs, rows, q_tile):
    first, second = _pair_masks()
    kr = min(WIN_R, rows)
    n_loc = rows * GRID_W
    tile_rows = q_tile // GRID_W
    win_rows = min(kr + tile_rows, rows)

    bias_ref[...] = jnp.full(bias_ref.shape, -jnp.inf, F32)

    def pair_body(p, carry):
        for hh in range(2):
            for qr in range(rows):
                rs = min(max(qr - kr // 2, 0), rows - kr)
                off = rs - qr + (WIN_R - 1)
                bias_ref[hh, qr * GRID_W:(qr + 1) * GRID_W, rs * GRID_W:(rs + kr) * GRID_W] = (
                    strip_ref[2 * p + hh, :, off * GRID_W:(off + kr) * GRID_W])
        ckp, cvp = ck_ref[p], cv_ref[p]
        for qt in range(n_loc // q_tile):
            rsl = slice(qt * q_tile, (qt + 1) * q_tile)
            ks = min(max(qt * tile_rows - kr // 2, 0), rows - win_rows)
            ksl = slice(ks * GRID_W, (ks + win_rows) * GRID_W)
            qp, kp, vp = q_ref[p, rsl, :], k_ref[p, ksl, :], v_ref[p, ksl, :]
            outs = []
            for hh, msk in enumerate((first, second)):
                qh = jnp.where(msk, qp, jnp.zeros_like(qp))
                s_loc = _dot_nt(qh, kp) + bias_ref[hh, rsl, ksl]
                s_ctx = _dot_nt(qh, ckp)
                m = jnp.maximum(jnp.max(s_loc, axis=-1, keepdims=True), jnp.max(s_ctx, axis=-1, keepdims=True))
                e_loc = jnp.exp(s_loc - m)
                e_ctx = jnp.exp(s_ctx - m)
                l = jnp.sum(e_loc, axis=-1, keepdims=True) + jnp.sum(e_ctx, axis=-1, keepdims=True)
                o = (jnp.dot(e_loc.astype(BF16), vp, preferred_element_type=F32)
                     + jnp.dot(e_ctx.astype(BF16), cvp, preferred_element_type=F32))
                outs.append(o / l)
            att_ref[p, rsl, :] = jnp.where(first, outs[0], outs[1]).astype(BF16)
        return carry

    lax.fori_loop(0, npairs, pair_body, 0)
    att = jnp.concatenate([att_ref[p] for p in range(npairs)], axis=-1)
    o = jnp.dot(att, wo_ref[...], preferred_element_type=F32)
    o_ref[...] = x_ref[...] + mod_ref[2:3, :] * o


def _na_attention(qp, kp, vp, ckp, cvp, strip, x, mod, wo, layer, *, n_batch):
    n, d = x.shape
    npairs = d // HEAD_PAIR
    n_loc = n // n_batch
    past = ckp.shape[3]
    rows = n_loc // GRID_W
    pair_spec = _single((npairs, n_loc, HEAD_PAIR), lambda b: (0, b, 0))
    ctx_spec = _single((None, None, npairs, past, HEAD_PAIR), lambda b: (layer, b, 0, 0, 0))
    return pl.pallas_call(
        functools.partial(_na_attn_kernel, npairs=npairs, rows=rows, q_tile=256),
        grid=(n_batch,),
        in_specs=[
            pair_spec, pair_spec, pair_spec, ctx_spec, ctx_spec,
            _single((None,) + strip.shape[1:], lambda b: (layer, 0, 0, 0)),
            _single((n_loc, d), lambda b: (b, 0)),
            pl.BlockSpec((None, N_MOD, d), lambda b: (1 + b, 0, 0)),
            _single((None, d, d), lambda b: (layer, 0, 0)),
        ],
        out_specs=_single((n_loc, d), lambda b: (b, 0)),
        out_shape=jax.ShapeDtypeStruct((n, d), F32),
        scratch_shapes=[pltpu.VMEM((2, n_loc, n_loc), F32), pltpu.VMEM((npairs, n_loc, HEAD_PAIR), BF16)],
        compiler_params=_cparams("arbitrary"),
        name="neighbourhood_attention",
    )(qp, kp, vp, ckp, cvp, strip, x, mod, wo)


def _rpb_strip(rpb):
    qc = np.arange(GRID_W)[:, None]
    kc = np.arange(GRID_W)[None, :]
    col_start = np.clip(qc - WIN_C // 2, 0, GRID_W - WIN_C)
    in_win = (kc >= col_start) & (kc < col_start + WIN_C)
    dc = np.clip(kc - qc, -(WIN_C - 1), WIN_C - 1) + (WIN_C - 1)
    sel = (dc[:, :, None] == np.arange(2 * WIN_C - 1)).astype(np.float32)
    t = jnp.einsum('lhdj,qkj->lhqdk', rpb.astype(F32), sel, precision=lax.Precision.HIGHEST)
    t = jnp.where(in_win[None, None, :, None, :], t, -jnp.inf)
    n_layers, h, ndr = rpb.shape[:3]
    return t.reshape(n_layers, h, GRID_W, ndr * GRID_W)


def _ffn_kernel(xa_ref, xb_ref, mod_ref, g_ref, w1_ref, w3_ref, w2_ref, oa_ref, ob_ref, t_ref, *, tiles_a):
    def tile(x_ref, o_ref):
        x = x_ref[...]
        h = _norm_mod(x, g_ref[...], mod_ref[3:4, :], mod_ref[4:5, :]).astype(BF16)
        for c in range(w1_ref.shape[1] // MXU_DIM):
            sl = slice(c * MXU_DIM, (c + 1) * MXU_DIM)
            a = jnp.dot(h, w1_ref[:, sl], preferred_element_type=F32)
            b = jnp.dot(h, w3_ref[:, sl], preferred_element_type=F32)
            t_ref[:, sl] = (a * jax.nn.sigmoid(a) * b).astype(BF16)
        o_ref[...] = x + mod_ref[5:6, :] * jnp.dot(t_ref[...], w2_ref[...], preferred_element_type=F32)

    first_stream = pl.program_id(0) < tiles_a
    pl.when(first_stream)(functools.partial(tile, xa_ref, oa_ref))
    pl.when(jnp.logical_not(first_stream))(functools.partial(tile, xb_ref, ob_ref))


def _ffn(xa, xb, mod, group_of_tile_b, g, w1, w3, w2, layer, *, tm):
    d = xa.shape[1]
    d_ff = w1.shape[-1]
    tiles_a, tiles_b = xa.shape[0] // tm, xb.shape[0] // tm
    a_map = lambda i: (jnp.minimum(i, tiles_a - 1), 0)
    b_map = lambda i: (jnp.maximum(i - tiles_a, 0), 0)
    group = lambda i: jnp.where(i < tiles_a, 0, group_of_tile_b(jnp.maximum(i - tiles_a, 0)))
    return pl.pallas_call(
        functools.partial(_ffn_kernel, tiles_a=tiles_a),
        grid=(tiles_a + tiles_b,),
        in_specs=[
            pl.BlockSpec((tm, d), a_map),
            pl.BlockSpec((tm, d), b_map),
            pl.BlockSpec((None, N_MOD, d), lambda i: (group(i), 0, 0)),
            pl.BlockSpec((1, d), lambda i: (0, 0)),
            _single((None, d, d_ff), lambda i: (layer, 0, 0)),
            _single((None, d, d_ff), lambda i: (layer, 0, 0)),
            _single((None, d_ff, d), lambda i: (layer, 0, 0)),
        ],
        out_specs=[pl.BlockSpec((tm, d), a_map), pl.BlockSpec((tm, d), b_map)],
        out_shape=[jax.ShapeDtypeStruct(xa.shape, F32), jax.ShapeDtypeStruct(xb.shape, F32)],
        scratch_shapes=[pltpu.VMEM((tm, d_ff), BF16)],
        compiler_params=_cparams("arbitrary"),
        name="swiglu_ffn",
    )(xa, xb, mod, g, w1, w3, w2)


def _lane_half_swap(x):
    n = x.shape[-1]
    lane = lax.broadcasted_iota(jnp.int32, (1, n), 1)
    return jnp.where(lane % LANES < LANES // 2, pltpu.roll(x, n - LANES // 2, axis=1),
                     pltpu.roll(x, LANES // 2, axis=1))


def _quad_pairs(t0, t1, t2, t3):
    lo = lax.broadcasted_iota(jnp.int32, (1, t0.shape[-1]), 1) % LANES < LANES // 2
    s0, s1, s2, s3 = (_lane_half_swap(t) for t in (t0, t1, t2, t3))
    even = (jnp.where(lo, t0, s1), jnp.where(lo, t2, s3))
    odd = (jnp.where(lo, s0, t1), jnp.where(lo, s2, t3))
    return even, odd


def _ssm_kernel(x_ref, mod_ref, g_ref, dskip_ref, win_ref, wcar_ref, kin_ref, a4_ref, h0_ref, wglu_ref, *rest,
                seq, n_seg, with_state):
    if with_state:
        o_ref, sre_ref, sim_ref = rest[:3]
    else:
        o_ref = rest[0]
    u_ref, y_ref, z_ref, sp_ref, f_ref, e_ref = rest[-6:]
    d = x_ref.shape[2]
    cps = win_ref.shape[0]
    nsq = a4_ref.shape[-1] // 2
    nst = cps * nsq
    nq = seq // QUAD
    steps_per_dir = (d // Q_CH) // cps
    tb = 4 * QUAD
    qb = tb // QUAD * SUBLANES
    dk = pl.program_id(1)
    kb = (dk % steps_per_dir) * cps
    fwd = dk // steps_per_dir == 0

    @pl.when(dk == 0)
    def _prepare():
        gain, shift, scale = g_ref[...], mod_ref[0], mod_ref[1]

        def body(i, carry):
            t0 = pl.multiple_of(i * tb, tb)
            xt = jnp.swapaxes(x_ref[:, pl.ds(t0, tb), :], 0, 1)
            h = _norm_mod(xt, gain, shift, scale).reshape(tb // QUAD, QUAD, SUBLANES, d)
            rows = pl.ds(pl.multiple_of(i * qb, qb), qb)
            halves = _quad_pairs(*(h[:, s].reshape(qb, d) for s in range(QUAD)))
            for m in range(d // LANES):
                for parity, (v01, v23) in enumerate(halves):
                    k = 2 * m + parity
                    sl = slice(m * LANES, (m + 1) * LANES)
                    u_ref[k, rows, :LANES] = v01[:, sl].astype(BF16)
                    u_ref[k, rows, LANES:] = v23[:, sl].astype(BF16)
                    y_ref[k, rows, :LANES] = v01[:, sl] * dskip_ref[k, :, :LANES]
                    y_ref[k, rows, LANES:] = v23[:, sl] * dskip_ref[k, :, LANES:]
            return carry

        lax.fori_loop(0, seq // tb, body, 0)

    pp = win_ref.shape[-1] // 2
    gq = nsq // pp
    src = lax.broadcasted_iota(jnp.int32, (2 * pp, 2 * nsq), 0)
    col = lax.broadcasted_iota(jnp.int32, (2 * pp, 2 * nsq), 1)
    spread = ((col % pp == src % pp) & ((col >= nsq) == (src >= pp))).astype(BF16)
    row_g = (lax.broadcasted_iota(jnp.int32, (MXU_DIM, 2 * nsq), 0) // (Q_CH // gq)) % gq
    col_g = (lax.broadcasted_iota(jnp.int32, (MXU_DIM, 2 * nsq), 1) % nsq) // pp

    def block_diag(w):
        return jnp.where(row_g == col_g, jnp.dot(w, spread, preferred_element_type=F32), 0.0).astype(BF16)

    a_re = jnp.broadcast_to(jnp.concatenate([a4_ref[c][:, :nsq] for c in range(cps)], axis=-1), (SUBLANES, nst))
    a_im = jnp.broadcast_to(jnp.concatenate([a4_ref[c][:, nsq:] for c in range(cps)], axis=-1), (SUBLANES, nst))

    def chunk_cols(ref, c):
        return jnp.concatenate([ref[:, c * nsq:(c + 1) * nsq], ref[:, nst + c * nsq:nst + (c + 1) * nsq]], axis=-1)

    def scan(state, store, reverse):
        s_re, s_im = state
        pairs = range(nq // 2)
        for m in (reversed(pairs) if reverse else pairs):
            entering = {}
            for j in ((2 * m + 1, 2 * m) if reverse else (2 * m, 2 * m + 1)):
                rsl = slice(j * SUBLANES, (j + 1) * SUBLANES)
                entering[j] = (s_re, s_im)
                s_re, s_im = (a_re * s_re - a_im * s_im + z_ref[rsl, :nst],
                              a_re * s_im + a_im * s_re + z_ref[rsl, nst:])
            if store:
                psl = slice(2 * m * SUBLANES, (2 * m + 2) * SUBLANES)
                lo, hi = entering[2 * m], entering[2 * m + 1]
                sp_ref[psl, :nst] = jnp.concatenate([lo[0], hi[0]], axis=0).astype(BF16)
                sp_ref[psl, nst:] = jnp.concatenate([lo[1], hi[1]], axis=0).astype(BF16)
        return s_re, s_im

    def run_direction(reverse):
        for c in range(cps):
            z = jnp.dot(u_ref[kb + c], block_diag(win_ref[c]), preferred_element_type=F32)
            z_ref[:, c * nsq:(c + 1) * nsq] = z[:, :nsq]
            z_ref[:, nst + c * nsq:nst + (c + 1) * nsq] = z[:, nsq:]
        start = (jnp.concatenate([h0_ref[c][:, :nsq] for c in range(cps)], axis=-1),
                 jnp.concatenate([h0_ref[c][:, nsq:] for c in range(cps)], axis=-1))
        if n_seg > 1:
            zero = jnp.zeros((SUBLANES, nst), F32)
            e_re, e_im = scan((zero, zero), False, reverse)
            e_ref[:, :nst] = e_re
            e_ref[:, nst:] = e_im
            f_ref[:, :nst] = start[0]
            f_ref[:, nst:] = start[1]
            p_re, p_im = a_re[:1], a_im[:1]
            for _ in range(int(math.log2(nq))):
                p_re, p_im = p_re * p_re - p_im * p_im, 2.0 * p_re * p_im
            for b in range(SUBLANES // n_seg):
                for s in (range(n_seg - 2, -1, -1) if reverse else range(1, n_seg)):
                    dst = b * n_seg + s
                    src = dst + 1 if reverse else dst - 1
                    f_re, f_im = f_ref[src:src + 1, :nst], f_ref[src:src + 1, nst:]
                    f_ref[dst:dst + 1, :nst] = p_re * f_re - p_im * f_im + e_ref[src:src + 1, :nst]
                    f_ref[dst:dst + 1, nst:] = p_re * f_im + p_im * f_re + e_ref[src:src + 1, nst:]
            start = (f_ref[:, :nst], f_ref[:, nst:])
        fin_re, fin_im = scan(start, True, reverse)
        if with_state:
            for c in range(cps):
                sre_ref[c] = fin_re[:, c * nsq:(c + 1) * nsq]
                sim_ref[c] = fin_im[:, c * nsq:(c + 1) * nsq]
        for c in range(cps):
            y_ref[kb + c] = (y_ref[kb + c]
                             + _dot_nt(chunk_cols(sp_ref, c), block_diag(wcar_ref[c]))
                             + jnp.dot(u_ref[kb + c], kin_ref[c], preferred_element_type=F32))

    pl.when(fwd)(functools.partial(run_direction, False))
    pl.when(jnp.logical_not(fwd))(functools.partial(run_direction, True))

    @pl.when(dk == 2 * steps_per_dir - 1)
    def _finish():
        gate = mod_ref[2]
        fq = FINISH_QUADS * SUBLANES
        ft = FINISH_QUADS * QUAD

        def body(i, carry):
            rows = pl.ds(pl.multiple_of(i * fq, fq), fq)
            gathered = [jnp.concatenate([y_ref[2 * m + parity, rows, half] for m in range(d // LANES)], axis=-1)
                        for half in (slice(0, LANES), slice(LANES, 2 * LANES)) for parity in (0, 1)]
            (y0, y2), (y1, y3) = _quad_pairs(*gathered)
            y = jnp.stack([t.reshape(FINISH_QUADS, SUBLANES, d) for t in (y0, y1, y2, y3)], axis=1)
            z = jax.nn.gelu(y.reshape(ft * SUBLANES, d)).astype(BF16)
            gl = jnp.dot(z, wglu_ref[...], preferred_element_type=F32)
            out = (gl[:, :d] * jax.nn.sigmoid(gl[:, d:])).reshape(ft, SUBLANES, d) * gate
            for j in range(ft // tb):
                t0 = pl.multiple_of(i * ft + j * tb, tb)
                upd = jnp.swapaxes(out[j * tb:(j + 1) * tb], 0, 1)
                o_ref[:, pl.ds(t0, tb), :] = x_ref[:, pl.ds(t0, tb), :] + upd
            return carry

        lax.fori_loop(0, nq // FINISH_QUADS, body, 0)


def _ssm_mixer(x, mod8, g, dskip, w_in, w_car, k_in, a4, h0, wglu, layer, *, seq, n_seg, with_state):
    n_seq, _, d = x.shape
    n_groups = n_seq // SUBLANES
    n_chunks = d // Q_CH
    nsq = a4.shape[-1] // 2
    cps = CHUNKS_PER_STEP
    steps_per_dir = n_chunks // cps
    rows_q = seq // QUAD * SUBLANES
    dk_map = lambda gi, dk: (layer, dk // steps_per_dir, dk % steps_per_dir, 0, 0)
    seq_map = lambda gi, dk: (dk // steps_per_dir, dk % steps_per_dir, gi, 0)
    x_spec = _single((SUBLANES, seq, d), lambda gi, dk: (gi, 0, 0))
    out_specs = [x_spec]
    out_shape = [jax.ShapeDtypeStruct(x.shape, F32)]
    if with_state:
        st_spec = pl.BlockSpec((None, cps, SUBLANES, nsq), seq_map)
        out_specs += [st_spec, st_spec]
        out_shape += [jax.ShapeDtypeStruct((2, n_chunks, n_seq, nsq), F32)] * 2
    return pl.pallas_call(
        functools.partial(_ssm_kernel, seq=seq, n_seg=n_seg, with_state=with_state),
        grid=(n_groups, 2 * steps_per_dir),
        in_specs=[
            pl.BlockSpec((SUBLANES, seq, d), lambda gi, dk: (gi, 0, 0)),
            pl.BlockSpec((N_MOD, SUBLANES, d), lambda gi, dk: (0, 0, 0)),
            pl.BlockSpec((1, d), lambda gi, dk: (0, 0)),
            pl.BlockSpec((n_chunks, 1, MXU_DIM), lambda gi, dk: (0, 0, 0)),
            pl.BlockSpec((None, None, cps) + w_in.shape[-2:], dk_map),
            pl.BlockSpec((None, None, cps) + w_car.shape[-2:], dk_map),
            pl.BlockSpec((None, None, cps, MXU_DIM, MXU_DIM), dk_map),
            pl.BlockSpec((None, None, cps, 1, 2 * nsq), dk_map),
            pl.BlockSpec((None, cps, SUBLANES, 2 * nsq), seq_map),
            _single((None, d, 2 * d), lambda gi, dk: (layer, 0, 0)),
        ],
        out_specs=out_specs,
        out_shape=out_shape,
        scratch_shapes=[
            pltpu.VMEM((n_chunks, rows_q, MXU_DIM), BF16),
            pltpu.VMEM((n_chunks, rows_q, MXU_DIM), F32),
            pltpu.VMEM((rows_q, 2 * cps * nsq), F32),
            pltpu.VMEM((rows_q, 2 * cps * nsq), BF16),
            pltpu.VMEM((SUBLANES, 2 * cps * nsq), F32),
            pltpu.VMEM((SUBLANES, 2 * cps * nsq), F32),
        ],
        compiler_params=_cparams("arbitrary", "arbitrary"),
        name="s5_mixer",
    )(x, mod8, g, dskip, w_in, w_car, k_in, a4, h0, wglu)


def _ssm_params(lam_re, lam_im, log_step, b_re, b_im, c_re, c_im):
    lam_re, lam_im = lam_re.astype(F32), lam_im.astype(F32)
    step = jnp.exp(log_step.astype(F32))[..., None]
    mag = jnp.exp(lam_re * step)
    a_re, a_im = mag * jnp.cos(lam_im * step), mag * jnp.sin(lam_im * step)
    den = lam_re * lam_re + lam_im * lam_im
    nr, ni = a_re - 1.0, a_im
    f_re = (nr * lam_re + ni * lam_im) / den
    f_im = (ni * lam_re - nr * lam_im) / den
    bt_re, bt_im = jnp.swapaxes(b_re.astype(F32), -1, -2), jnp.swapaxes(b_im.astype(F32), -1, -2)
    bb = (f_re[..., None, :] * bt_re - f_im[..., None, :] * bt_im,
          f_re[..., None, :] * bt_im + f_im[..., None, :] * bt_re)
    cc = (c_re.astype(F32), c_im.astype(F32))

    def cmul(x, y):
        return x[0] * y[0] - x[1] * y[1], x[0] * y[1] + x[1] * y[0]

    pw = [(jnp.ones_like(a_re), jnp.zeros_like(a_re))]
    for _ in range(QUAD):
        pw.append(cmul(pw[-1], (a_re, a_im)))

    def by_dir(n_fwd, n_bwd):
        return tuple(jnp.stack([pw[n_fwd][i][:, 0], pw[n_bwd][i][:, 1]], axis=1)[..., None, :] for i in range(2))

    n_layers, _, n_groups, c, p = bb[0].shape
    gq = Q_CH // c
    kk = n_groups // gq
    nsq = gq * p
    lead = (n_layers, 2, kk)

    def pack(ws, sign):
        def rows(i):
            t = jnp.stack([w[i] for w in ws], axis=2).reshape(n_layers, 2, QUAD, kk, gq, c, p)
            return t.transpose(0, 1, 3, 2, 4, 5, 6).reshape(*lead, MXU_DIM, p)
        return jnp.concatenate([rows(0), sign * rows(1)], axis=-1).astype(BF16)

    w_in = pack([cmul(by_dir(QUAD - 1 - s, s), bb) for s in range(QUAD)], 1.0)
    w_car = pack([cmul(cc, by_dir(s + 1, QUAD - s)) for s in range(QUAD)], -1.0)

    gc = gq * c
    same_g = (np.arange(gc) // c)[:, None] == (np.arange(gc) // c)[None, :]
    cb = [cmul(cc, tuple(t[..., None, :] for t in pw[n])) for n in range(QUAD)]
    cb = jnp.concatenate([jnp.concatenate([t[0] for t in cb], axis=-2),
                          -jnp.concatenate([t[1] for t in cb], axis=-2)], axis=-1)
    taps = jnp.einsum('ldgkp,ldgcp->ldgck', cb, jnp.concatenate(bb, axis=-1), precision=lax.Precision.HIGHEST)
    tiles = []
    for n in range(QUAD):
        tap = taps[..., n * c:(n + 1) * c]
        tiles.append(jnp.where(same_g, jnp.tile(tap.reshape(*lead, gc, c), (1, 1, 1, 1, gq)), 0.0))
    zero = jnp.zeros_like(tiles[0][:, 0])

    def tile_of(d, s, t):
        n = t - s if d == 0 else s - t
        return tiles[n][:, d] if n >= 0 else zero

    k_in = jnp.stack([jnp.concatenate([jnp.concatenate([tile_of(d, s, t) for t in range(QUAD)], axis=-1)
                                       for s in range(QUAD)], axis=-2) for d in range(2)], axis=1)

    a4 = jnp.concatenate([pw[QUAD][0].reshape(*lead, 1, nsq), pw[QUAD][1].reshape(*lead, 1, nsq)], axis=-1)
    return w_in.astype(BF16), w_car.astype(BF16), k_in.astype(BF16), a4


def kernel(x_prompt, x_sample, cache_k, cache_v, state_ssm_re, state_ssm_im, c, c_ctx, norm_mix, norm_ffn, ada_w, ada_b, na_w_qkv, na_w_o, na_q_gain, na_k_gain, na_rpb, ssm_lambda_re, ssm_lambda_im, ssm_log_step, ssm_b_re, ssm_b_im, ssm_c_re, ssm_c_im, ssm_d, ssm_w_glu, ffn_w1, ffn_w3, ffn_w2):
    batch, seq, d = x_prompt.shape
    dec_batch, dec_seq, _ = x_sample.shape
    depth = ada_w.shape[0]
    heads, head_dim = cache_k.shape[3], cache_k.shape[4]
    past = cache_k.shape[2]
    npairs = d // HEAD_PAIR
    d_ff = ffn_w1.shape[-1]
    n_dir = state_ssm_re.shape[2]
    n_seg = dec_seq // seq
    assert seq % (FINISH_QUADS * QUAD) == 0 and dec_seq % seq == 0 and SUBLANES % n_seg == 0
    assert dec_batch * n_seg == SUBLANES and batch % SUBLANES == 0 and seq & (seq - 1) == 0
    assert d_ff % MXU_DIM == 0 and d % (Q_CH * CHUNKS_PER_STEP) == 0 and head_dim * 2 == HEAD_PAIR
    assert Q_CH * 2 == LANES and Q_CH % SSM_GROUP == 0

    cond8 = jnp.zeros((SUBLANES, d), F32).at[0].set(c_ctx).at[1:1 + dec_batch].set(c)
    mod = _modulation(cond8, ada_w, ada_b).reshape(depth, SUBLANES, N_MOD, d)

    tm = math.gcd(512, dec_seq)
    prompt_group = lambda i: 0
    sample_group = lambda i: 1 + (i * tm) // dec_seq

    head_mean = jnp.kron(jnp.eye(MXU_DIM // head_dim, dtype=F32),
                         jnp.full((head_dim, head_dim), 1.0 / head_dim, F32)).astype(BF16)

    w_qkv, w_o, w_glu = na_w_qkv.astype(BF16), na_w_o.astype(BF16), ssm_w_glu.astype(BF16)
    w1, w3, w2 = ffn_w1.astype(BF16), ffn_w3.astype(BF16), ffn_w2.astype(BF16)
    n_na = cache_k.shape[1]
    ckp = cache_k.astype(BF16).reshape(dec_batch, n_na, past, npairs, HEAD_PAIR).transpose(1, 0, 3, 2, 4)
    cvp = cache_v.astype(BF16).reshape(dec_batch, n_na, past, npairs, HEAD_PAIR).transpose(1, 0, 3, 2, 4)
    strip = _rpb_strip(na_rpb)
    s5_w = _ssm_params(ssm_lambda_re, ssm_lambda_im, ssm_log_step, ssm_b_re, ssm_b_im, ssm_c_re, ssm_c_im)
    a4 = s5_w[-1]

    xp = x_prompt.reshape(batch * seq, d)
    xs = x_sample.reshape(dec_batch * dec_seq, d)
    new_cache, new_sre, new_sim = None, [], []
    for i in range(depth):
        j = i // 2
        g_mix = norm_mix[i].reshape(1, d)
        if i % 2 == 0:
            q_gain = jnp.tile(na_q_gain[j], heads).reshape(1, d)
            k_gain = jnp.tile(na_k_gain[j], heads).reshape(1, d)
            qp, kp, vp, *new_cache = _qkv(xp, mod[i], prompt_group, g_mix, w_qkv, j, n_na, q_gain, k_gain,
                                          head_mean, new_cache, tm=tm, head_dim=head_dim, seq=seq, with_cache=True)
            xp = _ctx_attention(qp, kp, vp, xp, mod[i], w_o, j, seq=seq)
            qs, ks, vs = _qkv(xs, mod[i], sample_group, g_mix, w_qkv, j, n_na, q_gain, k_gain, head_mean, None,
                              tm=tm, head_dim=head_dim, seq=seq, with_cache=False)
            xs = _na_attention(qs, ks, vs, ckp, cvp, strip, xs, mod[i], w_o, j, n_batch=dec_batch)
        else:
            n_chunks = d // Q_CH
            dskip = jnp.tile(ssm_d[j].astype(F32).reshape(n_chunks, 1, Q_CH), (1, 1, QUAD))
            ns = a4.shape[-1] // 2
            mod_p = jnp.broadcast_to(mod[i, 0][:, None, :], (N_MOD, SUBLANES, d))
            h0_p = jnp.zeros((n_dir, n_chunks, batch, 2 * ns), F32)
            xp, sre, sim = _ssm_mixer(xp.reshape(batch, seq, d), mod_p, g_mix, dskip, *s5_w, h0_p, w_glu, j,
                                      seq=seq, n_seg=1, with_state=True)
            xp = xp.reshape(batch * seq, d)
            new_sre.append(sre.transpose(2, 0, 1, 3).reshape(batch, n_dir, d // SSM_GROUP, SSM_STATE))
            new_sim.append(sim.transpose(2, 0, 1, 3).reshape(batch, n_dir, d // SSM_GROUP, SSM_STATE))
            mod_s = jnp.repeat(mod[i, 1:1 + dec_batch], n_seg, axis=0).transpose(1, 0, 2)
            s_re = state_ssm_re[:, j].astype(F32).reshape(dec_batch, n_dir, n_chunks, ns)
            s_im = state_ssm_im[:, j].astype(F32).reshape(dec_batch, n_dir, n_chunks, ns)
            s0 = jnp.concatenate([s_re, s_im], axis=-1)
            h0_s = jnp.zeros((dec_batch, n_seg, n_dir, n_chunks, 2 * ns), F32)
            h0_s = h0_s.at[:, 0, 0].set(s0[:, 0]).at[:, n_seg - 1, 1].set(s0[:, 1])
            h0_s = h0_s.reshape(dec_batch * n_seg, n_dir, n_chunks, 2 * ns).transpose(1, 2, 0, 3)
            (xs,) = _ssm_mixer(xs.reshape(dec_batch * n_seg, seq, d), mod_s, g_mix, dskip, *s5_w, h0_s, w_glu, j,
                               seq=seq, n_seg=n_seg, with_state=False)
            xs = xs.reshape(dec_batch * dec_seq, d)
        g_ffn = norm_ffn[i].reshape(1, d)
        xp, xs = _ffn(xp, xs, mod[i], sample_group, g_ffn, w1, w3, w2, i, tm=tm)
    new_k, new_v = (t.reshape(batch, n_na, heads, head_dim, seq).transpose(0, 1, 4, 2, 3) for t in new_cache)
    return (xp.reshape(batch, seq, d), xs.reshape(dec_batch, dec_seq, d), new_k, new_v,
            jnp.stack(new_sre, axis=1), jnp.stack(new_sim, axis=1))
```

```python
import functools
import math

import jax
import jax.numpy as jnp
import numpy as np
from jax import lax
from jax.experimental import pallas as pl
from jax.experimental.pallas import tpu as pltpu

F32 = jnp.float32
BF16 = jnp.bfloat16

EPS = 1e-6
N_MOD = 6
GRID_W = 64
WIN_R = 8
WIN_C = 16
SSM_GROUP = 16
SSM_STATE = 64

SUBLANES = 8
LANES = 128
MXU_DIM = 256
VMEM_LIMIT_BYTES = 56 * 1024 * 1024

HEAD_PAIR = LANES
QUAD = 4
Q_CH = MXU_DIM // QUAD
CHUNKS_PER_STEP = 4
FINISH_QUADS = 8


def _cparams(*sem):
    return pltpu.CompilerParams(dimension_semantics=sem, vmem_limit_bytes=VMEM_LIMIT_BYTES)


def _single(block_shape, index_map):
    return pl.BlockSpec(block_shape, index_map, pipeline_mode=pl.Buffered(1))


def _norm_mod(x, g, shift, scale):
    ms = jnp.mean(x * x, axis=-1, keepdims=True)
    y = x * lax.rsqrt(ms + EPS) * g
    return y * (1.0 + scale) + shift


def _mod_kernel(cond_ref, w_ref, b_ref, o_ref):
    c = cond_ref[...]
    a = (c * jax.nn.sigmoid(c)).astype(BF16)
    o_ref[...] = jnp.dot(a, w_ref[...].astype(BF16), preferred_element_type=F32) + b_ref[...]


def _modulation(cond8, ada_w, ada_b):
    depth, d, n = ada_w.shape
    tn = n // 4
    return pl.pallas_call(
        _mod_kernel,
        grid=(depth, n // tn),
        in_specs=[
            pl.BlockSpec((SUBLANES, d), lambda i, j: (0, 0)),
            pl.BlockSpec((None, d, tn), lambda i, j: (i, 0, j)),
            pl.BlockSpec((None, 1, tn), lambda i, j: (i, 0, j)),
        ],
        out_specs=pl.BlockSpec((None, SUBLANES, tn), lambda i, j: (i, 0, j)),
        out_shape=jax.ShapeDtypeStruct((depth, SUBLANES, n), F32),
        compiler_params=_cparams("arbitrary", "arbitrary"),
        name="adaln_modulation",
    )(cond8, ada_w, ada_b.reshape(depth, 1, n))


def _qkv_kernel(x_ref, mod_ref, g_ref, w_ref, qg_ref, kg_ref, hm_ref, *rest, d, attn_scale, layer, with_cache):
    out_refs = rest[-5:] if with_cache else rest
    qp_ref, kp_ref, vp_ref = out_refs[:3]
    h = _norm_mod(x_ref[...], g_ref[...], mod_ref[0:1, :], mod_ref[1:2, :]).astype(BF16)
    nch = d // MXU_DIM

    def proj(c):
        return jnp.dot(h, w_ref[:, c * MXU_DIM:(c + 1) * MXU_DIM].astype(BF16), preferred_element_type=F32)

    def head_norm(first, gain_ref):
        parts = []
        for c in range(nch):
            tc = proj(first + c)
            ms = jnp.dot((tc * tc).astype(BF16), hm_ref[...], preferred_element_type=F32)
            parts.append(tc * lax.rsqrt(ms + EPS) * gain_ref[:, c * MXU_DIM:(c + 1) * MXU_DIM])
        return jnp.concatenate(parts, axis=-1)

    q = head_norm(0, qg_ref)
    k = head_norm(nch, kg_ref)
    v = jnp.concatenate([proj(2 * nch + c) for c in range(nch)], axis=-1)
    if with_cache:
        for ref, val in ((out_refs[3], k), (out_refs[4], v)):
            seq = ref.shape[-1]
            for s in range(ref.shape[0]):
                vt = val[s * seq:(s + 1) * seq, :].T
                if ref.ndim == 3:
                    ref[s] = vt
                else:
                    for l in range(ref.shape[1]):
                        ref[s, l] = vt if l == layer else jnp.zeros_like(vt)
    qs = (q * attn_scale).astype(BF16)
    kb = k.astype(BF16)
    vb = v.astype(BF16)
    for p in range(d // HEAD_PAIR):
        sl = slice(p * HEAD_PAIR, (p + 1) * HEAD_PAIR)
        qp_ref[p] = qs[:, sl]
        kp_ref[p] = kb[:, sl]
        vp_ref[p] = vb[:, sl]


def _qkv(x, mod, group_of_tile, g, w_qkv, layer, n_layers, q_gain, k_gain, head_mean, cache, *,
         tm, head_dim, seq, with_cache):
    n, d = x.shape
    npairs = d // HEAD_PAIR
    pair_spec = pl.BlockSpec((npairs, tm, HEAD_PAIR), lambda i: (0, i, 0))
    out_specs = [pair_spec] * 3
    out_shape = [jax.ShapeDtypeStruct((npairs, n, HEAD_PAIR), BF16)] * 3
    in_specs = [
        pl.BlockSpec((tm, d), lambda i: (i, 0)),
        pl.BlockSpec((None, N_MOD, d), lambda i: (group_of_tile(i), 0, 0)),
        pl.BlockSpec((1, d), lambda i: (0, 0)),
        _single((None, d, 3 * d), lambda i: (layer, 0, 0)),
        pl.BlockSpec((1, d), lambda i: (0, 0)),
        pl.BlockSpec((1, d), lambda i: (0, 0)),
        pl.BlockSpec((MXU_DIM, MXU_DIM), lambda i: (0, 0)),
    ]
    args = [x, mod, g, w_qkv, q_gain, k_gain, head_mean]
    aliases = {}
    if with_cache:
        out_shape += [jax.ShapeDtypeStruct((n // seq, n_layers, d, seq), F32)] * 2
        if cache is None:
            out_specs += [pl.BlockSpec((tm // seq, n_layers, d, seq), lambda i: (i, 0, 0, 0))] * 2
        else:
            out_specs += [pl.BlockSpec((tm // seq, None, d, seq), lambda i: (i, layer, 0, 0))] * 2
            aliases = {len(args): 3, len(args) + 1: 4}
            in_specs += [pl.BlockSpec(memory_space=pl.ANY)] * 2
            args += list(cache)
    return pl.pallas_call(
        functools.partial(_qkv_kernel, d=d, attn_scale=head_dim ** -0.5, layer=layer, with_cache=with_cache),
        grid=(n // tm,),
        in_specs=in_specs,
        out_specs=out_specs,
        out_shape=out_shape,
        input_output_aliases=aliases,
        compiler_params=_cparams("arbitrary"),
        name="norm_qkv",
    )(*args)


def _pair_masks():
    lane = lax.broadcasted_iota(jnp.int32, (1, HEAD_PAIR), 1)
    first = lane < HEAD_PAIR // 2
    return first, jnp.logical_not(first)


def _dot_nt(a, b):
    return lax.dot_general(a, b, (((1,), (1,)), ((), ())), preferred_element_type=F32)


def _ctx_attn_kernel(q_ref, k_ref, v_ref, x_ref, mod_ref, wo_ref, o_ref, att_ref, *, npairs):
    first, second = _pair_masks()
    for p in range(npairs):
        qp, kp, vp = q_ref[p], k_ref[p], v_ref[p]
        outs = []
        for msk in (first, second):
            qh = jnp.where(msk, qp, jnp.zeros_like(qp))
            s = _dot_nt(qh, kp)
            m = jnp.max(s, axis=-1, keepdims=True)
            e = jnp.exp(s - m)
            l = jnp.sum(e, axis=-1, keepdims=True)
            outs.append(jnp.dot(e.astype(BF16), vp, preferred_element_type=F32) / l)
        att_ref[:, p * HEAD_PAIR:(p + 1) * HEAD_PAIR] = jnp.where(first, outs[0], outs[1]).astype(BF16)
    o = jnp.dot(att_ref[...], wo_ref[...], preferred_element_type=F32)
    o_ref[...] = x_ref[...] + mod_ref[2:3, :] * o


def _ctx_attention(qp, kp, vp, x, mod, wo, layer, *, seq):
    n, d = x.shape
    npairs = d // HEAD_PAIR
    pair_spec = pl.BlockSpec((npairs, seq, HEAD_PAIR), lambda b: (0, b, 0))
    return pl.pallas_call(
        functools.partial(_ctx_attn_kernel, npairs=npairs),
        grid=(n // seq,),
        in_specs=[
            pair_spec, pair_spec, pair_spec,
            pl.BlockSpec((seq, d), lambda b: (b, 0)),
            pl.BlockSpec((None, N_MOD, d), lambda b: (0, 0, 0)),
            _single((None, d, d), lambda b: (layer, 0, 0)),
        ],
        out_specs=pl.BlockSpec((seq, d), lambda b: (b, 0)),
        out_shape=jax.ShapeDtypeStruct((n, d), F32),
        scratch_shapes=[pltpu.VMEM((seq, d), BF16)],
        compiler_params=_cparams("arbitrary"),
        name="context_attention",
    )(qp, kp, vp, x, mod, wo)


def _na_attn_kernel(q_ref, k_ref, v_ref, ck_ref, cv_ref, strip_ref, x_ref, mod_ref, wo_ref, o_ref,
                    bias_ref, att_ref, *, npairs, rows, q_tile):
    first, second = _pair_masks()
    kr = min(WIN_R, rows)
    n_loc = rows * GRID_W
    tile_rows = q_tile // GRID_W
    win_rows = min(kr + tile_rows, rows)

    bias_ref[...] = jnp.full(bias_ref.shape, -jnp.inf, F32)

    def pair_body(p, carry):
        for hh in range(2):
            for qr in range(rows):
                rs = min(max(qr - kr // 2, 0), rows - kr)
                off = rs - qr + (WIN_R - 1)
                bias_ref[hh, qr * GRID_W:(qr + 1) * GRID_W, rs * GRID_W:(rs + kr) * GRID_W] = (
                    strip_ref[2 * p + hh, :, off * GRID_W:(off + kr) * GRID_W])
        ckp, cvp = ck_ref[p], cv_ref[p]
        for qt in range(n_loc // q_tile):
            rsl = slice(qt * q_tile, (qt + 1) * q_tile)
            ks = min(max(qt * tile_rows - kr // 2, 0), rows - win_rows)
            ksl = slice(ks * GRID_W, (ks + win_rows) * GRID_W)
            qp, kp, vp = q_ref[p, rsl, :], k_ref[p, ksl, :], v_ref[p, ksl, :]
            outs = []
            for hh, msk in enumerate((first, second)):
                qh = jnp.where(msk, qp, jnp.zeros_like(qp))
                s_loc = _dot_nt(qh, kp) + bias_ref[hh, rsl, ksl]
                s_ctx = _dot_nt(qh, ckp)
                m = jnp.maximum(jnp.max(s_loc, axis=-1, keepdims=True), jnp.max(s_ctx, axis=-1, keepdims=True))
                e_loc = jnp.exp(s_loc - m)
                e_ctx = jnp.exp(s_ctx - m)
                l = jnp.sum(e_loc, axis=-1, keepdims=True) + jnp.sum(e_ctx, axis=-1, keepdims=True)
                o = (jnp.dot(e_loc.astype(BF16), vp, preferred_element_type=F32)
                     + jnp.dot(e_ctx.astype(BF16), cvp, preferred_element_type=F32))
                outs.append(o / l)
            att_ref[p, rsl, :] = jnp.where(first, outs[0], outs[1]).astype(BF16)
        return carry

    lax.fori_loop(0, npairs, pair_body, 0)
    att = jnp.concatenate([att_ref[p] for p in range(npairs)], axis=-1)
    o = jnp.dot(att, wo_ref[...], preferred_element_type=F32)
    o_ref[...] = x_ref[...] + mod_ref[2:3, :] * o


def _na_attention(qp, kp, vp, ckp, cvp, strip, x, mod, wo, layer, *, n_batch):
    n, d = x.shape
    npairs = d // HEAD_PAIR
    n_loc = n // n_batch
    past = ckp.shape[3]
    rows = n_loc // GRID_W
    pair_spec = _single((npairs, n_loc, HEAD_PAIR), lambda b: (0, b, 0))
    ctx_spec = _single((None, None, npairs, past, HEAD_PAIR), lambda b: (layer, b, 0, 0, 0))
    return pl.pallas_call(
        functools.partial(_na_attn_kernel, npairs=npairs, rows=rows, q_tile=256),
        grid=(n_batch,),
        in_specs=[
            pair_spec, pair_spec, pair_spec, ctx_spec, ctx_spec,
            _single((None,) + strip.shape[1:], lambda b: (layer, 0, 0, 0)),
            _single((n_loc, d), lambda b: (b, 0)),
            pl.BlockSpec((None, N_MOD, d), lambda b: (1 + b, 0, 0)),
            _single((None, d, d), lambda b: (layer, 0, 0)),
        ],
        out_specs=_single((n_loc, d), lambda b: (b, 0)),
        out_shape=jax.ShapeDtypeStruct((n, d), F32),
        scratch_shapes=[pltpu.VMEM((2, n_loc, n_loc), F32), pltpu.VMEM((npairs, n_loc, HEAD_PAIR), BF16)],
        compiler_params=_cparams("arbitrary"),
        name="neighbourhood_attention",
    )(qp, kp, vp, ckp, cvp, strip, x, mod, wo)


def _rpb_strip(rpb):
    qc = np.arange(GRID_W)[:, None]
    kc = np.arange(GRID_W)[None, :]
    col_start = np.clip(qc - WIN_C // 2, 0, GRID_W - WIN_C)
    in_win = (kc >= col_start) & (kc < col_start + WIN_C)
    dc = np.clip(kc - qc, -(WIN_C - 1), WIN_C - 1) + (WIN_C - 1)
    sel = (dc[:, :, None] == np.arange(2 * WIN_C - 1)).astype(np.float32)
    t = jnp.einsum('lhdj,qkj->lhqdk', rpb.astype(F32), sel, precision=lax.Precision.HIGHEST)
    t = jnp.where(in_win[None, None, :, None, :], t, -jnp.inf)
    n_layers, h, ndr = rpb.shape[:3]
    return t.reshape(n_layers, h, GRID_W, ndr * GRID_W)


def _ffn_kernel(xa_ref, xb_ref, mod_ref, g_ref, w1_ref, w3_ref, w2_ref, oa_ref, ob_ref, t_ref, *, tiles_a):
    def tile(x_ref, o_ref):
        x = x_ref[...]
        h = _norm_mod(x, g_ref[...], mod_ref[3:4, :], mod_ref[4:5, :]).astype(BF16)
        for c in range(w1_ref.shape[1] // MXU_DIM):
            sl = slice(c * MXU_DIM, (c + 1) * MXU_DIM)
            a = jnp.dot(h, w1_ref[:, sl].astype(BF16), preferred_element_type=F32)
            b = jnp.dot(h, w3_ref[:, sl].astype(BF16), preferred_element_type=F32)
            t_ref[:, sl] = (a * jax.nn.sigmoid(a) * b).astype(BF16)
        t = t_ref[...]
        for c in range(x.shape[1] // MXU_DIM):
            sl = slice(c * MXU_DIM, (c + 1) * MXU_DIM)
            y = jnp.dot(t, w2_ref[:, sl].astype(BF16), preferred_element_type=F32)
            o_ref[:, sl] = x[:, sl] + mod_ref[5:6, sl] * y

    first_stream = pl.program_id(0) < tiles_a
    pl.when(first_stream)(functools.partial(tile, xa_ref, oa_ref))
    pl.when(jnp.logical_not(first_stream))(functools.partial(tile, xb_ref, ob_ref))


def _ffn(xa, xb, mod, group_of_tile_b, g, w1, w3, w2, layer, *, tm):
    d = xa.shape[1]
    d_ff = w1.shape[-1]
    tiles_a, tiles_b = xa.shape[0] // tm, xb.shape[0] // tm
    a_map = lambda i: (jnp.minimum(i, tiles_a - 1), 0)
    b_map = lambda i: (jnp.maximum(i - tiles_a, 0), 0)
    group = lambda i: jnp.where(i < tiles_a, 0, group_of_tile_b(jnp.maximum(i - tiles_a, 0)))
    return pl.pallas_call(
        functools.partial(_ffn_kernel, tiles_a=tiles_a),
        grid=(tiles_a + tiles_b,),
        in_specs=[
            pl.BlockSpec((tm, d), a_map),
            _single((tm, d), b_map),
            pl.BlockSpec((None, N_MOD, d), lambda i: (group(i), 0, 0)),
            pl.BlockSpec((1, d), lambda i: (0, 0)),
            _single((None, d, d_ff), lambda i: (layer, 0, 0)),
            _single((None, d, d_ff), lambda i: (layer, 0, 0)),
            _single((None, d_ff, d), lambda i: (layer, 0, 0)),
        ],
        out_specs=[pl.BlockSpec((tm, d), a_map), _single((tm, d), b_map)],
        out_shape=[jax.ShapeDtypeStruct(xa.shape, F32), jax.ShapeDtypeStruct(xb.shape, F32)],
        scratch_shapes=[pltpu.VMEM((tm, d_ff), BF16)],
        compiler_params=_cparams("arbitrary"),
        name="swiglu_ffn",
    )(xa, xb, mod, g, w1, w3, w2)


def _lane_half_swap(x):
    n = x.shape[-1]
    lane = lax.broadcasted_iota(jnp.int32, (1, n), 1)
    return jnp.where(lane % LANES < LANES // 2, pltpu.roll(x, n - LANES // 2, axis=1),
                     pltpu.roll(x, LANES // 2, axis=1))


def _quad_pairs(t0, t1, t2, t3):
    lo = lax.broadcasted_iota(jnp.int32, (1, t0.shape[-1]), 1) % LANES < LANES // 2
    s0, s1, s2, s3 = (_lane_half_swap(t) for t in (t0, t1, t2, t3))
    even = (jnp.where(lo, t0, s1), jnp.where(lo, t2, s3))
    odd = (jnp.where(lo, s0, t1), jnp.where(lo, s2, t3))
    return even, odd


def _ssm_kernel(x_ref, mod_ref, g_ref, dskip_ref, win_ref, wcar_ref, kin_ref, a4_ref, h0_ref, wglu_ref, *rest,
                seq, n_seg, with_state):
    if with_state:
        o_ref, sre_ref, sim_ref = rest[:3]
    else:
        o_ref = rest[0]
    u_ref, y_ref, z_ref, sp_ref, f_ref, e_ref = rest[-6:]
    d = x_ref.shape[2]
    cps = win_ref.shape[0]
    nsq = a4_ref.shape[-1] // 2
    nst = cps * nsq
    nq = seq // QUAD
    steps_per_dir = (d // Q_CH) // cps
    tb = 4 * QUAD
    qb = tb // QUAD * SUBLANES
    dk = pl.program_id(1)
    kb = (dk % steps_per_dir) * cps
    fwd = dk // steps_per_dir == 0

    @pl.when(dk == 0)
    def _prepare():
        gain, shift, scale = g_ref[...], mod_ref[0], mod_ref[1]

        def body(i, carry):
            t0 = pl.multiple_of(i * tb, tb)
            xt = jnp.swapaxes(x_ref[:, pl.ds(t0, tb), :], 0, 1)
            h = _norm_mod(xt, gain, shift, scale).reshape(tb // QUAD, QUAD, SUBLANES, d)
            rows = pl.ds(pl.multiple_of(i * qb, qb), qb)
            halves = _quad_pairs(*(h[:, s].reshape(qb, d) for s in range(QUAD)))
            for m in range(d // LANES):
                for parity, (v01, v23) in enumerate(halves):
                    k = 2 * m + parity
                    sl = slice(m * LANES, (m + 1) * LANES)
                    u_ref[k, rows, :LANES] = v01[:, sl].astype(BF16)
                    u_ref[k, rows, LANES:] = v23[:, sl].astype(BF16)
                    y_ref[k, rows, :LANES] = v01[:, sl] * dskip_ref[k, :, :LANES]
                    y_ref[k, rows, LANES:] = v23[:, sl] * dskip_ref[k, :, LANES:]
            return carry

        lax.fori_loop(0, seq // tb, body, 0)

    a_re = jnp.broadcast_to(jnp.concatenate([a4_ref[c][:, :nsq] for c in range(cps)], axis=-1), (SUBLANES, nst))
    a_im = jnp.broadcast_to(jnp.concatenate([a4_ref[c][:, nsq:] for c in range(cps)], axis=-1), (SUBLANES, nst))

    def chunk_cols(ref, c):
        return jnp.concatenate([ref[:, c * nsq:(c + 1) * nsq], ref[:, nst + c * nsq:nst + (c + 1) * nsq]], axis=-1)

    def scan(state, store, reverse):
        s_re, s_im = state
        pairs = range(nq // 2)
        for m in (reversed(pairs) if reverse else pairs):
            entering = {}
            for j in ((2 * m + 1, 2 * m) if reverse else (2 * m, 2 * m + 1)):
                rsl = slice(j * SUBLANES, (j + 1) * SUBLANES)
                entering[j] = (s_re, s_im)
                s_re, s_im = (a_re * s_re - a_im * s_im + z_ref[rsl, :nst],
                              a_re * s_im + a_im * s_re + z_ref[rsl, nst:])
            if store:
                psl = slice(2 * m * SUBLANES, (2 * m + 2) * SUBLANES)
                lo, hi = entering[2 * m], entering[2 * m + 1]
                sp_ref[psl, :nst] = jnp.concatenate([lo[0], hi[0]], axis=0).astype(BF16)
                sp_ref[psl, nst:] = jnp.concatenate([lo[1], hi[1]], axis=0).astype(BF16)
        return s_re, s_im

    def run_direction(reverse):
        for c in range(cps):
            z = jnp.dot(u_ref[kb + c], win_ref[c], preferred_element_type=F32)
            z_ref[:, c * nsq:(c + 1) * nsq] = z[:, :nsq]
            z_ref[:, nst + c * nsq:nst + (c + 1) * nsq] = z[:, nsq:]
        start = (jnp.concatenate([h0_ref[c][:, :nsq] for c in range(cps)], axis=-1),
                 jnp.concatenate([h0_ref[c][:, nsq:] for c in range(cps)], axis=-1))
        if n_seg > 1:
            zero = jnp.zeros((SUBLANES, nst), F32)
            e_re, e_im = scan((zero, zero), False, reverse)
            e_ref[:, :nst] = e_re
            e_ref[:, nst:] = e_im
            f_ref[:, :nst] = start[0]
            f_ref[:, nst:] = start[1]
            p_re, p_im = a_re[:1], a_im[:1]
            for _ in range(int(math.log2(nq))):
                p_re, p_im = p_re * p_re - p_im * p_im, 2.0 * p_re * p_im
            for b in range(SUBLANES // n_seg):
                for s in (range(n_seg - 2, -1, -1) if reverse else range(1, n_seg)):
                    dst = b * n_seg + s
                    src = dst + 1 if reverse else dst - 1
                    f_re, f_im = f_ref[src:src + 1, :nst], f_ref[src:src + 1, nst:]
                    f_ref[dst:dst + 1, :nst] = p_re * f_re - p_im * f_im + e_ref[src:src + 1, :nst]
                    f_ref[dst:dst + 1, nst:] = p_re * f_im + p_im * f_re + e_ref[src:src + 1, nst:]
            start = (f_ref[:, :nst], f_ref[:, nst:])
        fin_re, fin_im = scan(start, True, reverse)
        if with_state:
            for c in range(cps):
                sre_ref[c] = fin_re[:, c * nsq:(c + 1) * nsq]
                sim_ref[c] = fin_im[:, c * nsq:(c + 1) * nsq]
        for c in range(cps):
            y_ref[kb + c] = (y_ref[kb + c]
                             + _dot_nt(chunk_cols(sp_ref, c), wcar_ref[c])
                             + jnp.dot(u_ref[kb + c], kin_ref[c], preferred_element_type=F32))

    pl.when(fwd)(functools.partial(run_direction, False))
    pl.when(jnp.logical_not(fwd))(functools.partial(run_direction, True))

    @pl.when(dk == 2 * steps_per_dir - 1)
    def _finish():
        gate = mod_ref[2]
        fq = FINISH_QUADS * SUBLANES
        ft = FINISH_QUADS * QUAD

        def body(i, carry):
            rows = pl.ds(pl.multiple_of(i * fq, fq), fq)
            gathered = [jnp.concatenate([y_ref[2 * m + parity, rows, half] for m in range(d // LANES)], axis=-1)
                        for half in (slice(0, LANES), slice(LANES, 2 * LANES)) for parity in (0, 1)]
            (y0, y2), (y1, y3) = _quad_pairs(*gathered)
            y = jnp.stack([t.reshape(FINISH_QUADS, SUBLANES, d) for t in (y0, y1, y2, y3)], axis=1)
            z = jax.nn.gelu(y.reshape(ft * SUBLANES, d)).astype(BF16)
            gl = jnp.dot(z, wglu_ref[...], preferred_element_type=F32)
            out = (gl[:, :d] * jax.nn.sigmoid(gl[:, d:])).reshape(ft, SUBLANES, d) * gate
            for j in range(ft // tb):
                t0 = pl.multiple_of(i * ft + j * tb, tb)
                upd = jnp.swapaxes(out[j * tb:(j + 1) * tb], 0, 1)
                o_ref[:, pl.ds(t0, tb), :] = x_ref[:, pl.ds(t0, tb), :] + upd
            return carry

        lax.fori_loop(0, nq // FINISH_QUADS, body, 0)


def _ssm_mixer(x, mod8, g, dskip, w_in, w_car, k_in, a4, h0, wglu, layer, *, seq, n_seg, with_state):
    n_seq, _, d = x.shape
    n_groups = n_seq // SUBLANES
    n_chunks = d // Q_CH
    nsq = a4.shape[-1] // 2
    cps = CHUNKS_PER_STEP
    steps_per_dir = n_chunks // cps
    rows_q = seq // QUAD * SUBLANES
    dk_map = lambda gi, dk: (layer, dk // steps_per_dir, dk % steps_per_dir, 0, 0)
    seq_map = lambda gi, dk: (dk // steps_per_dir, dk % steps_per_dir, gi, 0)
    x_spec = _single((SUBLANES, seq, d), lambda gi, dk: (gi, 0, 0))
    out_specs = [x_spec]
    out_shape = [jax.ShapeDtypeStruct(x.shape, F32)]
    if with_state:
        st_spec = pl.BlockSpec((None, cps, SUBLANES, nsq), seq_map)
        out_specs += [st_spec, st_spec]
        out_shape += [jax.ShapeDtypeStruct((2, n_chunks, n_seq, nsq), F32)] * 2
    return pl.pallas_call(
        functools.partial(_ssm_kernel, seq=seq, n_seg=n_seg, with_state=with_state),
        grid=(n_groups, 2 * steps_per_dir),
        in_specs=[
            pl.BlockSpec((SUBLANES, seq, d), lambda gi, dk: (gi, 0, 0)),
            pl.BlockSpec((N_MOD, SUBLANES, d), lambda gi, dk: (0, 0, 0)),
            pl.BlockSpec((1, d), lambda gi, dk: (0, 0)),
            pl.BlockSpec((n_chunks, 1, MXU_DIM), lambda gi, dk: (0, 0, 0)),
            pl.BlockSpec((None, None, cps) + w_in.shape[-2:], dk_map),
            pl.BlockSpec((None, None, cps) + w_car.shape[-2:], dk_map),
            pl.BlockSpec((None, None, cps, MXU_DIM, MXU_DIM), dk_map),
            pl.BlockSpec((None, None, cps, 1, 2 * nsq), dk_map),
            pl.BlockSpec((None, cps, SUBLANES, 2 * nsq), seq_map),
            _single((None, d, 2 * d), lambda gi, dk: (layer, 0, 0)),
        ],
        out_specs=out_specs,
        out_shape=out_shape,
        scratch_shapes=[
            pltpu.VMEM((n_chunks, rows_q, MXU_DIM), BF16),
            pltpu.VMEM((n_chunks, rows_q, MXU_DIM), F32),
            pltpu.VMEM((rows_q, 2 * cps * nsq), F32),
            pltpu.VMEM((rows_q, 2 * cps * nsq), BF16),
            pltpu.VMEM((SUBLANES, 2 * cps * nsq), F32),
            pltpu.VMEM((SUBLANES, 2 * cps * nsq), F32),
        ],
        compiler_params=_cparams("arbitrary", "arbitrary"),
        name="s5_mixer",
    )(x, mod8, g, dskip, w_in, w_car, k_in, a4, h0, wglu)


def _ssm_params(lam_re, lam_im, log_step, b_re, b_im, c_re, c_im):
    lam_re, lam_im = lam_re.astype(F32), lam_im.astype(F32)
    step = jnp.exp(log_step.astype(F32))[..., None]
    mag = jnp.exp(lam_re * step)
    a_re, a_im = mag * jnp.cos(lam_im * step), mag * jnp.sin(lam_im * step)
    den = lam_re * lam_re + lam_im * lam_im
    nr, ni = a_re - 1.0, a_im
    f_re = (nr * lam_re + ni * lam_im) / den
    f_im = (ni * lam_re - nr * lam_im) / den
    bt_re, bt_im = jnp.swapaxes(b_re.astype(F32), -1, -2), jnp.swapaxes(b_im.astype(F32), -1, -2)
    bb = (f_re[..., None, :] * bt_re - f_im[..., None, :] * bt_im,
          f_re[..., None, :] * bt_im + f_im[..., None, :] * bt_re)
    cc = (c_re.astype(F32), c_im.astype(F32))

    def cmul(x, y):
        return x[0] * y[0] - x[1] * y[1], x[0] * y[1] + x[1] * y[0]

    pw = [(jnp.ones_like(a_re), jnp.zeros_like(a_re))]
    for _ in range(QUAD):
        pw.append(cmul(pw[-1], (a_re, a_im)))

    def by_dir(n_fwd, n_bwd):
        return tuple(jnp.stack([pw[n_fwd][i][:, 0], pw[n_bwd][i][:, 1]], axis=1)[..., None, :] for i in range(2))

    n_layers, _, n_groups, c, p = bb[0].shape
    gq = Q_CH // c
    kk = n_groups // gq
    nsq = gq * p
    lead = (n_layers, 2, kk)

    def pack(ws, sign):
        def rows(i):
            t = jnp.stack([w[i] for w in ws], axis=2).reshape(n_layers, 2, QUAD, kk, gq, c, p)
            return t.transpose(0, 1, 3, 2, 4, 5, 6).reshape(*lead, MXU_DIM, p)
        return jnp.concatenate([rows(0), sign * rows(1)], axis=-1).astype(BF16)

    w_in = pack([cmul(by_dir(QUAD - 1 - s, s), bb) for s in range(QUAD)], 1.0)
    w_car = pack([cmul(cc, by_dir(s + 1, QUAD - s)) for s in range(QUAD)], -1.0)

    gc = gq * c
    same_g = (np.arange(gc) // c)[:, None] == (np.arange(gc) // c)[None, :]
    cb = [cmul(cc, tuple(t[..., None, :] for t in pw[n])) for n in range(QUAD)]
    cb = jnp.concatenate([jnp.concatenate([t[0] for t in cb], axis=-2),
                          -jnp.concatenate([t[1] for t in cb], axis=-2)], axis=-1)
    taps = jnp.einsum('ldgkp,ldgcp->ldgck', cb, jnp.concatenate(bb, axis=-1), precision=lax.Precision.HIGHEST)
    tiles = []
    for n in range(QUAD):
        tap = taps[..., n * c:(n + 1) * c]
        tiles.append(jnp.where(same_g, jnp.tile(tap.reshape(*lead, gc, c), (1, 1, 1, 1, gq)), 0.0))
    zero = jnp.zeros_like(tiles[0][:, 0])

    def tile_of(d, s, t):
        n = t - s if d == 0 else s - t
        return tiles[n][:, d] if n >= 0 else zero

    k_in = jnp.stack([jnp.concatenate([jnp.concatenate([tile_of(d, s, t) for t in range(QUAD)], axis=-1)
                                       for s in range(QUAD)], axis=-2) for d in range(2)], axis=1)

    a4 = jnp.concatenate([pw[QUAD][0].reshape(*lead, 1, nsq), pw[QUAD][1].reshape(*lead, 1, nsq)], axis=-1)
    return w_in.astype(BF16), w_car.astype(BF16), k_in.astype(BF16), a4


def _block_diag_kernel(win_ref, wcar_ref, win_o_ref, wcar_o_ref, *, gq):
    pp = win_ref.shape[-1] // 2
    nsq = gq * pp
    src = lax.broadcasted_iota(jnp.int32, (2 * pp, 2 * nsq), 0)
    col = lax.broadcasted_iota(jnp.int32, (2 * pp, 2 * nsq), 1)
    spread = ((col % pp == src % pp) & ((col >= nsq) == (src >= pp))).astype(BF16)
    row_g = (lax.broadcasted_iota(jnp.int32, (MXU_DIM, 2 * nsq), 0) // (Q_CH // gq)) % gq
    col_g = (lax.broadcasted_iota(jnp.int32, (MXU_DIM, 2 * nsq), 1) % nsq) // pp
    for src_ref, dst_ref in ((win_ref, win_o_ref), (wcar_ref, wcar_o_ref)):
        for c in range(src_ref.shape[0]):
            w = jnp.dot(src_ref[c], spread, preferred_element_type=F32)
            dst_ref[c] = jnp.where(row_g == col_g, w, 0.0).astype(BF16)


def _block_diag(w_in, w_car, *, gq):
    n_layers, n_dir, kk, rows, two_p = w_in.shape
    cps = CHUNKS_PER_STEP
    spec_in = pl.BlockSpec((None, None, cps, rows, two_p), lambda l, dr, k: (l, dr, k, 0, 0))
    spec_out = pl.BlockSpec((None, None, cps, rows, gq * two_p), lambda l, dr, k: (l, dr, k, 0, 0))
    out = jax.ShapeDtypeStruct((n_layers, n_dir, kk, rows, gq * two_p), BF16)
    return pl.pallas_call(
        functools.partial(_block_diag_kernel, gq=gq),
        grid=(n_layers, n_dir, kk // cps),
        in_specs=[spec_in, spec_in],
        out_specs=[spec_out, spec_out],
        out_shape=[out, out],
        compiler_params=_cparams("arbitrary", "arbitrary", "arbitrary"),
        name="s5_block_diag",
    )(w_in, w_car)


def kernel(x_prompt, x_sample, cache_k, cache_v, state_ssm_re, state_ssm_im, c, c_ctx, norm_mix, norm_ffn, ada_w, ada_b, na_w_qkv, na_w_o, na_q_gain, na_k_gain, na_rpb, ssm_lambda_re, ssm_lambda_im, ssm_log_step, ssm_b_re, ssm_b_im, ssm_c_re, ssm_c_im, ssm_d, ssm_w_glu, ffn_w1, ffn_w3, ffn_w2):
    batch, seq, d = x_prompt.shape
    dec_batch, dec_seq, _ = x_sample.shape
    depth = ada_w.shape[0]
    heads, head_dim = cache_k.shape[3], cache_k.shape[4]
    past = cache_k.shape[2]
    npairs = d // HEAD_PAIR
    d_ff = ffn_w1.shape[-1]
    n_dir = state_ssm_re.shape[2]
    n_seg = dec_seq // seq
    assert seq % (FINISH_QUADS * QUAD) == 0 and dec_seq % seq == 0 and SUBLANES % n_seg == 0
    assert dec_batch * n_seg == SUBLANES and batch % SUBLANES == 0 and seq & (seq - 1) == 0
    assert d_ff % MXU_DIM == 0 and d % (Q_CH * CHUNKS_PER_STEP) == 0 and head_dim * 2 == HEAD_PAIR
    assert Q_CH * 2 == LANES and Q_CH % SSM_GROUP == 0

    cond8 = jnp.zeros((SUBLANES, d), F32).at[0].set(c_ctx).at[1:1 + dec_batch].set(c)
    mod = _modulation(cond8, ada_w, ada_b).reshape(depth, SUBLANES, N_MOD, d)

    tm = math.gcd(512, dec_seq)
    prompt_group = lambda i: 0
    sample_group = lambda i: 1 + (i * tm) // dec_seq

    head_mean = jnp.kron(jnp.eye(MXU_DIM // head_dim, dtype=F32),
                         jnp.full((head_dim, head_dim), 1.0 / head_dim, F32)).astype(BF16)

    w_qkv, w_o, w_glu = na_w_qkv, na_w_o.astype(BF16), ssm_w_glu.astype(BF16)
    w1, w3, w2 = ffn_w1, ffn_w3, ffn_w2
    n_na = cache_k.shape[1]
    ckp = cache_k.astype(BF16).reshape(dec_batch, n_na, past, npairs, HEAD_PAIR).transpose(1, 0, 3, 2, 4)
    cvp = cache_v.astype(BF16).reshape(dec_batch, n_na, past, npairs, HEAD_PAIR).transpose(1, 0, 3, 2, 4)
    strip = _rpb_strip(na_rpb)
    w_in, w_car, k_in, a4 = _ssm_params(ssm_lambda_re, ssm_lambda_im, ssm_log_step, ssm_b_re, ssm_b_im,
                                        ssm_c_re, ssm_c_im)
    s5_w = (*_block_diag(w_in, w_car, gq=Q_CH // SSM_GROUP), k_in, a4)

    xp = x_prompt.reshape(batch * seq, d)
    xs = x_sample.reshape(dec_batch * dec_seq, d)
    new_cache, new_sre, new_sim = None, [], []
    for i in range(depth):
        j = i // 2
        g_mix = norm_mix[i].reshape(1, d)
        if i % 2 == 0:
            q_gain = jnp.tile(na_q_gain[j], heads).reshape(1, d)
            k_gain = jnp.tile(na_k_gain[j], heads).reshape(1, d)
            qp, kp, vp, *new_cache = _qkv(xp, mod[i], prompt_group, g_mix, w_qkv, j, n_na, q_gain, k_gain,
                                          head_mean, new_cache, tm=tm, head_dim=head_dim, seq=seq, with_cache=True)
            xp = _ctx_attention(qp, kp, vp, xp, mod[i], w_o, j, seq=seq)
            qs, ks, vs = _qkv(xs, mod[i], sample_group, g_mix, w_qkv, j, n_na, q_gain, k_gain, head_mean, None,
                              tm=tm, head_dim=head_dim, seq=seq, with_cache=False)
            xs = _na_attention(qs, ks, vs, ckp, cvp, strip, xs, mod[i], w_o, j, n_batch=dec_batch)
        else:
            n_chunks = d // Q_CH
            dskip = jnp.tile(ssm_d[j].astype(F32).reshape(n_chunks, 1, Q_CH), (1, 1, QUAD))
            ns = a4.shape[-1] // 2
            mod_p = jnp.broadcast_to(mod[i, 0][:, None, :], (N_MOD, SUBLANES, d))
            h0_p = jnp.zeros((n_dir, n_chunks, batch, 2 * ns), F32)
            xp, sre, sim = _ssm_mixer(xp.reshape(batch, seq, d), mod_p, g_mix, dskip, *s5_w, h0_p, w_glu, j,
                                      seq=seq, n_seg=1, with_state=True)
            xp = xp.reshape(batch * seq, d)
            new_sre.append(sre.transpose(2, 0, 1, 3).reshape(batch, n_dir, d // SSM_GROUP, SSM_STATE))
            new_sim.append(sim.transpose(2, 0, 1, 3).reshape(batch, n_dir, d // SSM_GROUP, SSM_STATE))
            mod_s = jnp.repeat(mod[i, 1:1 + dec_batch], n_seg, axis=0).transpose(1, 0, 2)
            s_re = state_ssm_re[:, j].astype(F32).reshape(dec_batch, n_dir, n_chunks, ns)
            s_im = state_ssm_im[:, j].astype(F32).reshape(dec_batch, n_dir, n_chunks, ns)
            s0 = jnp.concatenate([s_re, s_im], axis=-1)
            h0_s = jnp.zeros((dec_batch, n_seg, n_dir, n_chunks, 2 * ns), F32)
            h0_s = h0_s.at[:, 0, 0].set(s0[:, 0]).at[:, n_seg - 1, 1].set(s0[:, 1])
            h0_s = h0_s.reshape(dec_batch * n_seg, n_dir, n_chunks, 2 * ns).transpose(1, 2, 0, 3)
            (xs,) = _ssm_mixer(xs.reshape(dec_batch * n_seg, seq, d), mod_s, g_mix, dskip, *s5_w, h0_s, w_glu, j,
                               seq=seq, n_seg=n_seg, with_state=False)
            xs = xs.reshape(dec_batch * dec_seq, d)
        g_ffn = norm_ffn[i].reshape(1, d)
        xp, xs = _ffn(xp, xs, mod[i], sample_group, g_ffn, w1, w3, w2, i, tm=tm)
    new_k, new_v = (t.reshape(batch, n_na, heads, head_dim, seq).transpose(0, 1, 4, 2, 3) for t in new_cache)
    return (xp.reshape(batch, seq, d), xs.reshape(dec_batch, dec_seq, d), new_k, new_v,
            jnp.stack(new_sre, axis=1), jnp.stack(new_sim, axis=1))
```

```python
import functools
import math

import jax
import jax.numpy as jnp
import numpy as np
from jax import lax
from jax.experimental import pallas as pl
from jax.experimental.pallas import tpu as pltpu

F32 = jnp.float32
BF16 = jnp.bfloat16

EPS = 1e-6
N_MOD = 6
GRID_W = 64
WIN_R = 8
WIN_C = 16
SSM_GROUP = 16
SSM_STATE = 64

SUBLANES = 8
LANES = 128
MXU_DIM = 256
VMEM_LIMIT_BYTES = 56 * 1024 * 1024

HEAD_PAIR = LANES
QUAD = 4
Q_CH = MXU_DIM // QUAD
CHUNKS_PER_STEP = 4
FINISH_QUADS = 8


def _cparams(*sem):
    return pltpu.CompilerParams(dimension_semantics=sem, vmem_limit_bytes=VMEM_LIMIT_BYTES)


def _single(block_shape, index_map):
    return pl.BlockSpec(block_shape, index_map, pipeline_mode=pl.Buffered(1))


def _norm_mod(x, g, shift, scale):
    ms = jnp.mean(x * x, axis=-1, keepdims=True)
    y = x * lax.rsqrt(ms + EPS) * g
    return y * (1.0 + scale) + shift


def _mod_kernel(cond_ref, w_ref, b_ref, o_ref):
    c = cond_ref[...]
    a = (c * jax.nn.sigmoid(c)).astype(BF16)
    o_ref[...] = jnp.dot(a, w_ref[...].astype(BF16), preferred_element_type=F32) + b_ref[...]


def _modulation(cond8, ada_w, ada_b):
    depth, d, n = ada_w.shape
    tn = n // 4
    return pl.pallas_call(
        _mod_kernel,
        grid=(depth, n // tn),
        in_specs=[
            pl.BlockSpec((SUBLANES, d), lambda i, j: (0, 0)),
            pl.BlockSpec((None, d, tn), lambda i, j: (i, 0, j)),
            pl.BlockSpec((None, 1, tn), lambda i, j: (i, 0, j)),
        ],
        out_specs=pl.BlockSpec((None, SUBLANES, tn), lambda i, j: (i, 0, j)),
        out_shape=jax.ShapeDtypeStruct((depth, SUBLANES, n), F32),
        compiler_params=_cparams("arbitrary", "arbitrary"),
        name="adaln_modulation",
    )(cond8, ada_w, ada_b.reshape(depth, 1, n))


def _qkv_kernel(x_ref, mod_ref, g_ref, w_ref, qg_ref, kg_ref, hm_ref, *rest, d, attn_scale, layer, with_cache):
    out_refs = rest[-5:] if with_cache else rest
    qp_ref, kp_ref, vp_ref = out_refs[:3]
    h = _norm_mod(x_ref[...], g_ref[...], mod_ref[0:1, :], mod_ref[1:2, :]).astype(BF16)
    qkv = jnp.dot(h, w_ref[...], preferred_element_type=F32)
    q, k, v = qkv[:, :d], qkv[:, d:2 * d], qkv[:, 2 * d:]

    def head_norm(t, gain):
        parts = []
        for c in range(d // MXU_DIM):
            tc = t[:, c * MXU_DIM:(c + 1) * MXU_DIM]
            ms = jnp.dot((tc * tc).astype(BF16), hm_ref[...], preferred_element_type=F32)
            parts.append(tc * lax.rsqrt(ms + EPS))
        return jnp.concatenate(parts, axis=-1) * gain

    q = head_norm(q, qg_ref[...])
    k = head_norm(k, kg_ref[...])
    if with_cache:
        for ref, val in ((out_refs[3], k), (out_refs[4], v)):
            seq = ref.shape[-1]
            for s in range(ref.shape[0]):
                vt = val[s * seq:(s + 1) * seq, :].T
                if ref.ndim == 3:
                    ref[s] = vt
                else:
                    for l in range(ref.shape[1]):
                        ref[s, l] = vt if l == layer else jnp.zeros_like(vt)
    qs = (q * attn_scale).astype(BF16)
    kb = k.astype(BF16)
    vb = v.astype(BF16)
    for p in range(d // HEAD_PAIR):
        sl = slice(p * HEAD_PAIR, (p + 1) * HEAD_PAIR)
        qp_ref[p] = qs[:, sl]
        kp_ref[p] = kb[:, sl]
        vp_ref[p] = vb[:, sl]


def _qkv(x, mod, group_of_tile, g, w_qkv, layer, n_layers, q_gain, k_gain, head_mean, cache, *,
         tm, head_dim, seq, with_cache):
    n, d = x.shape
    npairs = d // HEAD_PAIR
    pair_spec = pl.BlockSpec((npairs, tm, HEAD_PAIR), lambda i: (0, i, 0))
    out_specs = [pair_spec] * 3
    out_shape = [jax.ShapeDtypeStruct((npairs, n, HEAD_PAIR), BF16)] * 3
    in_specs = [
        pl.BlockSpec((tm, d), lambda i: (i, 0)),
        pl.BlockSpec((None, N_MOD, d), lambda i: (group_of_tile(i), 0, 0)),
        pl.BlockSpec((1, d), lambda i: (0, 0)),
        _single((None, d, 3 * d), lambda i: (layer, 0, 0)),
        pl.BlockSpec((1, d), lambda i: (0, 0)),
        pl.BlockSpec((1, d), lambda i: (0, 0)),
        pl.BlockSpec((MXU_DIM, MXU_DIM), lambda i: (0, 0)),
    ]
    args = [x, mod, g, w_qkv, q_gain, k_gain, head_mean]
    aliases = {}
    if with_cache:
        out_shape += [jax.ShapeDtypeStruct((n // seq, n_layers, d, seq), F32)] * 2
        if cache is None:
            out_specs += [pl.BlockSpec((tm // seq, n_layers, d, seq), lambda i: (i, 0, 0, 0))] * 2
        else:
            out_specs += [pl.BlockSpec((tm // seq, None, d, seq), lambda i: (i, layer, 0, 0))] * 2
            aliases = {len(args): 3, len(args) + 1: 4}
            in_specs += [pl.BlockSpec(memory_space=pl.ANY)] * 2
            args += list(cache)
    return pl.pallas_call(
        functools.partial(_qkv_kernel, d=d, attn_scale=head_dim ** -0.5, layer=layer, with_cache=with_cache),
        grid=(n // tm,),
        in_specs=in_specs,
        out_specs=out_specs,
        out_shape=out_shape,
        input_output_aliases=aliases,
        compiler_params=_cparams("arbitrary"),
        name="norm_qkv",
    )(*args)


def _pair_masks():
    lane = lax.broadcasted_iota(jnp.int32, (1, HEAD_PAIR), 1)
    first = lane < HEAD_PAIR // 2
    return first, jnp.logical_not(first)


def _dot_nt(a, b):
    return lax.dot_general(a, b, (((1,), (1,)), ((), ())), preferred_element_type=F32)


def _ctx_attn_kernel(q_ref, k_ref, v_ref, x_ref, mod_ref, wo_ref, o_ref, att_ref, *, npairs):
    first, second = _pair_masks()
    for p in range(npairs):
        qp, kp, vp = q_ref[p], k_ref[p], v_ref[p]
        outs = []
        for msk in (first, second):
            qh = jnp.where(msk, qp, jnp.zeros_like(qp))
            s = _dot_nt(qh, kp)
            m = jnp.max(s, axis=-1, keepdims=True)
            e = jnp.exp(s - m)
            l = jnp.sum(e, axis=-1, keepdims=True)
            outs.append(jnp.dot(e.astype(BF16), vp, preferred_element_type=F32) / l)
        att_ref[:, p * HEAD_PAIR:(p + 1) * HEAD_PAIR] = jnp.where(first, outs[0], outs[1]).astype(BF16)
    o = jnp.dot(att_ref[...], wo_ref[...], preferred_element_type=F32)
    o_ref[...] = x_ref[...] + mod_ref[2:3, :] * o


def _ctx_attention(qp, kp, vp, x, mod, wo, layer, *, seq):
    n, d = x.shape
    npairs = d // HEAD_PAIR
    pair_spec = pl.BlockSpec((npairs, seq, HEAD_PAIR), lambda b: (0, b, 0))
    return pl.pallas_call(
        functools.partial(_ctx_attn_kernel, npairs=npairs),
        grid=(n // seq,),
        in_specs=[
            pair_spec, pair_spec, pair_spec,
            pl.BlockSpec((seq, d), lambda b: (b, 0)),
            pl.BlockSpec((None, N_MOD, d), lambda b: (0, 0, 0)),
            _single((None, d, d), lambda b: (layer, 0, 0)),
        ],
        out_specs=pl.BlockSpec((seq, d), lambda b: (b, 0)),
        out_shape=jax.ShapeDtypeStruct((n, d), F32),
        scratch_shapes=[pltpu.VMEM((seq, d), BF16)],
        compiler_params=_cparams("arbitrary"),
        name="context_attention",
    )(qp, kp, vp, x, mod, wo)


def _na_attn_kernel(q_ref, k_ref, v_ref, ck_ref, cv_ref, strip_ref, x_ref, mod_ref, wo_ref, o_ref,
                    bias_ref, att_ref, *, npairs, rows, q_tile):
    first, second = _pair_masks()
    kr = min(WIN_R, rows)
    n_loc = rows * GRID_W
    tile_rows = q_tile // GRID_W
    win_rows = min(kr + tile_rows, rows)

    bias_ref[...] = jnp.full(bias_ref.shape, -jnp.inf, F32)

    def pair_body(p, carry):
        for hh in range(2):
            for qr in range(rows):
                rs = min(max(qr - kr // 2, 0), rows - kr)
                off = rs - qr + (WIN_R - 1)
                bias_ref[hh, qr * GRID_W:(qr + 1) * GRID_W, rs * GRID_W:(rs + kr) * GRID_W] = (
                    strip_ref[2 * p + hh, :, off * GRID_W:(off + kr) * GRID_W])
        ckp, cvp = ck_ref[p], cv_ref[p]
        for qt in range(n_loc // q_tile):
            rsl = slice(qt * q_tile, (qt + 1) * q_tile)
            ks = min(max(qt * tile_rows - kr // 2, 0), rows - win_rows)
            ksl = slice(ks * GRID_W, (ks + win_rows) * GRID_W)
            qp, kp, vp = q_ref[p, rsl, :], k_ref[p, ksl, :], v_ref[p, ksl, :]
            outs = []
            for hh, msk in enumerate((first, second)):
                qh = jnp.where(msk, qp, jnp.zeros_like(qp))
                s_loc = _dot_nt(qh, kp) + bias_ref[hh, rsl, ksl]
                s_ctx = _dot_nt(qh, ckp)
                m = jnp.maximum(jnp.max(s_loc, axis=-1, keepdims=True), jnp.max(s_ctx, axis=-1, keepdims=True))
                e_loc = jnp.exp(s_loc - m)
                e_ctx = jnp.exp(s_ctx - m)
                l = jnp.sum(e_loc, axis=-1, keepdims=True) + jnp.sum(e_ctx, axis=-1, keepdims=True)
                o = (jnp.dot(e_loc.astype(BF16), vp, preferred_element_type=F32)
                     + jnp.dot(e_ctx.astype(BF16), cvp, preferred_element_type=F32))
                outs.append(o / l)
            att_ref[p, rsl, :] = jnp.where(first, outs[0], outs[1]).astype(BF16)
        return carry

    lax.fori_loop(0, npairs, pair_body, 0)
    att = jnp.concatenate([att_ref[p] for p in range(npairs)], axis=-1)
    o = jnp.dot(att, wo_ref[...], preferred_element_type=F32)
    o_ref[...] = x_ref[...] + mod_ref[2:3, :] * o


def _na_attention(qp, kp, vp, ckp, cvp, strip, x, mod, wo, layer, *, n_batch):
    n, d = x.shape
    npairs = d // HEAD_PAIR
    n_loc = n // n_batch
    past = ckp.shape[3]
    rows = n_loc // GRID_W
    pair_spec = _single((npairs, n_loc, HEAD_PAIR), lambda b: (0, b, 0))
    ctx_spec = _single((None, None, npairs, past, HEAD_PAIR), lambda b: (layer, b, 0, 0, 0))
    return pl.pallas_call(
        functools.partial(_na_attn_kernel, npairs=npairs, rows=rows, q_tile=256),
        grid=(n_batch,),
        in_specs=[
            pair_spec, pair_spec, pair_spec, ctx_spec, ctx_spec,
            _single((None,) + strip.shape[1:], lambda b: (layer, 0, 0, 0)),
            _single((n_loc, d), lambda b: (b, 0)),
            pl.BlockSpec((None, N_MOD, d), lambda b: (1 + b, 0, 0)),
            _single((None, d, d), lambda b: (layer, 0, 0)),
        ],
        out_specs=_single((n_loc, d), lambda b: (b, 0)),
        out_shape=jax.ShapeDtypeStruct((n, d), F32),
        scratch_shapes=[pltpu.VMEM((2, n_loc, n_loc), F32), pltpu.VMEM((npairs, n_loc, HEAD_PAIR), BF16)],
        compiler_params=_cparams("arbitrary"),
        name="neighbourhood_attention",
    )(qp, kp, vp, ckp, cvp, strip, x, mod, wo)


def _rpb_strip(rpb):
    qc = np.arange(GRID_W)[:, None]
    kc = np.arange(GRID_W)[None, :]
    col_start = np.clip(qc - WIN_C // 2, 0, GRID_W - WIN_C)
    in_win = (kc >= col_start) & (kc < col_start + WIN_C)
    dc = np.clip(kc - qc, -(WIN_C - 1), WIN_C - 1) + (WIN_C - 1)
    sel = (dc[:, :, None] == np.arange(2 * WIN_C - 1)).astype(np.float32)
    t = jnp.einsum('lhdj,qkj->lhqdk', rpb.astype(F32), sel, precision=lax.Precision.HIGHEST)
    t = jnp.where(in_win[None, None, :, None, :], t, -jnp.inf)
    n_layers, h, ndr = rpb.shape[:3]
    return t.reshape(n_layers, h, GRID_W, ndr * GRID_W)


def _ffn_kernel(xa_ref, xb_ref, mod_ref, g_ref, w1_ref, w3_ref, w2_ref, oa_ref, ob_ref, t_ref, *, tiles_a):
    def tile(x_ref, o_ref):
        x = x_ref[...]
        h = _norm_mod(x, g_ref[...], mod_ref[3:4, :], mod_ref[4:5, :]).astype(BF16)
        for c in range(w1_ref.shape[1] // MXU_DIM):
            sl = slice(c * MXU_DIM, (c + 1) * MXU_DIM)
            a = jnp.dot(h, w1_ref[:, sl], preferred_element_type=F32)
            b = jnp.dot(h, w3_ref[:, sl], preferred_element_type=F32)
            t_ref[:, sl] = (a * jax.nn.sigmoid(a) * b).astype(BF16)
        o_ref[...] = x + mod_ref[5:6, :] * jnp.dot(t_ref[...], w2_ref[...], preferred_element_type=F32)

    first_stream = pl.program_id(0) < tiles_a
    pl.when(first_stream)(functools.partial(tile, xa_ref, oa_ref))
    pl.when(jnp.logical_not(first_stream))(functools.partial(tile, xb_ref, ob_ref))


def _ffn(xa, xb, mod, group_of_tile_b, g, w1, w3, w2, layer, *, tm):
    d = xa.shape[1]
    d_ff = w1.shape[-1]
    tiles_a, tiles_b = xa.shape[0] // tm, xb.shape[0] // tm
    a_map = lambda i: (jnp.minimum(i, tiles_a - 1), 0)
    b_map = lambda i: (jnp.maximum(i - tiles_a, 0), 0)
    group = lambda i: jnp.where(i < tiles_a, 0, group_of_tile_b(jnp.maximum(i - tiles_a, 0)))
    return pl.pallas_call(
        functools.partial(_ffn_kernel, tiles_a=tiles_a),
        grid=(tiles_a + tiles_b,),
        in_specs=[
            pl.BlockSpec((tm, d), a_map),
            pl.BlockSpec((tm, d), b_map),
            pl.BlockSpec((None, N_MOD, d), lambda i: (group(i), 0, 0)),
            pl.BlockSpec((1, d), lambda i: (0, 0)),
            _single((None, d, d_ff), lambda i: (layer, 0, 0)),
            _single((None, d, d_ff), lambda i: (layer, 0, 0)),
            _single((None, d_ff, d), lambda i: (layer, 0, 0)),
        ],
        out_specs=[pl.BlockSpec((tm, d), a_map), pl.BlockSpec((tm, d), b_map)],
        out_shape=[jax.ShapeDtypeStruct(xa.shape, F32), jax.ShapeDtypeStruct(xb.shape, F32)],
        scratch_shapes=[pltpu.VMEM((tm, d_ff), BF16)],
        compiler_params=_cparams("arbitrary"),
        name="swiglu_ffn",
    )(xa, xb, mod, g, w1, w3, w2)


def _lane_half_swap(x):
    n = x.shape[-1]
    lane = lax.broadcasted_iota(jnp.int32, (1, n), 1)
    return jnp.where(lane % LANES < LANES // 2, pltpu.roll(x, n - LANES // 2, axis=1),
                     pltpu.roll(x, LANES // 2, axis=1))


def _quad_pairs(t0, t1, t2, t3):
    lo = lax.broadcasted_iota(jnp.int32, (1, t0.shape[-1]), 1) % LANES < LANES // 2
    s0, s1, s2, s3 = (_lane_half_swap(t) for t in (t0, t1, t2, t3))
    even = (jnp.where(lo, t0, s1), jnp.where(lo, t2, s3))
    odd = (jnp.where(lo, s0, t1), jnp.where(lo, s2, t3))
    return even, odd


def _ssm_kernel(x_ref, mod_ref, g_ref, dskip_ref, win_ref, wcar_ref, kin_ref, a4_ref, h0_ref, wglu_ref, *rest,
                seq, n_seg, with_state):
    if with_state:
        o_ref, sre_ref, sim_ref = rest[:3]
    else:
        o_ref = rest[0]
    u_ref, y_ref, z_ref, sp_ref, f_ref, e_ref = rest[-6:]
    d = x_ref.shape[2]
    cps = win_ref.shape[0]
    nsq = a4_ref.shape[-1] // 2
    nst = cps * nsq
    nq = seq // QUAD
    steps_per_dir = (d // Q_CH) // cps
    tb = 4 * QUAD
    qb = tb // QUAD * SUBLANES
    dk = pl.program_id(1)
    kb = (dk % steps_per_dir) * cps
    fwd = dk // steps_per_dir == 0

    @pl.when(dk == 0)
    def _prepare():
        gain, shift, scale = g_ref[...], mod_ref[0], mod_ref[1]

        def body(i, carry):
            t0 = pl.multiple_of(i * tb, tb)
            xt = jnp.swapaxes(x_ref[:, pl.ds(t0, tb), :], 0, 1)
            h = _norm_mod(xt, gain, shift, scale).reshape(tb // QUAD, QUAD, SUBLANES, d)
            rows = pl.ds(pl.multiple_of(i * qb, qb), qb)
            halves = _quad_pairs(*(h[:, s].reshape(qb, d) for s in range(QUAD)))
            for m in range(d // LANES):
                for parity, (v01, v23) in enumerate(halves):
                    k = 2 * m + parity
                    sl = slice(m * LANES, (m + 1) * LANES)
                    u_ref[k, rows, :LANES] = v01[:, sl].astype(BF16)
                    u_ref[k, rows, LANES:] = v23[:, sl].astype(BF16)
                    y_ref[k, rows, :LANES] = v01[:, sl] * dskip_ref[k, :, :LANES]
                    y_ref[k, rows, LANES:] = v23[:, sl] * dskip_ref[k, :, LANES:]
            return carry

        lax.fori_loop(0, seq // tb, body, 0)

    a_re = jnp.broadcast_to(jnp.concatenate([a4_ref[c][:, :nsq] for c in range(cps)], axis=-1), (SUBLANES, nst))
    a_im = jnp.broadcast_to(jnp.concatenate([a4_ref[c][:, nsq:] for c in range(cps)], axis=-1), (SUBLANES, nst))

    def chunk_cols(ref, c):
        return jnp.concatenate([ref[:, c * nsq:(c + 1) * nsq], ref[:, nst + c * nsq:nst + (c + 1) * nsq]], axis=-1)

    def scan(state, store, reverse):
        s_re, s_im = state
        pairs = range(nq // 2)
        for m in (reversed(pairs) if reverse else pairs):
            entering = {}
            for j in ((2 * m + 1, 2 * m) if reverse else (2 * m, 2 * m + 1)):
                rsl = slice(j * SUBLANES, (j + 1) * SUBLANES)
                entering[j] = (s_re, s_im)
                s_re, s_im = (a_re * s_re - a_im * s_im + z_ref[rsl, :nst],
                              a_re * s_im + a_im * s_re + z_ref[rsl, nst:])
            if store:
                psl = slice(2 * m * SUBLANES, (2 * m + 2) * SUBLANES)
                lo, hi = entering[2 * m], entering[2 * m + 1]
                sp_ref[psl, :nst] = jnp.concatenate([lo[0], hi[0]], axis=0).astype(BF16)
                sp_ref[psl, nst:] = jnp.concatenate([lo[1], hi[1]], axis=0).astype(BF16)
        return s_re, s_im

    def run_direction(reverse):
        for c in range(cps):
            z = jnp.dot(u_ref[kb + c], win_ref[c], preferred_element_type=F32)
            z_ref[:, c * nsq:(c + 1) * nsq] = z[:, :nsq]
            z_ref[:, nst + c * nsq:nst + (c + 1) * nsq] = z[:, nsq:]
        start = (jnp.concatenate([h0_ref[c][:, :nsq] for c in range(cps)], axis=-1),
                 jnp.concatenate([h0_ref[c][:, nsq:] for c in range(cps)], axis=-1))
        if n_seg > 1:
            zero = jnp.zeros((SUBLANES, nst), F32)
            e_re, e_im = scan((zero, zero), False, reverse)
            e_ref[:, :nst] = e_re
            e_ref[:, nst:] = e_im
            f_ref[:, :nst] = start[0]
            f_ref[:, nst:] = start[1]
            p_re, p_im = a_re[:1], a_im[:1]
            for _ in range(int(math.log2(nq))):
                p_re, p_im = p_re * p_re - p_im * p_im, 2.0 * p_re * p_im
            for b in range(SUBLANES // n_seg):
                for s in (range(n_seg - 2, -1, -1) if reverse else range(1, n_seg)):
                    dst = b * n_seg + s
                    src = dst + 1 if reverse else dst - 1
                    f_re, f_im = f_ref[src:src + 1, :nst], f_ref[src:src + 1, nst:]
                    f_ref[dst:dst + 1, :nst] = p_re * f_re - p_im * f_im + e_ref[src:src + 1, :nst]
                    f_ref[dst:dst + 1, nst:] = p_re * f_im + p_im * f_re + e_ref[src:src + 1, nst:]
            start = (f_ref[:, :nst], f_ref[:, nst:])
        fin_re, fin_im = scan(start, True, reverse)
        if with_state:
            for c in range(cps):
                sre_ref[c] = fin_re[:, c * nsq:(c + 1) * nsq]
                sim_ref[c] = fin_im[:, c * nsq:(c + 1) * nsq]
        for c in range(cps):
            y_ref[kb + c] = (y_ref[kb + c]
                             + _dot_nt(chunk_cols(sp_ref, c), wcar_ref[c])
                             + jnp.dot(u_ref[kb + c], kin_ref[c], preferred_element_type=F32))

    pl.when(fwd)(functools.partial(run_direction, False))
    pl.when(jnp.logical_not(fwd))(functools.partial(run_direction, True))

    @pl.when(dk == 2 * steps_per_dir - 1)
    def _finish():
        gate = mod_ref[2]
        fqd = min(FINISH_QUADS, nq // 8)
        fq = fqd * SUBLANES
        ft = fqd * QUAD
        n_blk = nq // fqd

        def glu_in(i, b):
            rows = pl.ds(pl.multiple_of(i * fq, fq), fq)
            gathered = [jnp.concatenate([y_ref[2 * m + parity, rows, half] for m in range(d // LANES)], axis=-1)
                        for half in (slice(0, LANES), slice(LANES, 2 * LANES)) for parity in (0, 1)]
            (y0, y2), (y1, y3) = _quad_pairs(*gathered)
            y = jnp.stack([t.reshape(fqd, SUBLANES, d) for t in (y0, y1, y2, y3)], axis=1)
            z = jax.nn.gelu(y.reshape(ft * SUBLANES, d)).astype(BF16)
            z_ref[b * ft * SUBLANES:(b + 1) * ft * SUBLANES, :] = jnp.dot(z, wglu_ref[...],
                                                                        preferred_element_type=F32)

        def glu_out(i, b):
            gl = z_ref[b * ft * SUBLANES:(b + 1) * ft * SUBLANES, :]
            out = (gl[:, :d] * jax.nn.sigmoid(gl[:, d:])).reshape(ft, SUBLANES, d) * gate
            for j in range(ft // tb):
                t0 = pl.multiple_of(i * ft + j * tb, tb)
                upd = jnp.swapaxes(out[j * tb:(j + 1) * tb], 0, 1)
                o_ref[:, pl.ds(t0, tb), :] = x_ref[:, pl.ds(t0, tb), :] + upd

        def pair(jj, carry):
            i = 2 * jj
            glu_in(i + 1, 1)
            glu_out(i, 0)
            glu_in(i + 2, 0)
            glu_out(i + 1, 1)
            return carry

        glu_in(0, 0)
        lax.fori_loop(0, n_blk // 2 - 1, pair, 0)
        glu_in(n_blk - 1, 1)
        glu_out(n_blk - 2, 0)
        glu_out(n_blk - 1, 1)


def _ssm_mixer(x, mod8, g, dskip, w_in, w_car, k_in, a4, h0, wglu, layer, *, seq, n_seg, with_state):
    n_seq, _, d = x.shape
    n_groups = n_seq // SUBLANES
    n_chunks = d // Q_CH
    nsq = a4.shape[-1] // 2
    cps = CHUNKS_PER_STEP
    steps_per_dir = n_chunks // cps
    rows_q = seq // QUAD * SUBLANES
    assert cps * nsq == d and seq % (32 * QUAD) == 0
    dk_map = lambda gi, dk: (layer, dk // steps_per_dir, dk % steps_per_dir, 0, 0)
    seq_map = lambda gi, dk: (dk // steps_per_dir, dk % steps_per_dir, gi, 0)
    x_spec = _single((SUBLANES, seq, d), lambda gi, dk: (gi, 0, 0))
    out_specs = [x_spec]
    out_shape = [jax.ShapeDtypeStruct(x.shape, F32)]
    if with_state:
        st_spec = pl.BlockSpec((None, cps, SUBLANES, nsq), seq_map)
        out_specs += [st_spec, st_spec]
        out_shape += [jax.ShapeDtypeStruct((2, n_chunks, n_seq, nsq), F32)] * 2
    return pl.pallas_call(
        functools.partial(_ssm_kernel, seq=seq, n_seg=n_seg, with_state=with_state),
        grid=(n_groups, 2 * steps_per_dir),
        in_specs=[
            pl.BlockSpec((SUBLANES, seq, d), lambda gi, dk: (gi, 0, 0)),
            pl.BlockSpec((N_MOD, SUBLANES, d), lambda gi, dk: (0, 0, 0)),
            pl.BlockSpec((1, d), lambda gi, dk: (0, 0)),
            pl.BlockSpec((n_chunks, 1, MXU_DIM), lambda gi, dk: (0, 0, 0)),
            pl.BlockSpec((None, None, cps) + w_in.shape[-2:], dk_map),
            pl.BlockSpec((None, None, cps) + w_car.shape[-2:], dk_map),
            pl.BlockSpec((None, None, cps, MXU_DIM, MXU_DIM), dk_map),
            pl.BlockSpec((None, None, cps, 1, 2 * nsq), dk_map),
            pl.BlockSpec((None, cps, SUBLANES, 2 * nsq), seq_map),
            _single((None, d, 2 * d), lambda gi, dk: (layer, 0, 0)),
        ],
        out_specs=out_specs,
        out_shape=out_shape,
        scratch_shapes=[
            pltpu.VMEM((n_chunks, rows_q, MXU_DIM), BF16),
            pltpu.VMEM((n_chunks, rows_q, MXU_DIM), F32),
            pltpu.VMEM((rows_q, 2 * cps * nsq), F32),
            pltpu.VMEM((rows_q, 2 * cps * nsq), BF16),
            pltpu.VMEM((SUBLANES, 2 * cps * nsq), F32),
            pltpu.VMEM((SUBLANES, 2 * cps * nsq), F32),
        ],
        compiler_params=_cparams("arbitrary", "arbitrary"),
        name="s5_mixer",
    )(x, mod8, g, dskip, w_in, w_car, k_in, a4, h0, wglu)


def _ssm_params(lam_re, lam_im, log_step, b_re, b_im, c_re, c_im):
    lam_re, lam_im = lam_re.astype(F32), lam_im.astype(F32)
    step = jnp.exp(log_step.astype(F32))[..., None]
    mag = jnp.exp(lam_re * step)
    a_re, a_im = mag * jnp.cos(lam_im * step), mag * jnp.sin(lam_im * step)
    den = lam_re * lam_re + lam_im * lam_im
    nr, ni = a_re - 1.0, a_im
    f_re = (nr * lam_re + ni * lam_im) / den
    f_im = (ni * lam_re - nr * lam_im) / den
    bt_re, bt_im = jnp.swapaxes(b_re.astype(F32), -1, -2), jnp.swapaxes(b_im.astype(F32), -1, -2)
    bb = (f_re[..., None, :] * bt_re - f_im[..., None, :] * bt_im,
          f_re[..., None, :] * bt_im + f_im[..., None, :] * bt_re)
    cc = (c_re.astype(F32), c_im.astype(F32))

    def cmul(x, y):
        return x[0] * y[0] - x[1] * y[1], x[0] * y[1] + x[1] * y[0]

    pw = [(jnp.ones_like(a_re), jnp.zeros_like(a_re))]
    for _ in range(QUAD):
        pw.append(cmul(pw[-1], (a_re, a_im)))

    def by_dir(n_fwd, n_bwd):
        return tuple(jnp.stack([pw[n_fwd][i][:, 0], pw[n_bwd][i][:, 1]], axis=1)[..., None, :] for i in range(2))

    n_layers, _, n_groups, c, p = bb[0].shape
    gq = Q_CH // c
    kk = n_groups // gq
    nsq = gq * p
    lead = (n_layers, 2, kk)

    def pack(ws, sign):
        def rows(i):
            t = jnp.stack([w[i] for w in ws], axis=2).reshape(n_layers, 2, QUAD, kk, gq, c, p)
            return t.transpose(0, 1, 3, 2, 4, 5, 6).reshape(*lead, MXU_DIM, p)
        return jnp.concatenate([rows(0), sign * rows(1)], axis=-1).astype(BF16)

    w_in = pack([cmul(by_dir(QUAD - 1 - s, s), bb) for s in range(QUAD)], 1.0)
    w_car = pack([cmul(cc, by_dir(s + 1, QUAD - s)) for s in range(QUAD)], -1.0)

    gc = gq * c
    same_g = (np.arange(gc) // c)[:, None] == (np.arange(gc) // c)[None, :]
    cb = [cmul(cc, tuple(t[..., None, :] for t in pw[n])) for n in range(QUAD)]
    cb = jnp.concatenate([jnp.concatenate([t[0] for t in cb], axis=-2),
                          -jnp.concatenate([t[1] for t in cb], axis=-2)], axis=-1)
    taps = jnp.einsum('ldgkp,ldgcp->ldgck', cb, jnp.concatenate(bb, axis=-1), precision=lax.Precision.HIGHEST)
    tiles = []
    for n in range(QUAD):
        tap = taps[..., n * c:(n + 1) * c]
        tiles.append(jnp.where(same_g, jnp.tile(tap.reshape(*lead, gc, c), (1, 1, 1, 1, gq)), 0.0))
    zero = jnp.zeros_like(tiles[0][:, 0])

    def tile_of(d, s, t):
        n = t - s if d == 0 else s - t
        return tiles[n][:, d] if n >= 0 else zero

    k_in = jnp.stack([jnp.concatenate([jnp.concatenate([tile_of(d, s, t) for t in range(QUAD)], axis=-1)
                                       for s in range(QUAD)], axis=-2) for d in range(2)], axis=1)

    a4 = jnp.concatenate([pw[QUAD][0].reshape(*lead, 1, nsq), pw[QUAD][1].reshape(*lead, 1, nsq)], axis=-1)
    return w_in.astype(BF16), w_car.astype(BF16), k_in.astype(BF16), a4


def _block_diag_kernel(win_ref, wcar_ref, win_o_ref, wcar_o_ref, *, gq):
    pp = win_ref.shape[-1] // 2
    nsq = gq * pp
    src = lax.broadcasted_iota(jnp.int32, (2 * pp, 2 * nsq), 0)
    col = lax.broadcasted_iota(jnp.int32, (2 * pp, 2 * nsq), 1)
    spread = ((col % pp == src % pp) & ((col >= nsq) == (src >= pp))).astype(BF16)
    row_g = (lax.broadcasted_iota(jnp.int32, (MXU_DIM, 2 * nsq), 0) // (Q_CH // gq)) % gq
    col_g = (lax.broadcasted_iota(jnp.int32, (MXU_DIM, 2 * nsq), 1) % nsq) // pp
    for src_ref, dst_ref in ((win_ref, win_o_ref), (wcar_ref, wcar_o_ref)):
        for c in range(src_ref.shape[0]):
            w = jnp.dot(src_ref[c], spread, preferred_element_type=F32)
            dst_ref[c] = jnp.where(row_g == col_g, w, 0.0).astype(BF16)


def _block_diag(w_in, w_car, *, gq):
    n_layers, n_dir, kk, rows, two_p = w_in.shape
    cps = CHUNKS_PER_STEP
    spec_in = pl.BlockSpec((None, None, cps, rows, two_p), lambda l, dr, k: (l, dr, k, 0, 0))
    spec_out = pl.BlockSpec((None, None, cps, rows, gq * two_p), lambda l, dr, k: (l, dr, k, 0, 0))
    out = jax.ShapeDtypeStruct((n_layers, n_dir, kk, rows, gq * two_p), BF16)
    return pl.pallas_call(
        functools.partial(_block_diag_kernel, gq=gq),
        grid=(n_layers, n_dir, kk // cps),
        in_specs=[spec_in, spec_in],
        out_specs=[spec_out, spec_out],
        out_shape=[out, out],
        compiler_params=_cparams("arbitrary", "arbitrary", "arbitrary"),
        name="s5_block_diag",
    )(w_in, w_car)


def kernel(x_prompt, x_sample, cache_k, cache_v, state_ssm_re, state_ssm_im, c, c_ctx, norm_mix, norm_ffn, ada_w, ada_b, na_w_qkv, na_w_o, na_q_gain, na_k_gain, na_rpb, ssm_lambda_re, ssm_lambda_im, ssm_log_step, ssm_b_re, ssm_b_im, ssm_c_re, ssm_c_im, ssm_d, ssm_w_glu, ffn_w1, ffn_w3, ffn_w2):
    batch, seq, d = x_prompt.shape
    dec_batch, dec_seq, _ = x_sample.shape
    depth = ada_w.shape[0]
    heads, head_dim = cache_k.shape[3], cache_k.shape[4]
    past = cache_k.shape[2]
    npairs = d // HEAD_PAIR
    d_ff = ffn_w1.shape[-1]
    n_dir = state_ssm_re.shape[2]
    n_seg = dec_seq // seq
    assert seq % (FINISH_QUADS * QUAD) == 0 and dec_seq % seq == 0 and SUBLANES % n_seg == 0
    assert dec_batch * n_seg == SUBLANES and batch % SUBLANES == 0 and seq & (seq - 1) == 0
    assert d_ff % MXU_DIM == 0 and d % (Q_CH * CHUNKS_PER_STEP) == 0 and head_dim * 2 == HEAD_PAIR
    assert Q_CH * 2 == LANES and Q_CH % SSM_GROUP == 0

    cond8 = jnp.zeros((SUBLANES, d), F32).at[0].set(c_ctx).at[1:1 + dec_batch].set(c)
    mod = _modulation(cond8, ada_w, ada_b).reshape(depth, SUBLANES, N_MOD, d)

    tm = math.gcd(512, dec_seq)
    prompt_group = lambda i: 0
    sample_group = lambda i: 1 + (i * tm) // dec_seq

    head_mean = jnp.kron(jnp.eye(MXU_DIM // head_dim, dtype=F32),
                         jnp.full((head_dim, head_dim), 1.0 / head_dim, F32)).astype(BF16)

    w_qkv, w_o, w_glu = na_w_qkv.astype(BF16), na_w_o.astype(BF16), ssm_w_glu.astype(BF16)
    w1, w3, w2 = ffn_w1.astype(BF16), ffn_w3.astype(BF16), ffn_w2.astype(BF16)
    n_na = cache_k.shape[1]
    ckp = cache_k.astype(BF16).reshape(dec_batch, n_na, past, npairs, HEAD_PAIR).transpose(1, 0, 3, 2, 4)
    cvp = cache_v.astype(BF16).reshape(dec_batch, n_na, past, npairs, HEAD_PAIR).transpose(1, 0, 3, 2, 4)
    strip = _rpb_strip(na_rpb)
    w_in, w_car, k_in, a4 = _ssm_params(ssm_lambda_re, ssm_lambda_im, ssm_log_step, ssm_b_re, ssm_b_im,
                                        ssm_c_re, ssm_c_im)
    s5_w = (*_block_diag(w_in, w_car, gq=Q_CH // SSM_GROUP), k_in, a4)

    xp = x_prompt.reshape(batch * seq, d)
    xs = x_sample.reshape(dec_batch * dec_seq, d)
    new_cache, new_sre, new_sim = None, [], []
    for i in range(depth):
        j = i // 2
        g_mix = norm_mix[i].reshape(1, d)
        if i % 2 == 0:
            q_gain = jnp.tile(na_q_gain[j], heads).reshape(1, d)
            k_gain = jnp.tile(na_k_gain[j], heads).reshape(1, d)
            qp, kp, vp, *new_cache = _qkv(xp, mod[i], prompt_group, g_mix, w_qkv, j, n_na, q_gain, k_gain,
                                          head_mean, new_cache, tm=tm, head_dim=head_dim, seq=seq, with_cache=True)
            xp = _ctx_attention(qp, kp, vp, xp, mod[i], w_o, j, seq=seq)
            qs, ks, vs = _qkv(xs, mod[i], sample_group, g_mix, w_qkv, j, n_na, q_gain, k_gain, head_mean, None,
                              tm=tm, head_dim=head_dim, seq=seq, with_cache=False)
            xs = _na_attention(qs, ks, vs, ckp, cvp, strip, xs, mod[i], w_o, j, n_batch=dec_batch)
        else:
            n_chunks = d // Q_CH
            dskip = jnp.tile(ssm_d[j].astype(F32).reshape(n_chunks, 1, Q_CH), (1, 1, QUAD))
            ns = a4.shape[-1] // 2
            mod_p = jnp.broadcast_to(mod[i, 0][:, None, :], (N_MOD, SUBLANES, d))
            h0_p = jnp.zeros((n_dir, n_chunks, batch, 2 * ns), F32)
            xp, sre, sim = _ssm_mixer(xp.reshape(batch, seq, d), mod_p, g_mix, dskip, *s5_w, h0_p, w_glu, j,
                                      seq=seq, n_seg=1, with_state=True)
            xp = xp.reshape(batch * seq, d)
            new_sre.append(sre.transpose(2, 0, 1, 3).reshape(batch, n_dir, d // SSM_GROUP, SSM_STATE))
            new_sim.append(sim.transpose(2, 0, 1, 3).reshape(batch, n_dir, d // SSM_GROUP, SSM_STATE))
            mod_s = jnp.repeat(mod[i, 1:1 + dec_batch], n_seg, axis=0).transpose(1, 0, 2)
            s_re = state_ssm_re[:, j].astype(F32).reshape(dec_batch, n_dir, n_chunks, ns)
            s_im = state_ssm_im[:, j].astype(F32).reshape(dec_batch, n_dir, n_chunks, ns)
            s0 = jnp.concatenate([s_re, s_im], axis=-1)
            h0_s = jnp.zeros((dec_batch, n_seg, n_dir, n_chunks, 2 * ns), F32)
            h0_s = h0_s.at[:, 0, 0].set(s0[:, 0]).at[:, n_seg - 1, 1].set(s0[:, 1])
            h0_s = h0_s.reshape(dec_batch * n_seg, n_dir, n_chunks, 2 * ns).transpose(1, 2, 0, 3)
            (xs,) = _ssm_mixer(xs.reshape(dec_batch * n_seg, seq, d), mod_s, g_mix, dskip, *s5_w, h0_s, w_glu, j,
                               seq=seq, n_seg=n_seg, with_state=False)
            xs = xs.reshape(dec_batch * dec_seq, d)
        g_ffn = norm_ffn[i].reshape(1, d)
        xp, xs = _ffn(xp, xs, mod[i], sample_group, g_ffn, w1, w3, w2, i, tm=tm)
    new_k, new_v = (t.reshape(batch, n_na, heads, head_dim, seq).transpose(0, 1, 4, 2, 3) for t in new_cache)
    return (xp.reshape(batch, seq, d), xs.reshape(dec_batch, dec_seq, d), new_k, new_v,
            jnp.stack(new_sre, axis=1), jnp.stack(new_sim, axis=1))
```

```python
import functools
import math

import jax
import jax.numpy as jnp
import numpy as np
from jax import lax
from jax.experimental import pallas as pl
from jax.experimental.pallas import tpu as pltpu

F32 = jnp.float32
BF16 = jnp.bfloat16

EPS = 1e-6
N_MOD = 6
GRID_W = 64
WIN_R = 8
WIN_C = 16
SSM_GROUP = 16
SSM_STATE = 64

SUBLANES = 8
LANES = 128
MXU_DIM = 256
VMEM_LIMIT_BYTES = 56 * 1024 * 1024

HEAD_PAIR = LANES
QUAD = 4
Q_CH = MXU_DIM // QUAD
CHUNKS_PER_STEP = 4
FINISH_QUADS = 8


def _cparams(*sem):
    return pltpu.CompilerParams(dimension_semantics=sem, vmem_limit_bytes=VMEM_LIMIT_BYTES)


def _single(block_shape, index_map):
    return pl.BlockSpec(block_shape, index_map, pipeline_mode=pl.Buffered(1))


def _norm_mod(x, g, shift, scale):
    ms = jnp.mean(x * x, axis=-1, keepdims=True)
    y = x * lax.rsqrt(ms + EPS) * g
    return y * (1.0 + scale) + shift


def _mod_kernel(cond_ref, w_ref, b_ref, o_ref):
    c = cond_ref[...]
    a = (c * jax.nn.sigmoid(c)).astype(BF16)
    o_ref[...] = jnp.dot(a, w_ref[...].astype(BF16), preferred_element_type=F32) + b_ref[...]


def _modulation(cond8, ada_w, ada_b):
    depth, d, n = ada_w.shape
    tn = n // 4
    return pl.pallas_call(
        _mod_kernel,
        grid=(depth, n // tn),
        in_specs=[
            pl.BlockSpec((SUBLANES, d), lambda i, j: (0, 0)),
            pl.BlockSpec((None, d, tn), lambda i, j: (i, 0, j)),
            pl.BlockSpec((None, 1, tn), lambda i, j: (i, 0, j)),
        ],
        out_specs=pl.BlockSpec((None, SUBLANES, tn), lambda i, j: (i, 0, j)),
        out_shape=jax.ShapeDtypeStruct((depth, SUBLANES, n), F32),
        compiler_params=_cparams("arbitrary", "arbitrary"),
        name="adaln_modulation",
    )(cond8, ada_w, ada_b.reshape(depth, 1, n))


def _qkv_kernel(x_ref, mod_ref, g_ref, w_ref, qg_ref, kg_ref, hm_ref, *rest, d, attn_scale, layer, with_cache):
    out_refs = rest[-5:] if with_cache else rest
    qp_ref, kp_ref, vp_ref = out_refs[:3]
    h = _norm_mod(x_ref[...], g_ref[...], mod_ref[0:1, :], mod_ref[1:2, :]).astype(BF16)
    qkv = jnp.dot(h, w_ref[...], preferred_element_type=F32)
    q, k, v = qkv[:, :d], qkv[:, d:2 * d], qkv[:, 2 * d:]

    def head_norm(t, gain):
        parts = []
        for c in range(d // MXU_DIM):
            tc = t[:, c * MXU_DIM:(c + 1) * MXU_DIM]
            ms = jnp.dot((tc * tc).astype(BF16), hm_ref[...], preferred_element_type=F32)
            parts.append(tc * lax.rsqrt(ms + EPS))
        return jnp.concatenate(parts, axis=-1) * gain

    q = head_norm(q, qg_ref[...])
    k = head_norm(k, kg_ref[...])
    if with_cache:
        for ref, val in ((out_refs[3], k), (out_refs[4], v)):
            seq = ref.shape[-1]
            for s in range(ref.shape[0]):
                vt = val[s * seq:(s + 1) * seq, :].T
                if ref.ndim == 3:
                    ref[s] = vt
                else:
                    for l in range(ref.shape[1]):
                        ref[s, l] = vt if l == layer else jnp.zeros_like(vt)
    qs = (q * attn_scale).astype(BF16)
    kb = k.astype(BF16)
    vb = v.astype(BF16)
    for p in range(d // HEAD_PAIR):
        sl = slice(p * HEAD_PAIR, (p + 1) * HEAD_PAIR)
        qp_ref[p] = qs[:, sl]
        kp_ref[p] = kb[:, sl]
        vp_ref[p] = vb[:, sl]


def _qkv(x, mod, group_of_tile, g, w_qkv, layer, n_layers, q_gain, k_gain, head_mean, cache, *,
         tm, head_dim, seq, with_cache):
    n, d = x.shape
    npairs = d // HEAD_PAIR
    pair_spec = pl.BlockSpec((npairs, tm, HEAD_PAIR), lambda i: (0, i, 0))
    out_specs = [pair_spec] * 3
    out_shape = [jax.ShapeDtypeStruct((npairs, n, HEAD_PAIR), BF16)] * 3
    in_specs = [
        pl.BlockSpec((tm, d), lambda i: (i, 0)),
        pl.BlockSpec((None, N_MOD, d), lambda i: (group_of_tile(i), 0, 0)),
        pl.BlockSpec((1, d), lambda i: (0, 0)),
        _single((None, d, 3 * d), lambda i: (layer, 0, 0)),
        pl.BlockSpec((1, d), lambda i: (0, 0)),
        pl.BlockSpec((1, d), lambda i: (0, 0)),
        pl.BlockSpec((MXU_DIM, MXU_DIM), lambda i: (0, 0)),
    ]
    args = [x, mod, g, w_qkv, q_gain, k_gain, head_mean]
    aliases = {}
    if with_cache:
        out_shape += [jax.ShapeDtypeStruct((n // seq, n_layers, d, seq), F32)] * 2
        if cache is None:
            out_specs += [pl.BlockSpec((tm // seq, n_layers, d, seq), lambda i: (i, 0, 0, 0))] * 2
        else:
            out_specs += [pl.BlockSpec((tm // seq, None, d, seq), lambda i: (i, layer, 0, 0))] * 2
            aliases = {len(args): 3, len(args) + 1: 4}
            in_specs += [pl.BlockSpec(memory_space=pl.ANY)] * 2
            args += list(cache)
    return pl.pallas_call(
        functools.partial(_qkv_kernel, d=d, attn_scale=head_dim ** -0.5, layer=layer, with_cache=with_cache),
        grid=(n // tm,),
        in_specs=in_specs,
        out_specs=out_specs,
        out_shape=out_shape,
        input_output_aliases=aliases,
        compiler_params=_cparams("arbitrary"),
        name="norm_qkv",
    )(*args)


def _pair_masks():
    lane = lax.broadcasted_iota(jnp.int32, (1, HEAD_PAIR), 1)
    first = lane < HEAD_PAIR // 2
    return first, jnp.logical_not(first)


def _dot_nt(a, b):
    return lax.dot_general(a, b, (((1,), (1,)), ((), ())), preferred_element_type=F32)


def _ctx_attn_kernel(q_ref, k_ref, v_ref, x_ref, mod_ref, wo_ref, o_ref, att_ref, *, npairs, seq):
    first, second = _pair_masks()
    for sq in range(x_ref.shape[0] // seq):
        rsl = slice(sq * seq, (sq + 1) * seq)
        for p in range(npairs):
            qp, kp, vp = q_ref[p, rsl, :], k_ref[p, rsl, :], v_ref[p, rsl, :]
            outs = []
            for msk in (first, second):
                qh = jnp.where(msk, qp, jnp.zeros_like(qp))
                s = _dot_nt(qh, kp)
                m = jnp.max(s, axis=-1, keepdims=True)
                e = jnp.exp(s - m)
                l = jnp.sum(e, axis=-1, keepdims=True)
                outs.append(jnp.dot(e.astype(BF16), vp, preferred_element_type=F32) / l)
            att_ref[rsl, p * HEAD_PAIR:(p + 1) * HEAD_PAIR] = jnp.where(first, outs[0], outs[1]).astype(BF16)
    o = jnp.dot(att_ref[...], wo_ref[...], preferred_element_type=F32)
    o_ref[...] = x_ref[...] + mod_ref[2:3, :] * o


def _ctx_attention(qp, kp, vp, x, mod, wo, layer, *, seq, seqs_per_step):
    n, d = x.shape
    npairs = d // HEAD_PAIR
    rows = seq * seqs_per_step
    pair_spec = pl.BlockSpec((npairs, rows, HEAD_PAIR), lambda b: (0, b, 0))
    return pl.pallas_call(
        functools.partial(_ctx_attn_kernel, npairs=npairs, seq=seq),
        grid=(n // rows,),
        in_specs=[
            pair_spec, pair_spec, pair_spec,
            pl.BlockSpec((rows, d), lambda b: (b, 0)),
            pl.BlockSpec((None, N_MOD, d), lambda b: (0, 0, 0)),
            _single((None, d, d), lambda b: (layer, 0, 0)),
        ],
        out_specs=pl.BlockSpec((rows, d), lambda b: (b, 0)),
        out_shape=jax.ShapeDtypeStruct((n, d), F32),
        scratch_shapes=[pltpu.VMEM((rows, d), BF16)],
        compiler_params=_cparams("arbitrary"),
        name="context_attention",
    )(qp, kp, vp, x, mod, wo)


def _na_attn_kernel(q_ref, k_ref, v_ref, ck_ref, cv_ref, strip_ref, x_ref, mod_ref, wo_ref, o_ref,
                    bias_ref, att_ref, *, npairs, rows, q_tile):
    first, second = _pair_masks()
    kr = min(WIN_R, rows)
    n_loc = rows * GRID_W
    tile_rows = q_tile // GRID_W
    win_rows = min(kr + tile_rows, rows)

    bias_ref[...] = jnp.full(bias_ref.shape, -jnp.inf, F32)

    def pair_body(p, carry):
        for hh in range(2):
            for qr in range(rows):
                rs = min(max(qr - kr // 2, 0), rows - kr)
                off = rs - qr + (WIN_R - 1)
                bias_ref[hh, qr * GRID_W:(qr + 1) * GRID_W, rs * GRID_W:(rs + kr) * GRID_W] = (
                    strip_ref[2 * p + hh, :, off * GRID_W:(off + kr) * GRID_W])
        ckp, cvp = ck_ref[p], cv_ref[p]
        for qt in range(n_loc // q_tile):
            rsl = slice(qt * q_tile, (qt + 1) * q_tile)
            ks = min(max(qt * tile_rows - kr // 2, 0), rows - win_rows)
            ksl = slice(ks * GRID_W, (ks + win_rows) * GRID_W)
            qp, kp, vp = q_ref[p, rsl, :], k_ref[p, ksl, :], v_ref[p, ksl, :]
            outs = []
            for hh, msk in enumerate((first, second)):
                qh = jnp.where(msk, qp, jnp.zeros_like(qp))
                s_loc = _dot_nt(qh, kp) + bias_ref[hh, rsl, ksl]
                s_ctx = _dot_nt(qh, ckp)
                m = jnp.maximum(jnp.max(s_loc, axis=-1, keepdims=True), jnp.max(s_ctx, axis=-1, keepdims=True))
                e_loc = jnp.exp(s_loc - m)
                e_ctx = jnp.exp(s_ctx - m)
                l = jnp.sum(e_loc, axis=-1, keepdims=True) + jnp.sum(e_ctx, axis=-1, keepdims=True)
                o = (jnp.dot(e_loc.astype(BF16), vp, preferred_element_type=F32)
                     + jnp.dot(e_ctx.astype(BF16), cvp, preferred_element_type=F32))
                outs.append(o / l)
            att_ref[p, rsl, :] = jnp.where(first, outs[0], outs[1]).astype(BF16)
        return carry

    lax.fori_loop(0, npairs, pair_body, 0)
    att = jnp.concatenate([att_ref[p] for p in range(npairs)], axis=-1)
    o = jnp.dot(att, wo_ref[...], preferred_element_type=F32)
    o_ref[...] = x_ref[...] + mod_ref[2:3, :] * o


def _na_attention(qp, kp, vp, ckp, cvp, strip, x, mod, wo, layer, *, n_batch):
    n, d = x.shape
    npairs = d // HEAD_PAIR
    n_loc = n // n_batch
    past = ckp.shape[3]
    rows = n_loc // GRID_W
    pair_spec = _single((npairs, n_loc, HEAD_PAIR), lambda b: (0, b, 0))
    ctx_spec = _single((None, None, npairs, past, HEAD_PAIR), lambda b: (layer, b, 0, 0, 0))
    return pl.pallas_call(
        functools.partial(_na_attn_kernel, npairs=npairs, rows=rows, q_tile=256),
        grid=(n_batch,),
        in_specs=[
            pair_spec, pair_spec, pair_spec, ctx_spec, ctx_spec,
            _single((None,) + strip.shape[1:], lambda b: (layer, 0, 0, 0)),
            _single((n_loc, d), lambda b: (b, 0)),
            pl.BlockSpec((None, N_MOD, d), lambda b: (1 + b, 0, 0)),
            _single((None, d, d), lambda b: (layer, 0, 0)),
        ],
        out_specs=_single((n_loc, d), lambda b: (b, 0)),
        out_shape=jax.ShapeDtypeStruct((n, d), F32),
        scratch_shapes=[pltpu.VMEM((2, n_loc, n_loc), F32), pltpu.VMEM((npairs, n_loc, HEAD_PAIR), BF16)],
        compiler_params=_cparams("arbitrary"),
        name="neighbourhood_attention",
    )(qp, kp, vp, ckp, cvp, strip, x, mod, wo)


def _rpb_strip(rpb):
    qc = np.arange(GRID_W)[:, None]
    kc = np.arange(GRID_W)[None, :]
    col_start = np.clip(qc - WIN_C // 2, 0, GRID_W - WIN_C)
    in_win = (kc >= col_start) & (kc < col_start + WIN_C)
    dc = np.clip(kc - qc, -(WIN_C - 1), WIN_C - 1) + (WIN_C - 1)
    sel = (dc[:, :, None] == np.arange(2 * WIN_C - 1)).astype(np.float32)
    t = jnp.einsum('lhdj,qkj->lhqdk', rpb.astype(F32), sel, precision=lax.Precision.HIGHEST)
    t = jnp.where(in_win[None, None, :, None, :], t, -jnp.inf)
    n_layers, h, ndr = rpb.shape[:3]
    return t.reshape(n_layers, h, GRID_W, ndr * GRID_W)


def _ffn_kernel(xa_ref, xb_ref, mod_ref, g_ref, w1_ref, w3_ref, w2_ref, oa_ref, ob_ref, t_ref, *, tiles_a):
    def tile(x_ref, o_ref):
        x = x_ref[...]
        h = _norm_mod(x, g_ref[...], mod_ref[3:4, :], mod_ref[4:5, :]).astype(BF16)
        for c in range(w1_ref.shape[1] // MXU_DIM):
            sl = slice(c * MXU_DIM, (c + 1) * MXU_DIM)
            a = jnp.dot(h, w1_ref[:, sl], preferred_element_type=F32)
            b = jnp.dot(h, w3_ref[:, sl], preferred_element_type=F32)
            t_ref[:, sl] = (a * jax.nn.sigmoid(a) * b).astype(BF16)
        o_ref[...] = x + mod_ref[5:6, :] * jnp.dot(t_ref[...], w2_ref[...], preferred_element_type=F32)

    first_stream = pl.program_id(0) < tiles_a
    pl.when(first_stream)(functools.partial(tile, xa_ref, oa_ref))
    pl.when(jnp.logical_not(first_stream))(functools.partial(tile, xb_ref, ob_ref))


def _ffn(xa, xb, mod, group_of_tile_b, g, w1, w3, w2, layer, *, tm):
    d = xa.shape[1]
    d_ff = w1.shape[-1]
    tiles_a, tiles_b = xa.shape[0] // tm, xb.shape[0] // tm
    a_map = lambda i: (jnp.minimum(i, tiles_a - 1), 0)
    b_map = lambda i: (jnp.maximum(i - tiles_a, 0), 0)
    group = lambda i: jnp.where(i < tiles_a, 0, group_of_tile_b(jnp.maximum(i - tiles_a, 0)))
    return pl.pallas_call(
        functools.partial(_ffn_kernel, tiles_a=tiles_a),
        grid=(tiles_a + tiles_b,),
        in_specs=[
            pl.BlockSpec((tm, d), a_map),
            pl.BlockSpec((tm, d), b_map),
            pl.BlockSpec((None, N_MOD, d), lambda i: (group(i), 0, 0)),
            pl.BlockSpec((1, d), lambda i: (0, 0)),
            _single((None, d, d_ff), lambda i: (layer, 0, 0)),
            _single((None, d, d_ff), lambda i: (layer, 0, 0)),
            _single((None, d_ff, d), lambda i: (layer, 0, 0)),
        ],
        out_specs=[pl.BlockSpec((tm, d), a_map), pl.BlockSpec((tm, d), b_map)],
        out_shape=[jax.ShapeDtypeStruct(xa.shape, F32), jax.ShapeDtypeStruct(xb.shape, F32)],
        scratch_shapes=[pltpu.VMEM((tm, d_ff), BF16)],
        compiler_params=_cparams("arbitrary"),
        name="swiglu_ffn",
    )(xa, xb, mod, g, w1, w3, w2)


def _lane_half_swap(x):
    n = x.shape[-1]
    lane = lax.broadcasted_iota(jnp.int32, (1, n), 1)
    return jnp.where(lane % LANES < LANES // 2, pltpu.roll(x, n - LANES // 2, axis=1),
                     pltpu.roll(x, LANES // 2, axis=1))


def _quad_pairs(t0, t1, t2, t3):
    lo = lax.broadcasted_iota(jnp.int32, (1, t0.shape[-1]), 1) % LANES < LANES // 2
    s0, s1, s2, s3 = (_lane_half_swap(t) for t in (t0, t1, t2, t3))
    even = (jnp.where(lo, t0, s1), jnp.where(lo, t2, s3))
    odd = (jnp.where(lo, s0, t1), jnp.where(lo, s2, t3))
    return even, odd


def _ssm_kernel(x_ref, mod_ref, g_ref, dskip_ref, win_ref, wcar_ref, kin_ref, a4_ref, h0_ref, wglu_ref, *rest,
                seq, n_seg, with_state):
    if with_state:
        o_ref, sre_ref, sim_ref = rest[:3]
    else:
        o_ref = rest[0]
    u_ref, y_ref, z_ref, sp_ref, f_ref, e_ref = rest[-6:]
    d = x_ref.shape[2]
    cps = win_ref.shape[0]
    nsq = a4_ref.shape[-1] // 2
    nst = cps * nsq
    nq = seq // QUAD
    steps_per_dir = (d // Q_CH) // cps
    tb = 4 * QUAD
    qb = tb // QUAD * SUBLANES
    dk = pl.program_id(1)
    kb = (dk % steps_per_dir) * cps
    fwd = dk // steps_per_dir == 0

    @pl.when(dk == 0)
    def _prepare():
        gain, shift, scale = g_ref[...], mod_ref[0], mod_ref[1]

        def body(i, carry):
            t0 = pl.multiple_of(i * tb, tb)
            xt = jnp.swapaxes(x_ref[:, pl.ds(t0, tb), :], 0, 1)
            h = _norm_mod(xt, gain, shift, scale).reshape(tb // QUAD, QUAD, SUBLANES, d)
            rows = pl.ds(pl.multiple_of(i * qb, qb), qb)
            halves = _quad_pairs(*(h[:, s].reshape(qb, d) for s in range(QUAD)))
            for m in range(d // LANES):
                for parity, (v01, v23) in enumerate(halves):
                    k = 2 * m + parity
                    sl = slice(m * LANES, (m + 1) * LANES)
                    u_ref[k, rows, :LANES] = v01[:, sl].astype(BF16)
                    u_ref[k, rows, LANES:] = v23[:, sl].astype(BF16)
                    y_ref[k, rows, :LANES] = v01[:, sl] * dskip_ref[k, :, :LANES]
                    y_ref[k, rows, LANES:] = v23[:, sl] * dskip_ref[k, :, LANES:]
            return carry

        lax.fori_loop(0, seq // tb, body, 0)

    a_re = jnp.broadcast_to(jnp.concatenate([a4_ref[c][:, :nsq] for c in range(cps)], axis=-1), (SUBLANES, nst))
    a_im = jnp.broadcast_to(jnp.concatenate([a4_ref[c][:, nsq:] for c in range(cps)], axis=-1), (SUBLANES, nst))

    def chunk_cols(ref, c):
        return jnp.concatenate([ref[:, c * nsq:(c + 1) * nsq], ref[:, nst + c * nsq:nst + (c + 1) * nsq]], axis=-1)

    def scan(state, store, reverse):
        s_re, s_im = state
        pairs = range(nq // 2)
        for m in (reversed(pairs) if reverse else pairs):
            entering = {}
            for j in ((2 * m + 1, 2 * m) if reverse else (2 * m, 2 * m + 1)):
                rsl = slice(j * SUBLANES, (j + 1) * SUBLANES)
                entering[j] = (s_re, s_im)
                s_re, s_im = (a_re * s_re - a_im * s_im + z_ref[rsl, :nst],
                              a_re * s_im + a_im * s_re + z_ref[rsl, nst:])
            if store:
                psl = slice(2 * m * SUBLANES, (2 * m + 2) * SUBLANES)
                lo, hi = entering[2 * m], entering[2 * m + 1]
                sp_ref[psl, :nst] = jnp.concatenate([lo[0], hi[0]], axis=0).astype(BF16)
                sp_ref[psl, nst:] = jnp.concatenate([lo[1], hi[1]], axis=0).astype(BF16)
        return s_re, s_im

    def run_direction(reverse):
        for c in range(cps):
            z = jnp.dot(u_ref[kb + c], win_ref[c], preferred_element_type=F32)
            z_ref[:, c * nsq:(c + 1) * nsq] = z[:, :nsq]
            z_ref[:, nst + c * nsq:nst + (c + 1) * nsq] = z[:, nsq:]
        start = (jnp.concatenate([h0_ref[c][:, :nsq] for c in range(cps)], axis=-1),
                 jnp.concatenate([h0_ref[c][:, nsq:] for c in range(cps)], axis=-1))
        if n_seg > 1:
            zero = jnp.zeros((SUBLANES, nst), F32)
            e_re, e_im = scan((zero, zero), False, reverse)
            e_ref[:, :nst] = e_re
            e_ref[:, nst:] = e_im
            f_ref[:, :nst] = start[0]
            f_ref[:, nst:] = start[1]
            p_re, p_im = a_re[:1], a_im[:1]
            for _ in range(int(math.log2(nq))):
                p_re, p_im = p_re * p_re - p_im * p_im, 2.0 * p_re * p_im
            for b in range(SUBLANES // n_seg):
                for s in (range(n_seg - 2, -1, -1) if reverse else range(1, n_seg)):
                    dst = b * n_seg + s
                    src = dst + 1 if reverse else dst - 1
                    f_re, f_im = f_ref[src:src + 1, :nst], f_ref[src:src + 1, nst:]
                    f_ref[dst:dst + 1, :nst] = p_re * f_re - p_im * f_im + e_ref[src:src + 1, :nst]
                    f_ref[dst:dst + 1, nst:] = p_re * f_im + p_im * f_re + e_ref[src:src + 1, nst:]
            start = (f_ref[:, :nst], f_ref[:, nst:])
        fin_re, fin_im = scan(start, True, reverse)
        if with_state:
            for c in range(cps):
                sre_ref[c] = fin_re[:, c * nsq:(c + 1) * nsq]
                sim_ref[c] = fin_im[:, c * nsq:(c + 1) * nsq]
        for c in range(cps):
            y_ref[kb + c] = (y_ref[kb + c]
                             + _dot_nt(chunk_cols(sp_ref, c), wcar_ref[c])
                             + jnp.dot(u_ref[kb + c], kin_ref[c], preferred_element_type=F32))

    pl.when(fwd)(functools.partial(run_direction, False))
    pl.when(jnp.logical_not(fwd))(functools.partial(run_direction, True))

    @pl.when(dk == 2 * steps_per_dir - 1)
    def _finish():
        gate = mod_ref[2]
        fqd = min(FINISH_QUADS, nq // 8)
        fq = fqd * SUBLANES
        ft = fqd * QUAD
        n_blk = nq // fqd

        def glu_in(i, b):
            rows = pl.ds(pl.multiple_of(i * fq, fq), fq)
            gathered = [jnp.concatenate([y_ref[2 * m + parity, rows, half] for m in range(d // LANES)], axis=-1)
                        for half in (slice(0, LANES), slice(LANES, 2 * LANES)) for parity in (0, 1)]
            (y0, y2), (y1, y3) = _quad_pairs(*gathered)
            y = jnp.stack([t.reshape(fqd, SUBLANES, d) for t in (y0, y1, y2, y3)], axis=1)
            z = jax.nn.gelu(y.reshape(ft * SUBLANES, d)).astype(BF16)
            z_ref[b * ft * SUBLANES:(b + 1) * ft * SUBLANES, :] = jnp.dot(z, wglu_ref[...],
                                                                        preferred_element_type=F32)

        def glu_out(i, b):
            gl = z_ref[b * ft * SUBLANES:(b + 1) * ft * SUBLANES, :]
            out = (gl[:, :d] * jax.nn.sigmoid(gl[:, d:])).reshape(ft, SUBLANES, d) * gate
            for j in range(ft // tb):
                t0 = pl.multiple_of(i * ft + j * tb, tb)
                upd = jnp.swapaxes(out[j * tb:(j + 1) * tb], 0, 1)
                o_ref[:, pl.ds(t0, tb), :] = x_ref[:, pl.ds(t0, tb), :] + upd

        def pair(jj, carry):
            i = 2 * jj
            glu_in(i + 1, 1)
            glu_out(i, 0)
            glu_in(i + 2, 0)
            glu_out(i + 1, 1)
            return carry

        glu_in(0, 0)
        lax.fori_loop(0, n_blk // 2 - 1, pair, 0)
        glu_in(n_blk - 1, 1)
        glu_out(n_blk - 2, 0)
        glu_out(n_blk - 1, 1)


def _ssm_mixer(x, mod8, g, dskip, w_in, w_car, k_in, a4, h0, wglu, layer, *, seq, n_seg, with_state):
    n_seq, _, d = x.shape
    n_groups = n_seq // SUBLANES
    n_chunks = d // Q_CH
    nsq = a4.shape[-1] // 2
    cps = CHUNKS_PER_STEP
    steps_per_dir = n_chunks // cps
    rows_q = seq // QUAD * SUBLANES
    assert cps * nsq == d and seq % (32 * QUAD) == 0
    dk_map = lambda gi, dk: (layer, dk // steps_per_dir, dk % steps_per_dir, 0, 0)
    seq_map = lambda gi, dk: (dk // steps_per_dir, dk % steps_per_dir, gi, 0)
    x_spec = _single((SUBLANES, seq, d), lambda gi, dk: (gi, 0, 0))
    out_specs = [x_spec]
    out_shape = [jax.ShapeDtypeStruct(x.shape, F32)]
    if with_state:
        st_spec = pl.BlockSpec((None, cps, SUBLANES, nsq), seq_map)
        out_specs += [st_spec, st_spec]
        out_shape += [jax.ShapeDtypeStruct((2, n_chunks, n_seq, nsq), F32)] * 2
    return pl.pallas_call(
        functools.partial(_ssm_kernel, seq=seq, n_seg=n_seg, with_state=with_state),
        grid=(n_groups, 2 * steps_per_dir),
        in_specs=[
            pl.BlockSpec((SUBLANES, seq, d), lambda gi, dk: (gi, 0, 0)),
            pl.BlockSpec((N_MOD, SUBLANES, d), lambda gi, dk: (0, 0, 0)),
            pl.BlockSpec((1, d), lambda gi, dk: (0, 0)),
            pl.BlockSpec((n_chunks, 1, MXU_DIM), lambda gi, dk: (0, 0, 0)),
            pl.BlockSpec((None, None, cps) + w_in.shape[-2:], dk_map),
            pl.BlockSpec((None, None, cps) + w_car.shape[-2:], dk_map),
            pl.BlockSpec((None, None, cps, MXU_DIM, MXU_DIM), dk_map),
            pl.BlockSpec((None, None, cps, 1, 2 * nsq), dk_map),
            pl.BlockSpec((None, cps, SUBLANES, 2 * nsq), seq_map),
            _single((None, d, 2 * d), lambda gi, dk: (layer, 0, 0)),
        ],
        out_specs=out_specs,
        out_shape=out_shape,
        scratch_shapes=[
            pltpu.VMEM((n_chunks, rows_q, MXU_DIM), BF16),
            pltpu.VMEM((n_chunks, rows_q, MXU_DIM), F32),
            pltpu.VMEM((rows_q, 2 * cps * nsq), F32),
            pltpu.VMEM((rows_q, 2 * cps * nsq), BF16),
            pltpu.VMEM((SUBLANES, 2 * cps * nsq), F32),
            pltpu.VMEM((SUBLANES, 2 * cps * nsq), F32),
        ],
        compiler_params=_cparams("arbitrary", "arbitrary"),
        name="s5_mixer",
    )(x, mod8, g, dskip, w_in, w_car, k_in, a4, h0, wglu)


def _ssm_params(lam_re, lam_im, log_step, b_re, b_im, c_re, c_im):
    lam_re, lam_im = lam_re.astype(F32), lam_im.astype(F32)
    step = jnp.exp(log_step.astype(F32))[..., None]
    mag = jnp.exp(lam_re * step)
    a_re, a_im = mag * jnp.cos(lam_im * step), mag * jnp.sin(lam_im * step)
    den = lam_re * lam_re + lam_im * lam_im
    nr, ni = a_re - 1.0, a_im
    f_re = (nr * lam_re + ni * lam_im) / den
    f_im = (ni * lam_re - nr * lam_im) / den
    bt_re, bt_im = jnp.swapaxes(b_re.astype(F32), -1, -2), jnp.swapaxes(b_im.astype(F32), -1, -2)
    bb = (f_re[..., None, :] * bt_re - f_im[..., None, :] * bt_im,
          f_re[..., None, :] * bt_im + f_im[..., None, :] * bt_re)
    cc = (c_re.astype(F32), c_im.astype(F32))

    def cmul(x, y):
        return x[0] * y[0] - x[1] * y[1], x[0] * y[1] + x[1] * y[0]

    pw = [(jnp.ones_like(a_re), jnp.zeros_like(a_re))]
    for _ in range(QUAD):
        pw.append(cmul(pw[-1], (a_re, a_im)))

    def by_dir(n_fwd, n_bwd):
        return tuple(jnp.stack([pw[n_fwd][i][:, 0], pw[n_bwd][i][:, 1]], axis=1)[..., None, :] for i in range(2))

    n_layers, _, n_groups, c, p = bb[0].shape
    gq = Q_CH // c
    kk = n_groups // gq
    nsq = gq * p
    lead = (n_layers, 2, kk)

    def pack(ws, sign):
        def rows(i):
            t = jnp.stack([w[i] for w in ws], axis=2).reshape(n_layers, 2, QUAD, kk, gq, c, p)
            return t.transpose(0, 1, 3, 2, 4, 5, 6).reshape(*lead, MXU_DIM, p)
        return jnp.concatenate([rows(0), sign * rows(1)], axis=-1).astype(BF16)

    w_in = pack([cmul(by_dir(QUAD - 1 - s, s), bb) for s in range(QUAD)], 1.0)
    w_car = pack([cmul(cc, by_dir(s + 1, QUAD - s)) for s in range(QUAD)], -1.0)

    gc = gq * c
    same_g = (np.arange(gc) // c)[:, None] == (np.arange(gc) // c)[None, :]
    cb = [cmul(cc, tuple(t[..., None, :] for t in pw[n])) for n in range(QUAD)]
    cb = jnp.concatenate([jnp.concatenate([t[0] for t in cb], axis=-2),
                          -jnp.concatenate([t[1] for t in cb], axis=-2)], axis=-1)
    taps = jnp.einsum('ldgkp,ldgcp->ldgck', cb, jnp.concatenate(bb, axis=-1), precision=lax.Precision.HIGHEST)
    tiles = []
    for n in range(QUAD):
        tap = taps[..., n * c:(n + 1) * c]
        tiles.append(jnp.where(same_g, jnp.tile(tap.reshape(*lead, gc, c), (1, 1, 1, 1, gq)), 0.0))
    zero = jnp.zeros_like(tiles[0][:, 0])

    def tile_of(d, s, t):
        n = t - s if d == 0 else s - t
        return tiles[n][:, d] if n >= 0 else zero

    k_in = jnp.stack([jnp.concatenate([jnp.concatenate([tile_of(d, s, t) for t in range(QUAD)], axis=-1)
                                       for s in range(QUAD)], axis=-2) for d in range(2)], axis=1)

    a4 = jnp.concatenate([pw[QUAD][0].reshape(*lead, 1, nsq), pw[QUAD][1].reshape(*lead, 1, nsq)], axis=-1)
    return w_in.astype(BF16), w_car.astype(BF16), k_in.astype(BF16), a4


def _block_diag_kernel(win_ref, wcar_ref, win_o_ref, wcar_o_ref, *, gq):
    pp = win_ref.shape[-1] // 2
    nsq = gq * pp
    src = lax.broadcasted_iota(jnp.int32, (2 * pp, 2 * nsq), 0)
    col = lax.broadcasted_iota(jnp.int32, (2 * pp, 2 * nsq), 1)
    spread = ((col % pp == src % pp) & ((col >= nsq) == (src >= pp))).astype(BF16)
    row_g = (lax.broadcasted_iota(jnp.int32, (MXU_DIM, 2 * nsq), 0) // (Q_CH // gq)) % gq
    col_g = (lax.broadcasted_iota(jnp.int32, (MXU_DIM, 2 * nsq), 1) % nsq) // pp
    for src_ref, dst_ref in ((win_ref, win_o_ref), (wcar_ref, wcar_o_ref)):
        for c in range(src_ref.shape[0]):
            w = jnp.dot(src_ref[c], spread, preferred_element_type=F32)
            dst_ref[c] = jnp.where(row_g == col_g, w, 0.0).astype(BF16)


def _block_diag(w_in, w_car, *, gq):
    n_layers, n_dir, kk, rows, two_p = w_in.shape
    cps = CHUNKS_PER_STEP
    spec_in = pl.BlockSpec((None, None, cps, rows, two_p), lambda l, dr, k: (l, dr, k, 0, 0))
    spec_out = pl.BlockSpec((None, None, cps, rows, gq * two_p), lambda l, dr, k: (l, dr, k, 0, 0))
    out = jax.ShapeDtypeStruct((n_layers, n_dir, kk, rows, gq * two_p), BF16)
    return pl.pallas_call(
        functools.partial(_block_diag_kernel, gq=gq),
        grid=(n_layers, n_dir, kk // cps),
        in_specs=[spec_in, spec_in],
        out_specs=[spec_out, spec_out],
        out_shape=[out, out],
        compiler_params=_cparams("arbitrary", "arbitrary", "arbitrary"),
        name="s5_block_diag",
    )(w_in, w_car)


def kernel(x_prompt, x_sample, cache_k, cache_v, state_ssm_re, state_ssm_im, c, c_ctx, norm_mix, norm_ffn, ada_w, ada_b, na_w_qkv, na_w_o, na_q_gain, na_k_gain, na_rpb, ssm_lambda_re, ssm_lambda_im, ssm_log_step, ssm_b_re, ssm_b_im, ssm_c_re, ssm_c_im, ssm_d, ssm_w_glu, ffn_w1, ffn_w3, ffn_w2):
    batch, seq, d = x_prompt.shape
    dec_batch, dec_seq, _ = x_sample.shape
    depth = ada_w.shape[0]
    heads, head_dim = cache_k.shape[3], cache_k.shape[4]
    past = cache_k.shape[2]
    npairs = d // HEAD_PAIR
    d_ff = ffn_w1.shape[-1]
    n_dir = state_ssm_re.shape[2]
    n_seg = dec_seq // seq
    assert seq % (FINISH_QUADS * QUAD) == 0 and dec_seq % seq == 0 and SUBLANES % n_seg == 0
    assert dec_batch * n_seg == SUBLANES and batch % SUBLANES == 0 and seq & (seq - 1) == 0
    assert d_ff % MXU_DIM == 0 and d % (Q_CH * CHUNKS_PER_STEP) == 0 and head_dim * 2 == HEAD_PAIR
    assert Q_CH * 2 == LANES and Q_CH % SSM_GROUP == 0

    cond8 = jnp.zeros((SUBLANES, d), F32).at[0].set(c_ctx).at[1:1 + dec_batch].set(c)
    mod = _modulation(cond8, ada_w, ada_b).reshape(depth, SUBLANES, N_MOD, d)

    tm = math.gcd(512, dec_seq)
    prompt_group = lambda i: 0
    sample_group = lambda i: 1 + (i * tm) // dec_seq

    head_mean = jnp.kron(jnp.eye(MXU_DIM // head_dim, dtype=F32),
                         jnp.full((head_dim, head_dim), 1.0 / head_dim, F32)).astype(BF16)

    w_qkv, w_o, w_glu = na_w_qkv.astype(BF16), na_w_o.astype(BF16), ssm_w_glu.astype(BF16)
    w1, w3, w2 = ffn_w1.astype(BF16), ffn_w3.astype(BF16), ffn_w2.astype(BF16)
    n_na = cache_k.shape[1]
    ckp = cache_k.astype(BF16).reshape(dec_batch, n_na, past, npairs, HEAD_PAIR).transpose(1, 0, 3, 2, 4)
    cvp = cache_v.astype(BF16).reshape(dec_batch, n_na, past, npairs, HEAD_PAIR).transpose(1, 0, 3, 2, 4)
    strip = _rpb_strip(na_rpb)
    w_in, w_car, k_in, a4 = _ssm_params(ssm_lambda_re, ssm_lambda_im, ssm_log_step, ssm_b_re, ssm_b_im,
                                        ssm_c_re, ssm_c_im)
    s5_w = (*_block_diag(w_in, w_car, gq=Q_CH // SSM_GROUP), k_in, a4)

    xp = x_prompt.reshape(batch * seq, d)
    xs = x_sample.reshape(dec_batch * dec_seq, d)
    new_cache, new_sre, new_sim = None, [], []
    for i in range(depth):
        j = i // 2
        g_mix = norm_mix[i].reshape(1, d)
        if i % 2 == 0:
            q_gain = jnp.tile(na_q_gain[j], heads).reshape(1, d)
            k_gain = jnp.tile(na_k_gain[j], heads).reshape(1, d)
            qp, kp, vp, *new_cache = _qkv(xp, mod[i], prompt_group, g_mix, w_qkv, j, n_na, q_gain, k_gain,
                                          head_mean, new_cache, tm=tm, head_dim=head_dim, seq=seq, with_cache=True)
            xp = _ctx_attention(qp, kp, vp, xp, mod[i], w_o, j, seq=seq, seqs_per_step=tm // seq)
            qs, ks, vs = _qkv(xs, mod[i], sample_group, g_mix, w_qkv, j, n_na, q_gain, k_gain, head_mean, None,
                              tm=tm, head_dim=head_dim, seq=seq, with_cache=False)
            xs = _na_attention(qs, ks, vs, ckp, cvp, strip, xs, mod[i], w_o, j, n_batch=dec_batch)
        else:
            n_chunks = d // Q_CH
            dskip = jnp.tile(ssm_d[j].astype(F32).reshape(n_chunks, 1, Q_CH), (1, 1, QUAD))
            ns = a4.shape[-1] // 2
            mod_p = jnp.broadcast_to(mod[i, 0][:, None, :], (N_MOD, SUBLANES, d))
            h0_p = jnp.zeros((n_dir, n_chunks, batch, 2 * ns), F32)
            xp, sre, sim = _ssm_mixer(xp.reshape(batch, seq, d), mod_p, g_mix, dskip, *s5_w, h0_p, w_glu, j,
                                      seq=seq, n_seg=1, with_state=True)
            xp = xp.reshape(batch * seq, d)
            new_sre.append(sre.transpose(2, 0, 1, 3).reshape(batch, n_dir, d // SSM_GROUP, SSM_STATE))
            new_sim.append(sim.transpose(2, 0, 1, 3).reshape(batch, n_dir, d // SSM_GROUP, SSM_STATE))
            mod_s = jnp.repeat(mod[i, 1:1 + dec_batch], n_seg, axis=0).transpose(1, 0, 2)
            s_re = state_ssm_re[:, j].astype(F32).reshape(dec_batch, n_dir, n_chunks, ns)
            s_im = state_ssm_im[:, j].astype(F32).reshape(dec_batch, n_dir, n_chunks, ns)
            s0 = jnp.concatenate([s_re, s_im], axis=-1)
            h0_s = jnp.zeros((dec_batch, n_seg, n_dir, n_chunks, 2 * ns), F32)
            h0_s = h0_s.at[:, 0, 0].set(s0[:, 0]).at[:, n_seg - 1, 1].set(s0[:, 1])
            h0_s = h0_s.reshape(dec_batch * n_seg, n_dir, n_chunks, 2 * ns).transpose(1, 2, 0, 3)
            (xs,) = _ssm_mixer(xs.reshape(dec_batch * n_seg, seq, d), mod_s, g_mix, dskip, *s5_w, h0_s, w_glu, j,
                               seq=seq, n_seg=n_seg, with_state=False)
            xs = xs.reshape(dec_batch * dec_seq, d)
        g_ffn = norm_ffn[i].reshape(1, d)
        xp, xs = _ffn(xp, xs, mod[i], sample_group, g_ffn, w1, w3, w2, i, tm=tm)
    new_k, new_v = (t.reshape(batch, n_na, heads, head_dim, seq).transpose(0, 1, 4, 2, 3) for t in new_cache)
    return (xp.reshape(batch, seq, d), xs.reshape(dec_batch, dec_seq, d), new_k, new_v,
            jnp.stack(new_sre, axis=1), jnp.stack(new_sim, axis=1))
```

```python
import functools
import math

import jax
import jax.numpy as jnp
import numpy as np
from jax import lax
from jax.experimental import pallas as pl
from jax.experimental.pallas import tpu as pltpu

F32 = jnp.float32
BF16 = jnp.bfloat16

EPS = 1e-6
N_MOD = 6
GRID_W = 64
WIN_R = 8
WIN_C = 16
SSM_GROUP = 16
SSM_STATE = 64

SUBLANES = 8
LANES = 128
MXU_DIM = 256
VMEM_LIMIT_BYTES = 56 * 1024 * 1024

HEAD_PAIR = LANES
QUAD = 4
Q_CH = MXU_DIM // QUAD
CHUNKS_PER_STEP = 4
FINISH_QUADS = 8


def _cparams(*sem):
    return pltpu.CompilerParams(dimension_semantics=sem, vmem_limit_bytes=VMEM_LIMIT_BYTES)


def _single(block_shape, index_map):
    return pl.BlockSpec(block_shape, index_map, pipeline_mode=pl.Buffered(1))


def _norm_mod(x, g, shift, scale):
    ms = jnp.mean(x * x, axis=-1, keepdims=True)
    y = x * lax.rsqrt(ms + EPS) * g
    return y * (1.0 + scale) + shift


def _mod_kernel(cond_ref, w_ref, b_ref, o_ref):
    c = cond_ref[...]
    a = (c * jax.nn.sigmoid(c)).astype(BF16)
    o_ref[...] = jnp.dot(a, w_ref[...].astype(BF16), preferred_element_type=F32) + b_ref[...]


def _modulation(cond8, ada_w, ada_b):
    depth, d, n = ada_w.shape
    tn = n // 4
    return pl.pallas_call(
        _mod_kernel,
        grid=(depth, n // tn),
        in_specs=[
            pl.BlockSpec((SUBLANES, d), lambda i, j: (0, 0)),
            pl.BlockSpec((None, d, tn), lambda i, j: (i, 0, j)),
            pl.BlockSpec((None, 1, tn), lambda i, j: (i, 0, j)),
        ],
        out_specs=pl.BlockSpec((None, SUBLANES, tn), lambda i, j: (i, 0, j)),
        out_shape=jax.ShapeDtypeStruct((depth, SUBLANES, n), F32),
        compiler_params=_cparams("arbitrary", "arbitrary"),
        name="adaln_modulation",
    )(cond8, ada_w, ada_b.reshape(depth, 1, n))


def _qkv_kernel(x_ref, mod_ref, g_ref, w_ref, qg_ref, kg_ref, hm_ref, *rest, d, attn_scale, layer, with_cache):
    out_refs = rest[-5:] if with_cache else rest
    qp_ref, kp_ref, vp_ref = out_refs[:3]
    h = _norm_mod(x_ref[...], g_ref[...], mod_ref[0:1, :], mod_ref[1:2, :]).astype(BF16)
    qkv = jnp.dot(h, w_ref[...], preferred_element_type=F32)
    q, k, v = qkv[:, :d], qkv[:, d:2 * d], qkv[:, 2 * d:]

    def head_norm(t, gain):
        parts = []
        for c in range(d // MXU_DIM):
            tc = t[:, c * MXU_DIM:(c + 1) * MXU_DIM]
            ms = jnp.dot((tc * tc).astype(BF16), hm_ref[...], preferred_element_type=F32)
            parts.append(tc * lax.rsqrt(ms + EPS))
        return jnp.concatenate(parts, axis=-1) * gain

    q = head_norm(q, qg_ref[...])
    k = head_norm(k, kg_ref[...])
    if with_cache:
        for ref, val in ((out_refs[3], k), (out_refs[4], v)):
            seq = ref.shape[-1]
            for s in range(ref.shape[0]):
                vt = val[s * seq:(s + 1) * seq, :].T
                if ref.ndim == 3:
                    ref[s] = vt
                else:
                    for l in range(ref.shape[1]):
                        ref[s, l] = vt if l == layer else jnp.zeros_like(vt)
    qs = (q * attn_scale).astype(BF16)
    kb = k.astype(BF16)
    vb = v.astype(BF16)
    for p in range(d // HEAD_PAIR):
        sl = slice(p * HEAD_PAIR, (p + 1) * HEAD_PAIR)
        qp_ref[p] = qs[:, sl]
        kp_ref[p] = kb[:, sl]
        vp_ref[p] = vb[:, sl]


def _qkv(x, mod, group_of_tile, g, w_qkv, layer, n_layers, q_gain, k_gain, head_mean, cache, *,
         tm, head_dim, seq, with_cache):
    n, d = x.shape
    npairs = d // HEAD_PAIR
    pair_spec = pl.BlockSpec((npairs, tm, HEAD_PAIR), lambda i: (0, i, 0))
    out_specs = [pair_spec] * 3
    out_shape = [jax.ShapeDtypeStruct((npairs, n, HEAD_PAIR), BF16)] * 3
    in_specs = [
        pl.BlockSpec((tm, d), lambda i: (i, 0)),
        pl.BlockSpec((None, N_MOD, d), lambda i: (group_of_tile(i), 0, 0)),
        pl.BlockSpec((1, d), lambda i: (0, 0)),
        _single((None, d, 3 * d), lambda i: (layer, 0, 0)),
        pl.BlockSpec((1, d), lambda i: (0, 0)),
        pl.BlockSpec((1, d), lambda i: (0, 0)),
        pl.BlockSpec((MXU_DIM, MXU_DIM), lambda i: (0, 0)),
    ]
    args = [x, mod, g, w_qkv, q_gain, k_gain, head_mean]
    aliases = {}
    if with_cache:
        out_shape += [jax.ShapeDtypeStruct((n // seq, n_layers, d, seq), F32)] * 2
        if cache is None:
            out_specs += [pl.BlockSpec((tm // seq, n_layers, d, seq), lambda i: (i, 0, 0, 0))] * 2
        else:
            out_specs += [pl.BlockSpec((tm // seq, None, d, seq), lambda i: (i, layer, 0, 0))] * 2
            aliases = {len(args): 3, len(args) + 1: 4}
            in_specs += [pl.BlockSpec(memory_space=pl.ANY)] * 2
            args += list(cache)
    return pl.pallas_call(
        functools.partial(_qkv_kernel, d=d, attn_scale=head_dim ** -0.5, layer=layer, with_cache=with_cache),
        grid=(n // tm,),
        in_specs=in_specs,
        out_specs=out_specs,
        out_shape=out_shape,
        input_output_aliases=aliases,
        compiler_params=_cparams("arbitrary"),
        name="norm_qkv",
    )(*args)


def _pair_masks():
    lane = lax.broadcasted_iota(jnp.int32, (1, HEAD_PAIR), 1)
    first = lane < HEAD_PAIR // 2
    return first, jnp.logical_not(first)


def _dot_nt(a, b):
    return lax.dot_general(a, b, (((1,), (1,)), ((), ())), preferred_element_type=F32)


def _ctx_attn_kernel(q_ref, k_ref, v_ref, x_ref, mod_ref, wo_ref, o_ref, att_ref, *, npairs, seq):
    first, second = _pair_masks()
    for sq in range(x_ref.shape[0] // seq):
        rsl = slice(sq * seq, (sq + 1) * seq)
        for p in range(npairs):
            qp, kp, vp = q_ref[p, rsl, :], k_ref[p, rsl, :], v_ref[p, rsl, :]
            outs = []
            for msk in (first, second):
                qh = jnp.where(msk, qp, jnp.zeros_like(qp))
                s = _dot_nt(qh, kp)
                m = jnp.max(s, axis=-1, keepdims=True)
                e = jnp.exp(s - m)
                l = jnp.sum(e, axis=-1, keepdims=True)
                outs.append(jnp.dot(e.astype(BF16), vp, preferred_element_type=F32) / l)
            att_ref[rsl, p * HEAD_PAIR:(p + 1) * HEAD_PAIR] = jnp.where(first, outs[0], outs[1]).astype(BF16)
    o = jnp.dot(att_ref[...], wo_ref[...], preferred_element_type=F32)
    o_ref[...] = x_ref[...] + mod_ref[2:3, :] * o


def _ctx_attention(qp, kp, vp, x, mod, wo, layer, *, seq, seqs_per_step):
    n, d = x.shape
    npairs = d // HEAD_PAIR
    rows = seq * seqs_per_step
    pair_spec = pl.BlockSpec((npairs, rows, HEAD_PAIR), lambda b: (0, b, 0))
    return pl.pallas_call(
        functools.partial(_ctx_attn_kernel, npairs=npairs, seq=seq),
        grid=(n // rows,),
        in_specs=[
            pair_spec, pair_spec, pair_spec,
            pl.BlockSpec((rows, d), lambda b: (b, 0)),
            pl.BlockSpec((None, N_MOD, d), lambda b: (0, 0, 0)),
            _single((None, d, d), lambda b: (layer, 0, 0)),
        ],
        out_specs=pl.BlockSpec((rows, d), lambda b: (b, 0)),
        out_shape=jax.ShapeDtypeStruct((n, d), F32),
        scratch_shapes=[pltpu.VMEM((rows, d), BF16)],
        compiler_params=_cparams("arbitrary"),
        name="context_attention",
    )(qp, kp, vp, x, mod, wo)


def _na_attn_kernel(q_ref, k_ref, v_ref, ck_ref, cv_ref, strip_ref, x_ref, mod_ref, wo_ref, o_ref,
                    bias_ref, att_ref, *, npairs, rows, q_tile):
    first, second = _pair_masks()
    kr = min(WIN_R, rows)
    n_loc = rows * GRID_W
    tile_rows = q_tile // GRID_W
    win_rows = min(kr + tile_rows, rows)

    bias_ref[...] = jnp.full(bias_ref.shape, -jnp.inf, F32)

    def pair_body(p, carry):
        for hh in range(2):
            for qr in range(rows):
                rs = min(max(qr - kr // 2, 0), rows - kr)
                off = rs - qr + (WIN_R - 1)
                bias_ref[hh, qr * GRID_W:(qr + 1) * GRID_W, rs * GRID_W:(rs + kr) * GRID_W] = (
                    strip_ref[2 * p + hh, :, off * GRID_W:(off + kr) * GRID_W])
        ckp, cvp = ck_ref[p], cv_ref[p]
        for qt in range(n_loc // q_tile):
            rsl = slice(qt * q_tile, (qt + 1) * q_tile)
            ks = min(max(qt * tile_rows - kr // 2, 0), rows - win_rows)
            ksl = slice(ks * GRID_W, (ks + win_rows) * GRID_W)
            qp, kp, vp = q_ref[p, rsl, :], k_ref[p, ksl, :], v_ref[p, ksl, :]
            outs = []
            for hh, msk in enumerate((first, second)):
                qh = jnp.where(msk, qp, jnp.zeros_like(qp))
                s_loc = _dot_nt(qh, kp) + bias_ref[hh, rsl, ksl]
                s_ctx = _dot_nt(qh, ckp)
                m = jnp.maximum(jnp.max(s_loc, axis=-1, keepdims=True), jnp.max(s_ctx, axis=-1, keepdims=True))
                e_loc = jnp.exp(s_loc - m)
                e_ctx = jnp.exp(s_ctx - m)
                l = jnp.sum(e_loc, axis=-1, keepdims=True) + jnp.sum(e_ctx, axis=-1, keepdims=True)
                o = (jnp.dot(e_loc.astype(BF16), vp, preferred_element_type=F32)
                     + jnp.dot(e_ctx.astype(BF16), cvp, preferred_element_type=F32))
                outs.append(o / l)
            att_ref[p, rsl, :] = jnp.where(first, outs[0], outs[1]).astype(BF16)
        return carry

    lax.fori_loop(0, npairs, pair_body, 0)
    att = jnp.concatenate([att_ref[p] for p in range(npairs)], axis=-1)
    o = jnp.dot(att, wo_ref[...], preferred_element_type=F32)
    o_ref[...] = x_ref[...] + mod_ref[2:3, :] * o


def _na_attention(qp, kp, vp, ckp, cvp, strip, x, mod, wo, layer, *, n_batch):
    n, d = x.shape
    npairs = d // HEAD_PAIR
    n_loc = n // n_batch
    past = ckp.shape[3]
    rows = n_loc // GRID_W
    pair_spec = _single((npairs, n_loc, HEAD_PAIR), lambda b: (0, b, 0))
    ctx_spec = _single((None, None, npairs, past, HEAD_PAIR), lambda b: (layer, b, 0, 0, 0))
    return pl.pallas_call(
        functools.partial(_na_attn_kernel, npairs=npairs, rows=rows, q_tile=256),
        grid=(n_batch,),
        in_specs=[
            pair_spec, pair_spec, pair_spec, ctx_spec, ctx_spec,
            _single((None,) + strip.shape[1:], lambda b: (layer, 0, 0, 0)),
            _single((n_loc, d), lambda b: (b, 0)),
            pl.BlockSpec((None, N_MOD, d), lambda b: (1 + b, 0, 0)),
            _single((None, d, d), lambda b: (layer, 0, 0)),
        ],
        out_specs=_single((n_loc, d), lambda b: (b, 0)),
        out_shape=jax.ShapeDtypeStruct((n, d), F32),
        scratch_shapes=[pltpu.VMEM((2, n_loc, n_loc), F32), pltpu.VMEM((npairs, n_loc, HEAD_PAIR), BF16)],
        compiler_params=_cparams("arbitrary"),
        name="neighbourhood_attention",
    )(qp, kp, vp, ckp, cvp, strip, x, mod, wo)


def _rpb_strip(rpb):
    qc = np.arange(GRID_W)[:, None]
    kc = np.arange(GRID_W)[None, :]
    col_start = np.clip(qc - WIN_C // 2, 0, GRID_W - WIN_C)
    in_win = (kc >= col_start) & (kc < col_start + WIN_C)
    dc = np.clip(kc - qc, -(WIN_C - 1), WIN_C - 1) + (WIN_C - 1)
    sel = (dc[:, :, None] == np.arange(2 * WIN_C - 1)).astype(np.float32)
    t = jnp.einsum('lhdj,qkj->lhqdk', rpb.astype(F32), sel, precision=lax.Precision.HIGHEST)
    t = jnp.where(in_win[None, None, :, None, :], t, -jnp.inf)
    n_layers, h, ndr = rpb.shape[:3]
    return t.reshape(n_layers, h, GRID_W, ndr * GRID_W)


def _ffn_kernel(xa_ref, xb_ref, mod_ref, g_ref, w1_ref, w3_ref, w2_ref, oa_ref, ob_ref, t_ref, *, tiles_a):
    def tile(x_ref, o_ref):
        x = x_ref[...]
        h = _norm_mod(x, g_ref[...], mod_ref[3:4, :], mod_ref[4:5, :]).astype(BF16)
        for c in range(w1_ref.shape[1] // MXU_DIM):
            sl = slice(c * MXU_DIM, (c + 1) * MXU_DIM)
            a = jnp.dot(h, w1_ref[:, sl], preferred_element_type=F32)
            b = jnp.dot(h, w3_ref[:, sl], preferred_element_type=F32)
            t_ref[:, sl] = (a * jax.nn.sigmoid(a) * b).astype(BF16)
        o_ref[...] = x + mod_ref[5:6, :] * jnp.dot(t_ref[...], w2_ref[...], preferred_element_type=F32)

    first_stream = pl.program_id(0) < tiles_a
    pl.when(first_stream)(functools.partial(tile, xa_ref, oa_ref))
    pl.when(jnp.logical_not(first_stream))(functools.partial(tile, xb_ref, ob_ref))


def _ffn(xa, xb, mod, group_of_tile_b, g, w1, w3, w2, layer, *, tm):
    d = xa.shape[1]
    d_ff = w1.shape[-1]
    tiles_a, tiles_b = xa.shape[0] // tm, xb.shape[0] // tm
    a_map = lambda i: (jnp.minimum(i, tiles_a - 1), 0)
    b_map = lambda i: (jnp.maximum(i - tiles_a, 0), 0)
    group = lambda i: jnp.where(i < tiles_a, 0, group_of_tile_b(jnp.maximum(i - tiles_a, 0)))
    return pl.pallas_call(
        functools.partial(_ffn_kernel, tiles_a=tiles_a),
        grid=(tiles_a + tiles_b,),
        in_specs=[
            pl.BlockSpec((tm, d), a_map),
            pl.BlockSpec((tm, d), b_map),
            pl.BlockSpec((None, N_MOD, d), lambda i: (group(i), 0, 0)),
            pl.BlockSpec((1, d), lambda i: (0, 0)),
            _single((None, d, d_ff), lambda i: (layer, 0, 0)),
            _single((None, d, d_ff), lambda i: (layer, 0, 0)),
            _single((None, d_ff, d), lambda i: (layer, 0, 0)),
        ],
        out_specs=[pl.BlockSpec((tm, d), a_map), pl.BlockSpec((tm, d), b_map)],
        out_shape=[jax.ShapeDtypeStruct(xa.shape, F32), jax.ShapeDtypeStruct(xb.shape, F32)],
        scratch_shapes=[pltpu.VMEM((tm, d_ff), BF16)],
        compiler_params=_cparams("arbitrary"),
        name="swiglu_ffn",
    )(xa, xb, mod, g, w1, w3, w2)


def _lane_half_swap(x):
    n = x.shape[-1]
    lane = lax.broadcasted_iota(jnp.int32, (1, n), 1)
    return jnp.where(lane % LANES < LANES // 2, pltpu.roll(x, n - LANES // 2, axis=1),
                     pltpu.roll(x, LANES // 2, axis=1))


def _quad_pairs(t0, t1, t2, t3):
    lo = lax.broadcasted_iota(jnp.int32, (1, t0.shape[-1]), 1) % LANES < LANES // 2
    s0, s1, s2, s3 = (_lane_half_swap(t) for t in (t0, t1, t2, t3))
    even = (jnp.where(lo, t0, s1), jnp.where(lo, t2, s3))
    odd = (jnp.where(lo, s0, t1), jnp.where(lo, s2, t3))
    return even, odd


def _ssm_kernel(x_ref, mod_ref, g_ref, dskip_ref, win_ref, wcar_ref, kin_ref, a4_ref, h0_ref, wglu_ref, *rest,
                seq, n_seg, with_state):
    if with_state:
        o_ref, sre_ref, sim_ref = rest[:3]
    else:
        o_ref = rest[0]
    u_ref, y_ref, z_ref, sp_ref, f_ref, e_ref = rest[-6:]
    d = x_ref.shape[2]
    cps = win_ref.shape[0]
    nsq = a4_ref.shape[-1] // 2
    nst = cps * nsq
    nq = seq // QUAD
    steps_per_dir = (d // Q_CH) // cps
    tb = 4 * QUAD
    qb = tb // QUAD * SUBLANES
    dk = pl.program_id(1)
    kb = (dk % steps_per_dir) * cps
    fwd = dk // steps_per_dir == 0

    @pl.when(dk == 0)
    def _prepare():
        gain, shift, scale = g_ref[...], mod_ref[0], mod_ref[1]

        def body(i, carry):
            t0 = pl.multiple_of(i * tb, tb)
            xt = jnp.swapaxes(x_ref[:, pl.ds(t0, tb), :], 0, 1)
            h = _norm_mod(xt, gain, shift, scale).reshape(tb // QUAD, QUAD, SUBLANES, d)
            rows = pl.ds(pl.multiple_of(i * qb, qb), qb)
            halves = _quad_pairs(*(h[:, s].reshape(qb, d) for s in range(QUAD)))
            for m in range(d // LANES):
                for parity, (v01, v23) in enumerate(halves):
                    k = 2 * m + parity
                    sl = slice(m * LANES, (m + 1) * LANES)
                    u_ref[k, rows, :LANES] = v01[:, sl].astype(BF16)
                    u_ref[k, rows, LANES:] = v23[:, sl].astype(BF16)
                    y_ref[k, rows, :LANES] = v01[:, sl] * dskip_ref[k, :, :LANES]
                    y_ref[k, rows, LANES:] = v23[:, sl] * dskip_ref[k, :, LANES:]
            return carry

        lax.fori_loop(0, seq // tb, body, 0)

    a_re = jnp.broadcast_to(jnp.concatenate([a4_ref[c][:, :nsq] for c in range(cps)], axis=-1), (SUBLANES, nst))
    a_im = jnp.broadcast_to(jnp.concatenate([a4_ref[c][:, nsq:] for c in range(cps)], axis=-1), (SUBLANES, nst))

    def chunk_cols(ref, c):
        return jnp.concatenate([ref[:, c * nsq:(c + 1) * nsq], ref[:, nst + c * nsq:nst + (c + 1) * nsq]], axis=-1)

    def scan(state, store, reverse):
        s_re, s_im = state
        pairs = range(nq // 2)
        for m in (reversed(pairs) if reverse else pairs):
            entering = {}
            for j in ((2 * m + 1, 2 * m) if reverse else (2 * m, 2 * m + 1)):
                rsl = slice(j * SUBLANES, (j + 1) * SUBLANES)
                entering[j] = (s_re, s_im)
                s_re, s_im = (a_re * s_re - a_im * s_im + z_ref[rsl, :nst],
                              a_re * s_im + a_im * s_re + z_ref[rsl, nst:])
            if store:
                psl = slice(2 * m * SUBLANES, (2 * m + 2) * SUBLANES)
                lo, hi = entering[2 * m], entering[2 * m + 1]
                sp_ref[psl, :nst] = jnp.concatenate([lo[0], hi[0]], axis=0).astype(BF16)
                sp_ref[psl, nst:] = jnp.concatenate([lo[1], hi[1]], axis=0).astype(BF16)
        return s_re, s_im

    def run_direction(reverse):
        for c in range(cps):
            z = jnp.dot(u_ref[kb + c], win_ref[c], preferred_element_type=F32)
            z_ref[:, c * nsq:(c + 1) * nsq] = z[:, :nsq]
            z_ref[:, nst + c * nsq:nst + (c + 1) * nsq] = z[:, nsq:]
        start = (jnp.concatenate([h0_ref[c][:, :nsq] for c in range(cps)], axis=-1),
                 jnp.concatenate([h0_ref[c][:, nsq:] for c in range(cps)], axis=-1))
        if n_seg > 1:
            zero = jnp.zeros((SUBLANES, nst), F32)
            e_re, e_im = scan((zero, zero), False, reverse)
            e_ref[:, :nst] = e_re
            e_ref[:, nst:] = e_im
            f_ref[:, :nst] = start[0]
            f_ref[:, nst:] = start[1]
            p_re, p_im = a_re[:1], a_im[:1]
            for _ in range(int(math.log2(nq))):
                p_re, p_im = p_re * p_re - p_im * p_im, 2.0 * p_re * p_im
            for b in range(SUBLANES // n_seg):
                for s in (range(n_seg - 2, -1, -1) if reverse else range(1, n_seg)):
                    dst = b * n_seg + s
                    src = dst + 1 if reverse else dst - 1
                    f_re, f_im = f_ref[src:src + 1, :nst], f_ref[src:src + 1, nst:]
                    f_ref[dst:dst + 1, :nst] = p_re * f_re - p_im * f_im + e_ref[src:src + 1, :nst]
                    f_ref[dst:dst + 1, nst:] = p_re * f_im + p_im * f_re + e_ref[src:src + 1, nst:]
            start = (f_ref[:, :nst], f_ref[:, nst:])
        fin_re, fin_im = scan(start, True, reverse)
        if with_state:
            for c in range(cps):
                sre_ref[c] = fin_re[:, c * nsq:(c + 1) * nsq]
                sim_ref[c] = fin_im[:, c * nsq:(c + 1) * nsq]
        for c in range(cps):
            y_ref[kb + c] = (y_ref[kb + c]
                             + _dot_nt(chunk_cols(sp_ref, c), wcar_ref[c])
                             + jnp.dot(u_ref[kb + c], kin_ref[c], preferred_element_type=F32))

    pl.when(fwd)(functools.partial(run_direction, False))
    pl.when(jnp.logical_not(fwd))(functools.partial(run_direction, True))

    @pl.when(dk == 2 * steps_per_dir - 1)
    def _finish():
        gate = mod_ref[2]
        fqd = min(FINISH_QUADS, nq // 8)
        fq = fqd * SUBLANES
        ft = fqd * QUAD
        n_blk = nq // fqd

        def glu_in(i, b):
            rows = pl.ds(pl.multiple_of(i * fq, fq), fq)
            gathered = [jnp.concatenate([y_ref[2 * m + parity, rows, half] for m in range(d // LANES)], axis=-1)
                        for half in (slice(0, LANES), slice(LANES, 2 * LANES)) for parity in (0, 1)]
            (y0, y2), (y1, y3) = _quad_pairs(*gathered)
            y = jnp.stack([t.reshape(fqd, SUBLANES, d) for t in (y0, y1, y2, y3)], axis=1)
            z = jax.nn.gelu(y.reshape(ft * SUBLANES, d)).astype(BF16)
            z_ref[b * ft * SUBLANES:(b + 1) * ft * SUBLANES, :] = jnp.dot(z, wglu_ref[...],
                                                                        preferred_element_type=F32)

        def glu_out(i, b):
            gl = z_ref[b * ft * SUBLANES:(b + 1) * ft * SUBLANES, :]
            out = (gl[:, :d] * jax.nn.sigmoid(gl[:, d:])).reshape(ft, SUBLANES, d) * gate
            for j in range(ft // tb):
                t0 = pl.multiple_of(i * ft + j * tb, tb)
                upd = jnp.swapaxes(out[j * tb:(j + 1) * tb], 0, 1)
                o_ref[:, pl.ds(t0, tb), :] = x_ref[:, pl.ds(t0, tb), :] + upd

        def pair(jj, carry):
            i = 2 * jj
            glu_in(i + 1, 1)
            glu_out(i, 0)
            glu_in(i + 2, 0)
            glu_out(i + 1, 1)
            return carry

        glu_in(0, 0)
        lax.fori_loop(0, n_blk // 2 - 1, pair, 0)
        glu_in(n_blk - 1, 1)
        glu_out(n_blk - 2, 0)
        glu_out(n_blk - 1, 1)


def _ssm_mixer(x, mod8, g, dskip, w_in, w_car, k_in, a4, h0, wglu, layer, *, seq, n_seg, with_state):
    n_seq, _, d = x.shape
    n_groups = n_seq // SUBLANES
    n_chunks = d // Q_CH
    nsq = a4.shape[-1] // 2
    cps = CHUNKS_PER_STEP
    steps_per_dir = n_chunks // cps
    rows_q = seq // QUAD * SUBLANES
    assert cps * nsq == d and seq % (32 * QUAD) == 0
    dk_map = lambda gi, dk: (layer, dk // steps_per_dir, dk % steps_per_dir, 0, 0)
    seq_map = lambda gi, dk: (dk // steps_per_dir, dk % steps_per_dir, gi, 0)
    x_spec = _single((SUBLANES, seq, d), lambda gi, dk: (gi, 0, 0))
    out_specs = [x_spec]
    out_shape = [jax.ShapeDtypeStruct(x.shape, F32)]
    if with_state:
        st_spec = pl.BlockSpec((None, cps, SUBLANES, nsq), seq_map)
        out_specs += [st_spec, st_spec]
        out_shape += [jax.ShapeDtypeStruct((2, n_chunks, n_seq, nsq), F32)] * 2
    return pl.pallas_call(
        functools.partial(_ssm_kernel, seq=seq, n_seg=n_seg, with_state=with_state),
        grid=(n_groups, 2 * steps_per_dir),
        in_specs=[
            pl.BlockSpec((SUBLANES, seq, d), lambda gi, dk: (gi, 0, 0)),
            pl.BlockSpec((N_MOD, SUBLANES, d), lambda gi, dk: (0, 0, 0)),
            pl.BlockSpec((1, d), lambda gi, dk: (0, 0)),
            pl.BlockSpec((n_chunks, 1, MXU_DIM), lambda gi, dk: (0, 0, 0)),
            pl.BlockSpec((None, None, cps) + w_in.shape[-2:], dk_map),
            pl.BlockSpec((None, None, cps) + w_car.shape[-2:], dk_map),
            pl.BlockSpec((None, None, cps, MXU_DIM, MXU_DIM), dk_map),
            pl.BlockSpec((None, None, cps, 1, 2 * nsq), dk_map),
            pl.BlockSpec((None, cps, SUBLANES, 2 * nsq), seq_map),
            _single((None, d, 2 * d), lambda gi, dk: (layer, 0, 0)),
        ],
        out_specs=out_specs,
        out_shape=out_shape,
        scratch_shapes=[
            pltpu.VMEM((n_chunks, rows_q, MXU_DIM), BF16),
            pltpu.VMEM((n_chunks, rows_q, MXU_DIM), F32),
            pltpu.VMEM((rows_q, 2 * cps * nsq), F32),
            pltpu.VMEM((rows_q, 2 * cps * nsq), BF16),
            pltpu.VMEM((SUBLANES, 2 * cps * nsq), F32),
            pltpu.VMEM((SUBLANES, 2 * cps * nsq), F32),
        ],
        compiler_params=_cparams("arbitrary", "arbitrary"),
        name="s5_mixer",
    )(x, mod8, g, dskip, w_in, w_car, k_in, a4, h0, wglu)


def _ssm_params(lam_re, lam_im, log_step, b_re, b_im, c_re, c_im):
    lam_re, lam_im = lam_re.astype(F32), lam_im.astype(F32)
    step = jnp.exp(log_step.astype(F32))[..., None]
    mag = jnp.exp(lam_re * step)
    a_re, a_im = mag * jnp.cos(lam_im * step), mag * jnp.sin(lam_im * step)
    den = lam_re * lam_re + lam_im * lam_im
    nr, ni = a_re - 1.0, a_im
    f_re = (nr * lam_re + ni * lam_im) / den
    f_im = (ni * lam_re - nr * lam_im) / den
    bt_re, bt_im = jnp.swapaxes(b_re.astype(F32), -1, -2), jnp.swapaxes(b_im.astype(F32), -1, -2)
    bb = (f_re[..., None, :] * bt_re - f_im[..., None, :] * bt_im,
          f_re[..., None, :] * bt_im + f_im[..., None, :] * bt_re)
    cc = (c_re.astype(F32), c_im.astype(F32))

    def cmul(x, y):
        return x[0] * y[0] - x[1] * y[1], x[0] * y[1] + x[1] * y[0]

    pw = [(jnp.ones_like(a_re), jnp.zeros_like(a_re))]
    for _ in range(QUAD):
        pw.append(cmul(pw[-1], (a_re, a_im)))

    def by_dir(n_fwd, n_bwd):
        return tuple(jnp.stack([pw[n_fwd][i][:, 0], pw[n_bwd][i][:, 1]], axis=1)[..., None, :] for i in range(2))

    n_layers, _, n_groups, c, p = bb[0].shape
    gq = Q_CH // c
    kk = n_groups // gq
    nsq = gq * p
    lead = (n_layers, 2, kk)

    def pack(ws, sign):
        def rows(i):
            t = jnp.stack([w[i] for w in ws], axis=2).reshape(n_layers, 2, QUAD, kk, gq, c, p)
            return t.transpose(0, 1, 3, 2, 4, 5, 6).reshape(*lead, MXU_DIM, p)
        return jnp.concatenate([rows(0), sign * rows(1)], axis=-1).astype(BF16)

    w_in = pack([cmul(by_dir(QUAD - 1 - s, s), bb) for s in range(QUAD)], 1.0)
    w_car = pack([cmul(cc, by_dir(s + 1, QUAD - s)) for s in range(QUAD)], -1.0)

    gc = gq * c
    same_g = (np.arange(gc) // c)[:, None] == (np.arange(gc) // c)[None, :]
    cb = [cmul(cc, tuple(t[..., None, :] for t in pw[n])) for n in range(QUAD)]
    cb = jnp.concatenate([jnp.concatenate([t[0] for t in cb], axis=-2),
                          -jnp.concatenate([t[1] for t in cb], axis=-2)], axis=-1)
    taps = jnp.einsum('ldgkp,ldgcp->ldgck', cb, jnp.concatenate(bb, axis=-1), precision=lax.Precision.HIGHEST)
    tiles = []
    for n in range(QUAD):
        tap = taps[..., n * c:(n + 1) * c]
        tiles.append(jnp.where(same_g, jnp.tile(tap.reshape(*lead, gc, c), (1, 1, 1, 1, gq)), 0.0))
    zero = jnp.zeros_like(tiles[0][:, 0])

    def tile_of(d, s, t):
        n = t - s if d == 0 else s - t
        return tiles[n][:, d] if n >= 0 else zero

    k_in = jnp.stack([jnp.concatenate([jnp.concatenate([tile_of(d, s, t) for t in range(QUAD)], axis=-1)
                                       for s in range(QUAD)], axis=-2) for d in range(2)], axis=1)

    a4 = jnp.concatenate([pw[QUAD][0].reshape(*lead, 1, nsq), pw[QUAD][1].reshape(*lead, 1, nsq)], axis=-1)
    return w_in.astype(BF16), w_car.astype(BF16), k_in.astype(BF16), a4


def _block_diag_kernel(win_ref, wcar_ref, win_o_ref, wcar_o_ref, *, gq):
    pp = win_ref.shape[-1] // 2
    nsq = gq * pp
    src = lax.broadcasted_iota(jnp.int32, (2 * pp, 2 * nsq), 0)
    col = lax.broadcasted_iota(jnp.int32, (2 * pp, 2 * nsq), 1)
    spread = ((col % pp == src % pp) & ((col >= nsq) == (src >= pp))).astype(BF16)
    row_g = (lax.broadcasted_iota(jnp.int32, (MXU_DIM, 2 * nsq), 0) // (Q_CH // gq)) % gq
    col_g = (lax.broadcasted_iota(jnp.int32, (MXU_DIM, 2 * nsq), 1) % nsq) // pp
    for src_ref, dst_ref in ((win_ref, win_o_ref), (wcar_ref, wcar_o_ref)):
        for c in range(src_ref.shape[0]):
            w = jnp.dot(src_ref[c], spread, preferred_element_type=F32)
            dst_ref[c] = jnp.where(row_g == col_g, w, 0.0).astype(BF16)


def _block_diag(w_in, w_car, *, gq):
    n_layers, n_dir, kk, rows, two_p = w_in.shape
    cps = CHUNKS_PER_STEP
    spec_in = pl.BlockSpec((None, None, cps, rows, two_p), lambda l, dr, k: (l, dr, k, 0, 0))
    spec_out = pl.BlockSpec((None, None, cps, rows, gq * two_p), lambda l, dr, k: (l, dr, k, 0, 0))
    out = jax.ShapeDtypeStruct((n_layers, n_dir, kk, rows, gq * two_p), BF16)
    return pl.pallas_call(
        functools.partial(_block_diag_kernel, gq=gq),
        grid=(n_layers, n_dir, kk // cps),
        in_specs=[spec_in, spec_in],
        out_specs=[spec_out, spec_out],
        out_shape=[out, out],
        compiler_params=_cparams("arbitrary", "arbitrary", "arbitrary"),
        name="s5_block_diag",
    )(w_in, w_car)


def kernel(x_prompt, x_sample, cache_k, cache_v, state_ssm_re, state_ssm_im, c, c_ctx, norm_mix, norm_ffn, ada_w, ada_b, na_w_qkv, na_w_o, na_q_gain, na_k_gain, na_rpb, ssm_lambda_re, ssm_lambda_im, ssm_log_step, ssm_b_re, ssm_b_im, ssm_c_re, ssm_c_im, ssm_d, ssm_w_glu, ffn_w1, ffn_w3, ffn_w2):
    batch, seq, d = x_prompt.shape
    dec_batch, dec_seq, _ = x_sample.shape
    depth = ada_w.shape[0]
    heads, head_dim = cache_k.shape[3], cache_k.shape[4]
    past = cache_k.shape[2]
    npairs = d // HEAD_PAIR
    d_ff = ffn_w1.shape[-1]
    n_dir = state_ssm_re.shape[2]
    n_seg = dec_seq // seq
    assert seq % (FINISH_QUADS * QUAD) == 0 and dec_seq % seq == 0 and SUBLANES % n_seg == 0
    assert dec_batch * n_seg == SUBLANES and batch % SUBLANES == 0 and seq & (seq - 1) == 0
    assert d_ff % MXU_DIM == 0 and d % (Q_CH * CHUNKS_PER_STEP) == 0 and head_dim * 2 == HEAD_PAIR
    assert Q_CH * 2 == LANES and Q_CH % SSM_GROUP == 0

    cond8 = jnp.zeros((SUBLANES, d), F32).at[0].set(c_ctx).at[1:1 + dec_batch].set(c)
    mod = _modulation(cond8, ada_w, ada_b).reshape(depth, SUBLANES, N_MOD, d)

    tm = math.gcd(512, dec_seq)
    prompt_group = lambda i: 0
    sample_group = lambda i: 1 + (i * tm) // dec_seq

    head_mean = jnp.kron(jnp.eye(MXU_DIM // head_dim, dtype=F32),
                         jnp.full((head_dim, head_dim), 1.0 / head_dim, F32)).astype(BF16)

    w_qkv, w_o, w_glu = na_w_qkv.astype(BF16), na_w_o.astype(BF16), ssm_w_glu.astype(BF16)
    w1, w3, w2 = ffn_w1.astype(BF16), ffn_w3.astype(BF16), ffn_w2.astype(BF16)
    n_na = cache_k.shape[1]
    ckp = cache_k.astype(BF16).reshape(dec_batch, n_na, past, npairs, HEAD_PAIR).transpose(1, 0, 3, 2, 4)
    cvp = cache_v.astype(BF16).reshape(dec_batch, n_na, past, npairs, HEAD_PAIR).transpose(1, 0, 3, 2, 4)
    strip = _rpb_strip(na_rpb)
    w_in, w_car, k_in, a4 = _ssm_params(ssm_lambda_re, ssm_lambda_im, ssm_log_step, ssm_b_re, ssm_b_im,
                                        ssm_c_re, ssm_c_im)
    s5_w = (*_block_diag(w_in, w_car, gq=Q_CH // SSM_GROUP), k_in, a4)

    xp = x_prompt.reshape(batch * seq, d)
    xs = x_sample.reshape(dec_batch * dec_seq, d)
    new_cache, new_sre, new_sim = None, [], []
    for i in range(depth):
        j = i // 2
        g_mix = norm_mix[i].reshape(1, d)
        if i % 2 == 0:
            q_gain = jnp.tile(na_q_gain[j], heads).reshape(1, d)
            k_gain = jnp.tile(na_k_gain[j], heads).reshape(1, d)
            qp, kp, vp, *new_cache = _qkv(xp, mod[i], prompt_group, g_mix, w_qkv, j, n_na, q_gain, k_gain,
                                          head_mean, new_cache, tm=tm, head_dim=head_dim, seq=seq, with_cache=True)
            xp = _ctx_attention(qp, kp, vp, xp, mod[i], w_o, j, seq=seq, seqs_per_step=math.gcd(4, batch))
            qs, ks, vs = _qkv(xs, mod[i], sample_group, g_mix, w_qkv, j, n_na, q_gain, k_gain, head_mean, None,
                              tm=tm, head_dim=head_dim, seq=seq, with_cache=False)
            xs = _na_attention(qs, ks, vs, ckp, cvp, strip, xs, mod[i], w_o, j, n_batch=dec_batch)
        else:
            n_chunks = d // Q_CH
            dskip = jnp.tile(ssm_d[j].astype(F32).reshape(n_chunks, 1, Q_CH), (1, 1, QUAD))
            ns = a4.shape[-1] // 2
            mod_p = jnp.broadcast_to(mod[i, 0][:, None, :], (N_MOD, SUBLANES, d))
            h0_p = jnp.zeros((n_dir, n_chunks, batch, 2 * ns), F32)
            xp, sre, sim = _ssm_mixer(xp.reshape(batch, seq, d), mod_p, g_mix, dskip, *s5_w, h0_p, w_glu, j,
                                      seq=seq, n_seg=1, with_state=True)
            xp = xp.reshape(batch * seq, d)
            new_sre.append(sre.transpose(2, 0, 1, 3).reshape(batch, n_dir, d // SSM_GROUP, SSM_STATE))
            new_sim.append(sim.transpose(2, 0, 1, 3).reshape(batch, n_dir, d // SSM_GROUP, SSM_STATE))
            mod_s = jnp.repeat(mod[i, 1:1 + dec_batch], n_seg, axis=0).transpose(1, 0, 2)
            s_re = state_ssm_re[:, j].astype(F32).reshape(dec_batch, n_dir, n_chunks, ns)
            s_im = state_ssm_im[:, j].astype(F32).reshape(dec_batch, n_dir, n_chunks, ns)
            s0 = jnp.concatenate([s_re, s_im], axis=-1)
            h0_s = jnp.zeros((dec_batch, n_seg, n_dir, n_chunks, 2 * ns), F32)
            h0_s = h0_s.at[:, 0, 0].set(s0[:, 0]).at[:, n_seg - 1, 1].set(s0[:, 1])
            h0_s = h0_s.reshape(dec_batch * n_seg, n_dir, n_chunks, 2 * ns).transpose(1, 2, 0, 3)
            (xs,) = _ssm_mixer(xs.reshape(dec_batch * n_seg, seq, d), mod_s, g_mix, dskip, *s5_w, h0_s, w_glu, j,
                               seq=seq, n_seg=n_seg, with_state=False)
            xs = xs.reshape(dec_batch * dec_seq, d)
        g_ffn = norm_ffn[i].reshape(1, d)
        xp, xs = _ffn(xp, xs, mod[i], sample_group, g_ffn, w1, w3, w2, i, tm=tm)
    new_k, new_v = (t.reshape(batch, n_na, heads, head_dim, seq).transpose(0, 1, 4, 2, 3) for t in new_cache)
    return (xp.reshape(batch, seq, d), xs.reshape(dec_batch, dec_seq, d), new_k, new_v,
            jnp.stack(new_sre, axis=1), jnp.stack(new_sim, axis=1))
```
